```python
import math
import jax
import jax.numpy as jnp
from jax import lax
import numpy as np

D_MODEL = 1024
BATCH = 4
SEQ = 4096
DEPTH = 4
DEC_BATCH = 32
DEC_SEQ = 64
PAST_LEN = 2048

CHUNK = 64
N_MIXERS = 3
N_A = (DEPTH + 2) // 3
N_B = (DEPTH + 1) // 3
N_C = DEPTH // 3
H_A = 8
DK_A = D_MODEL // H_A
DV_A = D_MODEL // H_A
REC_BLOCK = 32
H_B = 8
DH_B = 64
Q_BLOCK = 128
REL_BUCKETS = 32
REL_MAX_DIST = 1024
D_RNN = D_MODEL
N_BLK_C = 4
BLK_C = D_RNN // N_BLK_C
CONV_W = 4
C_RG = 8.0
N_EXPERTS = 64
D_EXPERT = 128
TOP_K = 8
N_GROUPS = 8
TOPK_GROUPS = 4
ROUTED_SCALE = 2.5
D_SHARED = D_EXPERT
TOK_BLOCK = 256
ALPHA = (2 * DEPTH) ** 0.25
BETA = (8 * DEPTH) ** -0.25
LN_EPS = 1e-5
RMS_EPS = 1e-6

kernel_name = 'hybrid_stream_encoder_step'

F32 = jnp.float32


def layer_norm(x, g, b):
    xf = x.astype(F32)
    mu = jnp.mean(xf, -1, keepdims=True)
    var = jnp.mean(jnp.square(xf - mu), -1, keepdims=True)
    return ((xf - mu) * lax.rsqrt(var + LN_EPS) * g.astype(F32) + b.astype(F32)).astype(x.dtype)


def rms_norm(x, g):
    xf = x.astype(F32)
    return xf * lax.rsqrt(jnp.mean(xf * xf, -1, keepdims=True) + RMS_EPS) * g.astype(F32)


def hgrn2_recurrence(q, k, v, logf, s0):
    B, S, H, DK = q.shape
    DV = v.shape[-1]
    C = REC_BLOCK if S % REC_BLOCK == 0 else S
    n = S // C

    def to_chunks(t):
        return t.reshape(B, n, C, H, t.shape[-1]).transpose(1, 0, 3, 2, 4)

    causal = jnp.tril(jnp.ones((C, C), bool))[:, :, None]

    def step(s_prev, inp):
        qc, kc, vc, gc = inp
        b = jnp.cumsum(gc, axis=2)
        o = jnp.einsum('bhtk,bhkv->bhtv', qc * jnp.exp(b), s_prev)
        gap = jnp.where(causal, b[:, :, :, None, :] - b[:, :, None, :, :], -jnp.inf)
        scores = jnp.einsum('bhtk,bhsk,bhtsk->bhts', qc, kc, jnp.exp(gap))
        o = o + jnp.einsum('bhts,bhsv->bhtv', scores, vc)
        b_end = b[:, :, -1:, :]
        s_new = (jnp.exp(b_end[:, :, 0, :, None]) * s_prev
                 + jnp.einsum('bhsk,bhsv->bhkv', kc * jnp.exp(b_end - b), vc))
        return s_new, o

    s_fin, o = lax.scan(step, s0, (to_chunks(q), to_chunks(k), to_chunks(v), to_chunks(logf)))
    o = o.transpose(1, 0, 3, 2, 4).reshape(B, S, H, DV)
    return o, s_fin


def hgrn2_mixer(x, s0, w_in, gn_g, w_out, lb):
    B, S, _ = x.shape
    q, fz, inp, g = jnp.split(x @ w_in, 4, axis=-1)
    fz = fz.astype(F32)
    lb = jnp.clip(lb.astype(F32), 0.0, 1.0)
    log_f = jnp.logaddexp(jnp.log(lb), jnp.log1p(-lb) + jax.nn.log_sigmoid(fz))
    k = (1.0 - lb) * jax.nn.sigmoid(-fz)
    hd = lambda t: t.reshape(B, S, H_A, -1)
    o, s_fin = hgrn2_recurrence(hd(jax.nn.silu(q).astype(F32)), hd(k), hd(inp.astype(F32)),
                                hd(log_f), s0.astype(F32))
    o = rms_norm(o, gn_g) * hd(jax.nn.silu(g.astype(F32)))
    y = o.reshape(B, S, H_A * DV_A).astype(x.dtype) @ w_out
    return y, s_fin.astype(x.dtype)


def lambda_init(layer_idx):
    return 0.8 - 0.6 * math.exp(-0.3 * layer_idx)


def rel_bucket(rel):
    nb = REL_BUCKETS // 2
    max_exact = nb // 2
    ret = jnp.where(rel > 0, nb, 0)
    n = jnp.abs(rel)
    nf = jnp.maximum(n, 1).astype(F32)
    large = max_exact + (jnp.log(nf / max_exact) / math.log(REL_MAX_DIST / max_exact)
                         * (nb - max_exact)).astype(jnp.int32)
    large = jnp.minimum(large, nb - 1)
    return ret + jnp.where(n < max_exact, n, large)


def diff_attn_block(q, k, v, q_pos, k_pos, lam, rel_bias):
    scale = DH_B ** -0.5
    s1 = jnp.einsum('bqhd,bkhd->bhqk', q[..., :DH_B], k[..., :DH_B]).astype(F32) * scale
    s2 = jnp.einsum('bqhd,bkhd->bhqk', q[..., DH_B:], k[..., DH_B:]).astype(F32) * scale
    bias = jnp.transpose(rel_bias[rel_bucket(k_pos[None, :] - q_pos[:, None])], (2, 0, 1)).astype(F32)
    visible = (k_pos[None, :] // CHUNK) <= (q_pos[:, None] // CHUNK)

    def probs(s):
        return jax.nn.softmax(jnp.where(visible, s + bias, -1e30), axis=-1)

    attn = probs(s1) - lam * probs(s2)
    return jnp.einsum('bhqk,bkhd->bqhd', attn, v.astype(F32))


def diff_attn_mixer(x, k_past, v_past, w_qkv, lam_q1, lam_k1, lam_q2, lam_k2, subln_g, w_out,
                    rel_bias, lam_init):
    B, S, _ = x.shape
    q, k, v = [t.reshape(B, S, H_B, 2 * DH_B) for t in jnp.split(x @ w_qkv, 3, axis=-1)]
    lam = (jnp.exp(jnp.sum(lam_q1.astype(F32) * lam_k1.astype(F32)))
           - jnp.exp(jnp.sum(lam_q2.astype(F32) * lam_k2.astype(F32))) + lam_init)
    if k_past is None:
        k_all, v_all = k, v
    else:
        k_all = jnp.concatenate([k_past.astype(k.dtype), k], axis=1)
        v_all = jnp.concatenate([v_past.astype(v.dtype), v], axis=1)
    t_k = k_all.shape[1]
    k_pos = jnp.arange(t_k)
    q_pos = (t_k - S) + jnp.arange(S)
    if S % Q_BLOCK == 0 and S > Q_BLOCK:
        nb = S // Q_BLOCK
        qb = q.reshape(B, nb, Q_BLOCK, H_B, 2 * DH_B).transpose(1, 0, 2, 3, 4)
        pb = q_pos.reshape(nb, Q_BLOCK)
        o = lax.map(lambda a: diff_attn_block(a[0], k_all, v_all, a[1], k_pos, lam, rel_bias), (qb, pb))
        o = o.transpose(1, 0, 2, 3, 4).reshape(B, S, H_B, 2 * DH_B)
    else:
        o = diff_attn_block(q, k_all, v_all, q_pos, k_pos, lam, rel_bias)
    o = rms_norm(o, subln_g) * (1.0 - lam_init)
    y = o.reshape(B, S, H_B * 2 * DH_B).astype(x.dtype) @ w_out
    return y, k, v


def rglru_mixer(x, conv_buf, h0, w_in, conv_w, conv_b, w_a, b_a, w_x, b_x, lam, w_out):
    B, S, _ = x.shape
    gate_branch, u = jnp.split(x @ w_in, 2, axis=-1)
    u_ext = jnp.concatenate([conv_buf.astype(u.dtype), u], axis=1)
    conv = conv_b
    for j in range(CONV_W):
        conv = conv + u_ext[:, j:j + S] * conv_w[j]
    ub = conv.reshape(B, S, N_BLK_C, BLK_C)
    r = jax.nn.sigmoid((jnp.einsum('bsnc,ncd->bsnd', ub, w_a) + b_a).astype(F32)).reshape(B, S, D_RNN)
    ig = jax.nn.sigmoid((jnp.einsum('bsnc,ncd->bsnd', ub, w_x) + b_x).astype(F32)).reshape(B, S, D_RNN)
    log_a = -C_RG * r * jax.nn.softplus(-lam.astype(F32))
    a = jnp.exp(log_a)
    xin = jnp.sqrt(-jnp.expm1(2.0 * log_a)) * ig * conv.astype(F32)
    xin = xin.at[:, 0].add(a[:, 0] * h0.astype(F32))

    def combine(lhs, rhs):
        return (lhs[0] * rhs[0], rhs[0] * lhs[1] + rhs[1])

    _, h = lax.associative_scan(combine, (a, xin), axis=1)
    y = (h * jax.nn.gelu(gate_branch.astype(F32))).astype(x.dtype) @ w_out
    return y, u_ext[:, -(CONV_W - 1):], h[:, -1].astype(x.dtype)


def moe_ffn(x, w_router, router_bias, w_in, w_down, ws_in, ws_down):
    B, S, D = x.shape
    n = B * S
    n_pad = -n % TOK_BLOCK
    xt = jnp.pad(x.reshape(n, D), ((0, n_pad), (0, 0)))
    e_per_g = N_EXPERTS // N_GROUPS

    def block(xb):
        t = xb.shape[0]
        score = jax.nn.sigmoid((xb @ w_router).astype(F32))
        biased = score + router_bias.astype(F32)
        grp_score = lax.top_k(biased.reshape(t, N_GROUPS, e_per_g), 2)[0].sum(-1)
        _, grp_idx = lax.top_k(grp_score, TOPK_GROUPS)
        grp_keep = jax.nn.one_hot(grp_idx, N_GROUPS, dtype=F32).sum(1) > 0
        keep = jnp.repeat(grp_keep, e_per_g, axis=1)
        _, e_idx = lax.top_k(jnp.where(keep, biased, -jnp.inf), TOP_K)
        w = jnp.take_along_axis(score, e_idx, axis=1)
        w = w / jnp.sum(w, -1, keepdims=True) * ROUTED_SCALE
        gates = jnp.einsum('tke,tk->te', jax.nn.one_hot(e_idx, N_EXPERTS, dtype=F32), w)
        hg, hu = jnp.split(jnp.einsum('td,edf->tef', xb, w_in), 2, axis=-1)
        act = jax.nn.silu(hg) * hu * gates[..., None].astype(hg.dtype)
        routed = jnp.einsum('tef,efd->td', act, w_down)
        sg, su = jnp.split(xb @ ws_in, 2, axis=-1)
        return routed + (jax.nn.silu(sg) * su) @ ws_down

    y = lax.map(block, xt.reshape(-1, TOK_BLOCK, D))
    return y.reshape(-1, D)[:n].reshape(B, S, D)


def setup_inputs(seed: int = 0) -> dict:
    key = jax.random.key(seed)
    keys = iter(jax.random.split(key, 48))
    D = D_MODEL

    def nrm(shape, scale):
        return scale * jax.random.normal(next(keys), shape, F32)

    u = jax.random.uniform(next(keys), (N_C, D_RNN), F32, minval=0.9, maxval=0.999)
    a_base = u ** (1.0 / C_RG)
    rglru_lam = jnp.log(a_base) - jnp.log1p(-a_base)
    return {
        'x_prompt': nrm((BATCH, SEQ, D), 1.0),
        'x_sample': nrm((DEC_BATCH, DEC_SEQ, D), 1.0),
        'state_hgrn_s': nrm((N_A, DEC_BATCH, H_A, DK_A, DV_A), 0.3),
        'cache_k': nrm((N_B, DEC_BATCH, PAST_LEN, H_B, 2 * DH_B), 1.0),
        'cache_v': nrm((N_B, DEC_BATCH, PAST_LEN, H_B, 2 * DH_B), 1.0),
        'state_rglru_conv': nrm((N_C, DEC_BATCH, CONV_W - 1, D_RNN), 1.0),
        'state_rglru_h': nrm((N_C, DEC_BATCH, D_RNN), 0.5),
        'hgrn_lb_logits': nrm((DEPTH, H_A * DK_A), 0.5),
        'hgrn_w_in': nrm((N_A, D, 4 * H_A * DK_A), D ** -0.5),
        'hgrn_gn_g': 1.0 + nrm((N_A, DV_A), 0.02),
        'hgrn_w_out': nrm((N_A, H_A * DV_A, D), BETA * (H_A * DV_A) ** -0.5),
        'rel_bias': nrm((REL_BUCKETS, H_B), 0.5),
        'attn_w_qkv': nrm((N_B, D, 3 * H_B * 2 * DH_B), D ** -0.5),
        'attn_lam_q1': nrm((N_B, DH_B), 0.1),
        'attn_lam_k1': nrm((N_B, DH_B), 0.1),
        'attn_lam_q2': nrm((N_B, DH_B), 0.1),
        'attn_lam_k2': nrm((N_B, DH_B), 0.1),
        'attn_subln_g': 1.0 + nrm((N_B, 2 * DH_B), 0.02),
        'attn_w_out': nrm((N_B, H_B * 2 * DH_B, D), BETA * (H_B * 2 * DH_B) ** -0.5),
        'rglru_w_in': nrm((N_C, D, 2 * D_RNN), D ** -0.5),
        'rglru_conv_w': nrm((N_C, CONV_W, D_RNN), CONV_W ** -0.5),
        'rglru_conv_b': nrm((N_C, D_RNN), 0.02),
        'rglru_w_a': nrm((N_C, N_BLK_C, BLK_C, BLK_C), BLK_C ** -0.5),
        'rglru_b_a': nrm((N_C, N_BLK_C, BLK_C), 0.02),
        'rglru_w_x': nrm((N_C, N_BLK_C, BLK_C, BLK_C), BLK_C ** -0.5),
        'rglru_b_x': nrm((N_C, N_BLK_C, BLK_C), 0.02),
        'rglru_lam': rglru_lam,
        'rglru_w_out': nrm((N_C, D_RNN, D), BETA * D_RNN ** -0.5),
        'ln1_g': 1.0 + nrm((DEPTH, D), 0.02),
        'ln1_b': nrm((DEPTH, D), 0.02),
        'ln2_g': 1.0 + nrm((DEPTH, D), 0.02),
        'ln2_b': nrm((DEPTH, D), 0.02),
        'moe_w_router': nrm((DEPTH, D, N_EXPERTS), D ** -0.5),
        'moe_router_bias': nrm((DEPTH, N_EXPERTS), 0.01),
        'moe_w_in': nrm((DEPTH, N_EXPERTS, D, 2 * D_EXPERT), D ** -0.5),
        'moe_w_down': nrm((DEPTH, N_EXPERTS, D_EXPERT, D), BETA * D_EXPERT ** -0.5),
        'moe_ws_in': nrm((DEPTH, D, 2 * D_SHARED), D ** -0.5),
        'moe_ws_down': nrm((DEPTH, D_SHARED, D), BETA * D_SHARED ** -0.5),
    }


def reference(x_prompt, x_sample, state_hgrn_s, cache_k, cache_v, state_rglru_conv, state_rglru_h,
              hgrn_lb_logits, hgrn_w_in, hgrn_gn_g, hgrn_w_out,
              rel_bias, attn_w_qkv, attn_lam_q1, attn_lam_k1, attn_lam_q2, attn_lam_k2,
              attn_subln_g, attn_w_out,
              rglru_w_in, rglru_conv_w, rglru_conv_b, rglru_w_a, rglru_b_a, rglru_w_x, rglru_b_x,
              rglru_lam, rglru_w_out,
              ln1_g, ln1_b, ln2_g, ln2_b,
              moe_w_router, moe_router_bias, moe_w_in, moe_w_down, moe_ws_in, moe_ws_down):
    lb_p = jax.nn.softmax(hgrn_lb_logits.astype(F32), axis=0)
    lower_bounds = jnp.cumsum(lb_p, axis=0) - lb_p[0]
    xp, xs = x_prompt, x_sample
    bp = xp.shape[0]
    hs_p, hs_s = [], []
    k_p, v_p, k_s, v_s = [], [], [], []
    cv_p, h_p, cv_s, h_s = [], [], [], []
    for i in range(DEPTH):
        j = i // N_MIXERS
        kind = i % N_MIXERS
        if kind == 0:
            w = (hgrn_w_in[j], hgrn_gn_g[j], hgrn_w_out[j], lower_bounds[i])
            s0 = jnp.zeros((bp, H_A, DK_A, DV_A), xp.dtype)
            yp, sp = hgrn2_mixer(xp, s0, *w)
            ys, ss = hgrn2_mixer(xs, state_hgrn_s[j], *w)
            hs_p.append(sp)
            hs_s.append(ss)
        elif kind == 1:
            w = (attn_w_qkv[j], attn_lam_q1[j], attn_lam_k1[j], attn_lam_q2[j], attn_lam_k2[j],
                 attn_subln_g[j], attn_w_out[j], rel_bias, lambda_init(i))
            yp, kp_new, vp_new = diff_attn_mixer(xp, None, None, *w)
            ys, ks_new, vs_new = diff_attn_mixer(xs, cache_k[j], cache_v[j], *w)
            k_p.append(kp_new)
            v_p.append(vp_new)
            k_s.append(ks_new)
            v_s.append(vs_new)
        else:
            w = (rglru_w_in[j], rglru_conv_w[j], rglru_conv_b[j], rglru_w_a[j], rglru_b_a[j],
                 rglru_w_x[j], rglru_b_x[j], rglru_lam[j], rglru_w_out[j])
            buf0 = jnp.zeros((bp, CONV_W - 1, D_RNN), xp.dtype)
            h00 = jnp.zeros((bp, D_RNN), xp.dtype)
            yp, cbp, hlp = rglru_mixer(xp, buf0, h00, *w)
            ys, cbs, hls = rglru_mixer(xs, state_rglru_conv[j], state_rglru_h[j], *w)
            cv_p.append(cbp)
            h_p.append(hlp)
            cv_s.append(cbs)
            h_s.append(hls)
        xp = layer_norm(ALPHA * xp + yp, ln1_g[i], ln1_b[i])
        xs = layer_norm(ALPHA * xs + ys, ln1_g[i], ln1_b[i])
        mw = (moe_w_router[i], moe_router_bias[i], moe_w_in[i], moe_w_down[i], moe_ws_in[i], moe_ws_down[i])
        xp = layer_norm(ALPHA * xp + moe_ffn(xp, *mw), ln2_g[i], ln2_b[i])
        xs = layer_norm(ALPHA * xs + moe_ffn(xs, *mw), ln2_g[i], ln2_b[i])
    return (xp, xs, jnp.stack(hs_p), jnp.stack(hs_s), jnp.stack(k_p), jnp.stack(v_p),
            jnp.stack(k_s), jnp.stack(v_s), jnp.stack(cv_p), jnp.stack(h_p),
            jnp.stack(cv_s), jnp.stack(h_s))
```

```python
import functools
import math

import jax
import jax.numpy as jnp
from jax import lax
from jax.experimental import pallas as pl
from jax.experimental.pallas import tpu as pltpu

F32 = jnp.float32
BF16 = jnp.bfloat16

DEPTH = 4
CHUNK = 64
N_MIXERS = 3
H_A = 8
H_B = 8
DH_B = 64
REL_BUCKETS = 32
REL_MAX_DIST = 1024
N_BLK_C = 4
CONV_W = 4
C_RG = 8.0
N_EXPERTS = 64
D_EXPERT = 128
TOP_K = 8
N_GROUPS = 8
TOPK_GROUPS = 4
ROUTED_SCALE = 2.5
ALPHA = (2 * DEPTH) ** 0.25
LN_EPS = 1e-5
RMS_EPS = 1e-6

VMEM_LIMIT = 56 * 1024 * 1024
REC_C = 32
ATT_T = 256
MOE_EB = 4

_NT = (((1,), (1,)), ((), ()))
_TN = (((0,), (0,)), ((), ()))


def _cparams(sem):
    return pltpu.CompilerParams(dimension_semantics=sem, vmem_limit_bytes=VMEM_LIMIT)


def _sigmoid(x):
    return 1.0 / (1.0 + jnp.exp(-x))


def _layer_norm(z, g, b):
    mu = jnp.mean(z, -1, keepdims=True)
    zc = z - mu
    var = jnp.mean(zc * zc, -1, keepdims=True)
    return zc * lax.rsqrt(var + LN_EPS) * g + b


def _mm_kernel(x_ref, w_ref, o_ref):
    o_ref[...] = jnp.dot(x_ref[...].astype(BF16), w_ref[...],
                         preferred_element_type=F32).astype(o_ref.dtype)


def _matmul(x, w, *, tm, tn, out_dtype=F32):
    t, k = x.shape
    n = w.shape[1]
    return pl.pallas_call(
        _mm_kernel,
        grid=(n // tn, t // tm),
        in_specs=[pl.BlockSpec((tm, k), lambda j, i: (i, 0)),
                  pl.BlockSpec((k, tn), lambda j, i: (0, j))],
        out_specs=pl.BlockSpec((tm, tn), lambda j, i: (i, j)),
        out_shape=jax.ShapeDtypeStruct((t, n), out_dtype),
        compiler_params=_cparams(("arbitrary", "arbitrary")),
        name="proj_in",
    )(x, w)


def _proj_ln_kernel(o_ref, w_ref, x_ref, g_ref, b_ref, out_ref):
    y = jnp.dot(o_ref[...], w_ref[...], preferred_element_type=F32)
    out_ref[...] = _layer_norm(ALPHA * x_ref[...] + y, g_ref[...], b_ref[...])


def _proj_ln(o, w, x, g, b, *, tm):
    t, d = x.shape
    k = o.shape[1]
    return pl.pallas_call(
        _proj_ln_kernel,
        grid=(t // tm,),
        in_specs=[pl.BlockSpec((tm, k), lambda i: (i, 0)),
                  pl.BlockSpec((k, d), lambda i: (0, 0)),
                  pl.BlockSpec((tm, d), lambda i: (i, 0)),
                  pl.BlockSpec((1, d), lambda i: (0, 0)),
                  pl.BlockSpec((1, d), lambda i: (0, 0))],
        out_specs=pl.BlockSpec((tm, d), lambda i: (i, 0)),
        out_shape=jax.ShapeDtypeStruct((t, d), F32),
        compiler_params=_cparams(("arbitrary",)),
        name="proj_out_ln",
    )(o, w, x, g.reshape(1, d), b.reshape(1, d))


def _hgrn_kernel(q_ref, fz_ref, v_ref, g_ref, lb_ref, gn_ref, s0_ref, o_ref, sfin_ref, st_scr,
                 *, chunk):
    c = pl.program_id(2)

    @pl.when(c == 0)
    def _():
        st_scr[...] = s0_ref[0, 0].T

    fz = fz_ref[0]
    rows = fz.shape[0]
    lb = lb_ref[...]
    e = jnp.exp(-jnp.abs(fz))
    r = 1.0 / (1.0 + e)
    pos = fz >= 0
    sig = jnp.where(pos, r, e * r)
    nsig = jnp.where(pos, e * r, r)
    logf = jnp.log(lb + (1.0 - lb) * sig)
    kk = (1.0 - lb) * nsig
    qv = q_ref[0]
    q = qv * _sigmoid(qv)
    v = v_ref[0].astype(BF16)

    ri = lax.broadcasted_iota(jnp.int32, (rows, rows), 0)
    ci = lax.broadcasted_iota(jnp.int32, (rows, rows), 1)
    tri = jnp.where((ri // chunk == ci // chunk) & (ci <= ri), 1.0, 0.0).astype(BF16)
    hi = logf.astype(BF16)
    r1 = logf - hi.astype(F32)
    mid = r1.astype(BF16)
    lo = (r1 - mid.astype(F32)).astype(BF16)
    bc = (jnp.dot(tri, hi, preferred_element_type=F32)
          + jnp.dot(tri, mid, preferred_element_type=F32)
          + jnp.dot(tri, lo, preferred_element_type=F32))

    causal = (lax.broadcasted_iota(jnp.int32, (chunk, chunk), 0)
              >= lax.broadcasted_iota(jnp.int32, (chunk, chunk), 1))
    outs = []
    for i in range(rows // chunk):
        sl = slice(i * chunk, (i + 1) * chunk)
        b = bc[sl]
        bm = b[chunk // 2 - 1:chunk // 2]
        be = b[chunk - 1:chunk]
        a_fac = q[sl] * jnp.exp(b - bm)
        k_fac = kk[sl] * jnp.exp(bm - b)
        qe = a_fac * jnp.exp(bm)
        st = st_scr[...]
        o = lax.dot_general(qe.astype(BF16), st.astype(BF16), _NT, preferred_element_type=F32)
        sc = lax.dot_general(a_fac.astype(BF16), k_fac.astype(BF16), _NT, preferred_element_type=F32)
        sc = jnp.where(causal, sc, 0.0)
        o = o + jnp.dot(sc.astype(BF16), v[sl], preferred_element_type=F32)
        kd = k_fac * jnp.exp(be - bm)
        st_scr[...] = st * jnp.exp(be) + lax.dot_general(v[sl], kd.astype(BF16), _TN,
                                                         preferred_element_type=F32)
        outs.append(o)
    o = jnp.concatenate(outs, axis=0)
    gv = g_ref[0]
    o = o * lax.rsqrt(jnp.mean(o * o, -1, keepdims=True) + RMS_EPS) * gn_ref[...] * (gv * _sigmoid(gv))
    o_ref[0] = o.astype(o_ref.dtype)

    @pl.when(c == pl.num_programs(2) - 1)
    def _():
        sfin_ref[0, 0] = st_scr[...].T


def _hgrn(xw, s0, lb, gn_g, *, rows):
    bsz, s, _ = xw.shape
    h, dk, dv = s0.shape[1:]
    sec = lambda k: pl.BlockSpec((1, rows, dk), lambda b, hh, c, k=k: (b, c, k * h + hh))
    return pl.pallas_call(
        functools.partial(_hgrn_kernel, chunk=REC_C),
        grid=(bsz, h, s // rows),
        in_specs=[sec(0), sec(1), sec(2), sec(3),
                  pl.BlockSpec((1, dk), lambda b, hh, c: (0, hh)),
                  pl.BlockSpec((1, dv), lambda b, hh, c: (0, 0)),
                  pl.BlockSpec((1, 1, dk, dv), lambda b, hh, c: (b, hh, 0, 0))],
        out_specs=[pl.BlockSpec((1, rows, dv), lambda b, hh, c: (b, c, hh)),
                   pl.BlockSpec((1, 1, dk, dv), lambda b, hh, c: (b, hh, 0, 0))],
        out_shape=[jax.ShapeDtypeStruct((bsz, s, h * dv), BF16),
                   jax.ShapeDtypeStruct((bsz, h, dk, dv), F32)],
        scratch_shapes=[pltpu.VMEM((dv, dk), F32)],
        compiler_params=_cparams(("arbitrary", "arbitrary", "arbitrary")),
        name="hgrn2",
    )(xw, xw, xw, xw, lb.reshape(1, h * dk), gn_g.reshape(1, dv), s0)


def _attn_init(q_ref, qs_scr, m_scr, l_scr, acc_scr):
    q = q_ref[0] * (DH_B ** -0.5)
    lane = lax.broadcasted_iota(jnp.int32, q.shape, 1)
    qs_scr[...] = jnp.concatenate([jnp.where(lane < DH_B, q, 0.0),
                                   jnp.where(lane >= DH_B, q, 0.0)], axis=0).astype(BF16)
    m_scr[...] = jnp.full(m_scr.shape, -1e30, F32)
    l_scr[...] = jnp.zeros(l_scr.shape, F32)
    acc_scr[...] = jnp.zeros(acc_scr.shape, F32)


def _attn_tile(k, v, seg, qs_scr, m_scr, l_scr, acc_scr, *, tq, mask_pos):
    tk = k.shape[0]
    w = seg.shape[1]
    s = lax.dot_general(qs_scr[...], k.astype(BF16), _NT, preferred_element_type=F32)
    bias = pltpu.roll(jnp.broadcast_to(seg, (tq, w)), w - tq, 1, stride=1, stride_axis=0)[:, :tk]
    if mask_pos is not None:
        q0, k0 = mask_pos
        qc = (q0 + lax.broadcasted_iota(jnp.int32, (tq, tk), 0)) // CHUNK
        kc = (k0 + lax.broadcasted_iota(jnp.int32, (tq, tk), 1)) // CHUNK
        vis = kc <= qc
        s = jnp.concatenate([jnp.where(vis, s[:tq] + bias, -1e30),
                             jnp.where(vis, s[tq:] + bias, -1e30)], axis=0)
    else:
        s = jnp.concatenate([s[:tq] + bias, s[tq:] + bias], axis=0)
    m_old = m_scr[...]
    m_new = jnp.maximum(m_old, jnp.max(s, -1, keepdims=True))
    alpha = jnp.exp(m_old - m_new)
    p = jnp.exp(s - m_new)
    l_scr[...] = alpha * l_scr[...] + jnp.sum(p, -1, keepdims=True)
    acc_scr[...] = alpha * acc_scr[...] + jnp.dot(p.astype(BF16), v.astype(BF16),
                                                  preferred_element_type=F32)
    m_scr[...] = m_new


def _attn_finish(lam_ref, g_ref, o_ref, l_scr, acc_scr, *, tq, out_scale):
    on = acc_scr[...] / l_scr[...]
    o = on[:tq] - lam_ref[0] * on[tq:]
    o = o * lax.rsqrt(jnp.mean(o * o, -1, keepdims=True) + RMS_EPS) * g_ref[...] * out_scale
    o_ref[0] = o.astype(o_ref.dtype)


def _attn_prompt_kernel(lam_ref, q_ref, k_ref, v_ref, seg_ref, g_ref, o_ref,
                        qs_scr, m_scr, l_scr, acc_scr, *, tq, out_scale):
    qi = pl.program_id(2)
    ki = pl.program_id(3)
    scr = (qs_scr, m_scr, l_scr, acc_scr)

    @pl.when(ki == 0)
    def _():
        _attn_init(q_ref, *scr)

    @pl.when(ki < qi)
    def _():
        _attn_tile(k_ref[0], v_ref[0], seg_ref[0, 0], *scr, tq=tq, mask_pos=None)

    @pl.when(ki == qi)
    def _():
        _attn_tile(k_ref[0], v_ref[0], seg_ref[0, 0], *scr, tq=tq, mask_pos=(qi * tq, ki * tq))

    @pl.when(ki == pl.num_programs(3) - 1)
    def _():
        _attn_finish(lam_ref, g_ref, o_ref, l_scr, acc_scr, tq=tq, out_scale=out_scale)


def _attn_sample_kernel(lam_ref, q_ref, kp_ref, vp_ref, segp_ref, kn_ref, vn_ref, segn_ref, g_ref,
                        o_ref, qs_scr, m_scr, l_scr, acc_scr, *, tq, n_past, out_scale):
    ki = pl.program_id(2)
    scr = (qs_scr, m_scr, l_scr, acc_scr)

    @pl.when(ki == 0)
    def _():
        _attn_init(q_ref, *scr)

    @pl.when(ki < n_past)
    def _():
        _attn_tile(kp_ref[0], vp_ref[0], segp_ref[0, 0], *scr, tq=tq, mask_pos=None)

    @pl.when(ki == n_past)
    def _():
        _attn_tile(kn_ref[0], vn_ref[0], segn_ref[0, 0], *scr, tq=tq, mask_pos=None)
        _attn_finish(lam_ref, g_ref, o_ref, l_scr, acc_scr, tq=tq, out_scale=out_scale)


def _rel_bucket(rel):
    nb = REL_BUCKETS // 2
    max_exact = nb // 2
    ret = jnp.where(rel > 0, nb, 0)
    n = jnp.abs(rel)
    nf = jnp.maximum(n, 1).astype(F32)
    large = max_exact + (jnp.log(nf / max_exact) / math.log(REL_MAX_DIST / max_exact)
                         * (nb - max_exact)).astype(jnp.int32)
    large = jnp.minimum(large, nb - 1)
    return ret + jnp.where(n < max_exact, n, large)


def _bias_segments(rel_bias, rel_bases, width):
    rel = jnp.asarray(rel_bases, jnp.int32)[:, None] + jnp.arange(width, dtype=jnp.int32)[None, :]
    seg = rel_bias.astype(F32)[_rel_bucket(rel)]
    return jnp.transpose(seg, (2, 0, 1))[:, :, None, :]


def _attn_scratch(tq, dv):
    return [pltpu.VMEM((2 * tq, dv), BF16), pltpu.VMEM((2 * tq, 1), F32),
            pltpu.VMEM((2 * tq, 1), F32), pltpu.VMEM((2 * tq, dv), F32)]


def _attn_prompt(qkv, rel_bias, lam, subln_g, lam_init):
    bsz, s, _ = qkv.shape
    h, dv, t = H_B, 2 * DH_B, ATT_T
    n = s // t
    w = 2 * t
    segs = _bias_segments(rel_bias, [-d * t - t for d in range(n)], w)
    kv = lambda sec: pl.BlockSpec((1, t, dv), lambda b, hh, qi, ki, sec=sec:
                                  (b, jnp.minimum(ki, qi), sec * h + hh))
    return pl.pallas_call(
        functools.partial(_attn_prompt_kernel, tq=t, out_scale=1.0 - lam_init),
        grid=(bsz, h, n, n),
        in_specs=[pl.BlockSpec(memory_space=pltpu.SMEM),
                  pl.BlockSpec((1, t, dv), lambda b, hh, qi, ki: (b, qi, hh)),
                  kv(1), kv(2),
                  pl.BlockSpec((1, 1, 1, w), lambda b, hh, qi, ki: (hh, qi - jnp.minimum(ki, qi), 0, 0)),
                  pl.BlockSpec((1, dv), lambda b, hh, qi, ki: (0, 0))],
        out_specs=pl.BlockSpec((1, t, dv), lambda b, hh, qi, ki: (b, qi, hh)),
        out_shape=jax.ShapeDtypeStruct((bsz, s, h * dv), BF16),
        scratch_shapes=_attn_scratch(t, dv),
        compiler_params=_cparams(("arbitrary",) * 4),
        name="diff_attn_prompt",
    )(lam, qkv, qkv, qkv, segs, subln_g.reshape(1, dv))


def _attn_sample(qkv, k_past, v_past, rel_bias, lam, subln_g, lam_init):
    bsz, s, _ = qkv.shape
    past = k_past.shape[1]
    h, dv, tkp = H_B, 2 * DH_B, ATT_T
    n_past = past // tkp
    wp = -(-(s + tkp) // 128) * 128
    wn = -(-(2 * s) // 128) * 128
    segp = _bias_segments(rel_bias, [j * tkp - past - s for j in range(n_past)], wp)
    segn = _bias_segments(rel_bias, [-s], wn)
    pk = lambda: pl.BlockSpec((1, tkp, dv), lambda b, hh, ki: (b, jnp.minimum(ki, n_past - 1), hh))
    nk = lambda sec: pl.BlockSpec((1, s, dv), lambda b, hh, ki, sec=sec: (b, 0, sec * h + hh))
    return pl.pallas_call(
        functools.partial(_attn_sample_kernel, tq=s, n_past=n_past, out_scale=1.0 - lam_init),
        grid=(bsz, h, n_past + 1),
        in_specs=[pl.BlockSpec(memory_space=pltpu.SMEM),
                  nk(0), pk(), pk(),
                  pl.BlockSpec((1, 1, 1, wp), lambda b, hh, ki: (hh, jnp.minimum(ki, n_past - 1), 0, 0)),
                  nk(1), nk(2),
                  pl.BlockSpec((1, 1, 1, wn), lambda b, hh, ki: (hh, 0, 0, 0)),
                  pl.BlockSpec((1, dv), lambda b, hh, ki: (0, 0))],
        out_specs=pl.BlockSpec((1, s, dv), lambda b, hh, ki: (b, 0, hh)),
        out_shape=jax.ShapeDtypeStruct((bsz, s, h * dv), BF16),
        scratch_shapes=_attn_scratch(s, dv),
        compiler_params=_cparams(("arbitrary",) * 3),
        name="diff_attn_sample",
    )(lam, qkv, k_past, v_past, segp, qkv, qkv, segn, subln_g.reshape(1, dv))


def _rglru_kernel(xw_ref, cw_ref, cb_ref, wa_ref, ba_ref, wx_ref, bx_ref, lam_ref, cbuf_ref, h0_ref,
                  y_ref, hlast_ref, ext_scr, h_scr):
    t = pl.program_id(1)
    tt = y_ref.shape[1]
    d = y_ref.shape[2]
    pad = ext_scr.shape[0] - tt

    @pl.when(t == 0)
    def _():
        ext_scr[0:pad] = cbuf_ref[0]
        h_scr[...] = h0_ref[0]

    gate = xw_ref[0, :, 0:d]
    u = xw_ref[0, :, d:2 * d]
    ext_scr[pad:pad + tt] = u
    conv = cb_ref[...] + cw_ref[CONV_W - 1:CONV_W] * u
    for j in range(1, CONV_W):
        conv = conv + cw_ref[CONV_W - 1 - j:CONV_W - j] * ext_scr[pad - j:pad - j + tt]
    ext_scr[0:pad] = ext_scr[tt:tt + pad]

    cb16 = conv.astype(BF16)
    blk = d // N_BLK_C
    ra, rx = [], []
    for n in range(N_BLK_C):
        cs = cb16[:, n * blk:(n + 1) * blk]
        ra.append(jnp.dot(cs, wa_ref[n], preferred_element_type=F32))
        rx.append(jnp.dot(cs, wx_ref[n], preferred_element_type=F32))
    r = _sigmoid(jnp.concatenate(ra, axis=1) + ba_ref[...])
    ig = _sigmoid(jnp.concatenate(rx, axis=1) + bx_ref[...])
    nl = -lam_ref[...]
    sp = jnp.maximum(nl, 0.0) + jnp.log(1.0 + jnp.exp(-jnp.abs(nl)))
    log_a = (-C_RG) * r * sp
    a = jnp.exp(log_a)
    xin = jnp.sqrt(-jnp.tanh(log_a) * (1.0 + a * a)) * ig * conv

    row = lax.broadcasted_iota(jnp.int32, (tt, d), 0)
    ca, cx = a, xin
    sft = 1
    while sft < tt:
        keep = row >= sft
        pa = jnp.where(keep, pltpu.roll(ca, sft, 0), 1.0)
        px = jnp.where(keep, pltpu.roll(cx, sft, 0), 0.0)
        cx = ca * px + cx
        ca = ca * pa
        sft *= 2
    hh = ca * h_scr[...] + cx
    h_scr[...] = hh[tt - 1:tt]
    gelu = 0.5 * gate * (1.0 + jnp.tanh(math.sqrt(2.0 / math.pi) * (gate + 0.044715 * gate * gate * gate)))
    y_ref[0] = (hh * gelu).astype(y_ref.dtype)

    @pl.when(t == pl.num_programs(1) - 1)
    def _():
        hlast_ref[0] = hh[tt - 1:tt]


def _rglru(xw, cbuf8, h0, conv_w, conv_b, w_a, b_a, w_x, b_x, lam, *, rows):
    bsz, s, d2 = xw.shape
    d = d2 // 2
    pad = cbuf8.shape[1]
    blk = d // N_BLK_C
    row = lambda a: a.reshape(1, d).astype(F32)
    full2 = lambda shp: pl.BlockSpec(shp, lambda b, t: (0,) * len(shp))
    return pl.pallas_call(
        _rglru_kernel,
        grid=(bsz, s // rows),
        in_specs=[pl.BlockSpec((1, rows, d2), lambda b, t: (b, t, 0)),
                  full2((CONV_W, d)), full2((1, d)),
                  full2((N_BLK_C, blk, blk)), full2((1, d)),
                  full2((N_BLK_C, blk, blk)), full2((1, d)),
                  full2((1, d)),
                  pl.BlockSpec((1, pad, d), lambda b, t: (b, 0, 0)),
                  pl.BlockSpec((1, 1, d), lambda b, t: (b, 0, 0))],
        out_specs=[pl.BlockSpec((1, rows, d), lambda b, t: (b, t, 0)),
                   pl.BlockSpec((1, 1, d), lambda b, t: (b, 0, 0))],
        out_shape=[jax.ShapeDtypeStruct((bsz, s, d), BF16),
                   jax.ShapeDtypeStruct((bsz, 1, d), F32)],
        scratch_shapes=[pltpu.VMEM((rows + pad, d), F32), pltpu.VMEM((1, d), F32)],
        compiler_params=_cparams(("arbitrary", "arbitrary")),
        name="rglru",
    )(xw, conv_w.astype(F32), row(conv_b), w_a.astype(BF16), row(b_a), w_x.astype(BF16), row(b_x),
      row(lam), cbuf8, h0)


def _route(x, wr_t, rbias):
    ne = wr_t.shape[0]
    tm = x.shape[0]
    gsz = ne // N_GROUPS
    logits = lax.dot_general(wr_t, x, _NT, precision=lax.Precision.HIGHEST,
                             preferred_element_type=F32)
    score = _sigmoid(logits)
    biased = score + rbias
    neg = -jnp.inf
    sub = lax.broadcasted_iota(jnp.int32, (gsz, tm), 0)
    gscore = []
    for g in range(N_GROUPS):
        vg = biased[g * gsz:(g + 1) * gsz]
        m1 = jnp.max(vg, axis=0, keepdims=True)
        first = jnp.min(jnp.where(vg == m1, sub, gsz), axis=0, keepdims=True)
        m2 = jnp.max(jnp.where(sub == first, neg, vg), axis=0, keepdims=True)
        gscore.append(m1 + m2)
    masked = []
    for g in range(N_GROUPS):
        rank = jnp.zeros((1, tm), jnp.int32)
        for o in range(N_GROUPS):
            if o == g:
                continue
            ahead = (gscore[o] >= gscore[g]) if o < g else (gscore[o] > gscore[g])
            rank = rank + ahead.astype(jnp.int32)
        keep = rank < TOPK_GROUPS
        masked.append(jnp.where(keep, biased[g * gsz:(g + 1) * gsz], neg))
    cur = jnp.concatenate(masked, axis=0)
    eidx = lax.broadcasted_iota(jnp.int32, (ne, tm), 0)
    chosen = jnp.zeros((ne, tm), F32)
    for _ in range(TOP_K):
        m = jnp.max(cur, axis=0, keepdims=True)
        first = jnp.min(jnp.where(cur == m, eidx, ne), axis=0, keepdims=True)
        hit = eidx == first
        chosen = jnp.where(hit, 1.0, chosen)
        cur = jnp.where(hit, neg, cur)
    wsel = chosen * score
    return wsel / jnp.sum(wsel, axis=0, keepdims=True) * ROUTED_SCALE


def _swiglu(h, f):
    hg = h[:, :f]
    return hg * _sigmoid(hg) * h[:, f:]


def _moe_kernel(x_ref, wr_ref, rb_ref, wsi_ref, wsd_ref, wi_ref, wd_ref, g_ref, b_ref, out_ref,
                xb_scr, acc_scr, gate_scr, *, eb):
    j = pl.program_id(1)
    f = wsd_ref.shape[0]

    @pl.when(j == 0)
    def _():
        x = x_ref[...]
        xb = x.astype(BF16)
        xb_scr[...] = xb
        w_et = _route(x, wr_ref[...], rb_ref[...])
        ne, tm = w_et.shape
        gate_scr[...] = jnp.concatenate([w_et, jnp.zeros((gate_scr.shape[1] - ne, tm), F32)], axis=0).T
        sh = _swiglu(jnp.dot(xb, wsi_ref[...], preferred_element_type=F32), f)
        acc_scr[...] = jnp.dot(sh.astype(BF16), wsd_ref[...], preferred_element_type=F32)

    xb = xb_scr[...]
    gates = gate_scr[...]
    lane = lax.broadcasted_iota(jnp.int32, gates.shape, 1)
    acts = []
    for i in range(eb):
        h = jnp.dot(xb, wi_ref[i], preferred_element_type=F32)
        gcol = jnp.sum(jnp.where(lane == j * eb + i, gates, 0.0), axis=1, keepdims=True)
        acts.append((_swiglu(h, f) * gcol).astype(BF16))
    acc_scr[...] += jnp.dot(jnp.concatenate(acts, axis=1), wd_ref[0], preferred_element_type=F32)

    @pl.when(j == pl.num_programs(1) - 1)
    def _():
        out_ref[...] = _layer_norm(ALPHA * x_ref[...] + acc_scr[...], g_ref[...], b_ref[...])


def _moe_ln(x, w_router, router_bias, w_in, w_down, ws_in, ws_down, g, b, *, tm):
    t, d = x.shape
    ne, _, f2 = w_in.shape
    f = f2 // 2
    eb = MOE_EB
    wd = w_down.astype(BF16).reshape(ne // eb, eb * f, d)
    const = lambda shp: pl.BlockSpec(shp, lambda i, j: (0,) * len(shp))
    return pl.pallas_call(
        functools.partial(_moe_kernel, eb=eb),
        grid=(t // tm, ne // eb),
        in_specs=[pl.BlockSpec((tm, d), lambda i, j: (i, 0)),
                  const((ne, d)), const((ne, 1)),
                  const((d, f2)), const((f, d)),
                  pl.BlockSpec((eb, d, f2), lambda i, j: (j, 0, 0)),
                  pl.BlockSpec((1, eb * f, d), lambda i, j: (j, 0, 0)),
                  const((1, d)), const((1, d))],
        out_specs=pl.BlockSpec((tm, d), lambda i, j: (i, 0)),
        out_shape=jax.ShapeDtypeStruct((t, d), F32),
        scratch_shapes=[pltpu.VMEM((tm, d), BF16), pltpu.VMEM((tm, d), F32),
                        pltpu.VMEM((tm, 128), F32)],
        compiler_params=_cparams(("arbitrary", "arbitrary")),
        name="moe_ln",
    )(x, w_router.T.astype(F32), router_bias.reshape(ne, 1).astype(F32),
      ws_in.astype(BF16), ws_down.astype(BF16), w_in.astype(BF16), wd,
      g.reshape(1, d), b.reshape(1, d))


def _lambda_init(layer_idx):
    return 0.8 - 0.6 * math.exp(-0.3 * layer_idx)


def kernel(x_prompt, x_sample, state_hgrn_s, cache_k, cache_v, state_rglru_conv, state_rglru_h, hgrn_lb_logits, hgrn_w_in, hgrn_gn_g, hgrn_w_out, rel_bias, attn_w_qkv, attn_lam_q1, attn_lam_k1, attn_lam_q2, attn_lam_k2, attn_subln_g, attn_w_out, rglru_w_in, rglru_conv_w, rglru_conv_b, rglru_w_a, rglru_b_a, rglru_w_x, rglru_b_x, rglru_lam, rglru_w_out, ln1_g, ln1_b, ln2_g, ln2_b, moe_w_router, moe_router_bias, moe_w_in, moe_w_down, moe_ws_in, moe_ws_down):
    bp, sp, d = x_prompt.shape
    bs, ss, _ = x_sample.shape
    lb_p = jax.nn.softmax(hgrn_lb_logits.astype(F32), axis=0)
    lower_bounds = jnp.clip(jnp.cumsum(lb_p, axis=0) - lb_p[0], 0.0, 1.0)

    streams = [(x_prompt.reshape(bp * sp, d), bp, sp), (x_sample.reshape(bs * ss, d), bs, ss)]
    tile = lambda t: 1024 if t % 1024 == 0 else 512
    hs, kvs, cvs, hls = [[], []], [[], []], [[], []], [[], []]
    for i in range(DEPTH):
        j, kind = i // N_MIXERS, i % N_MIXERS
        new_streams = []
        for si, (x, bsz, s) in enumerate(streams):
            t = bsz * s
            tm = tile(t)
            if kind == 0:
                xw = _matmul(x, hgrn_w_in[j].astype(BF16), tm=tm, tn=1024)
                s0 = (jnp.zeros((bsz, H_A, d // H_A, d // H_A), F32) if si == 0
                      else state_hgrn_s[j].astype(F32))
                o, s_fin = _hgrn(xw.reshape(bsz, s, -1), s0, lower_bounds[i], hgrn_gn_g[j],
                                 rows=min(s, 256))
                hs[si].append(s_fin)
                w_out = hgrn_w_out[j]
            elif kind == 1:
                qkv = _matmul(x, attn_w_qkv[j].astype(BF16), tm=tm, tn=1024)
                hd = H_B * 2 * DH_B
                lam_init = _lambda_init(i)
                lam = (jnp.exp(jnp.sum(attn_lam_q1[j].astype(F32) * attn_lam_k1[j].astype(F32)))
                       - jnp.exp(jnp.sum(attn_lam_q2[j].astype(F32) * attn_lam_k2[j].astype(F32)))
                       + lam_init).reshape(1)
                qkv3 = qkv.reshape(bsz, s, 3 * hd)
                if si == 0:
                    o = _attn_prompt(qkv3, rel_bias, lam, attn_subln_g[j], lam_init)
                else:
                    past = cache_k.shape[2]
                    o = _attn_sample(qkv3, cache_k[j].reshape(bsz, past, hd),
                                     cache_v[j].reshape(bsz, past, hd), rel_bias, lam,
                                     attn_subln_g[j], lam_init)
                kvs[si].append((qkv3[:, :, hd:2 * hd].reshape(bsz, s, H_B, 2 * DH_B),
                                qkv3[:, :, 2 * hd:].reshape(bsz, s, H_B, 2 * DH_B)))
                w_out = attn_w_out[j]
            else:
                xw = _matmul(x, rglru_w_in[j].astype(BF16), tm=tm, tn=1024).reshape(bsz, s, 2 * d)
                if si == 0:
                    cbuf = jnp.zeros((bsz, CONV_W - 1, d), F32)
                    h0 = jnp.zeros((bsz, d), F32)
                else:
                    cbuf, h0 = state_rglru_conv[j].astype(F32), state_rglru_h[j].astype(F32)
                cbuf8 = jnp.pad(cbuf, ((0, 0), (8 - (CONV_W - 1), 0), (0, 0)))
                o, hlast = _rglru(xw, cbuf8, h0.reshape(bsz, 1, d), rglru_conv_w[j], rglru_conv_b[j],
                                  rglru_w_a[j], rglru_b_a[j], rglru_w_x[j], rglru_b_x[j], rglru_lam[j],
                                  rows=min(s, 256))
                cvs[si].append(xw[:, s - (CONV_W - 1):, d:])
                hls[si].append(hlast.reshape(bsz, d))
                w_out = rglru_w_out[j]
            x = _proj_ln(o.reshape(t, d), w_out.astype(BF16), x, ln1_g[i], ln1_b[i], tm=tm)
            x = _moe_ln(x, moe_w_router[i], moe_router_bias[i], moe_w_in[i], moe_w_down[i],
                        moe_ws_in[i], moe_ws_down[i], ln2_g[i], ln2_b[i], tm=tm)
            new_streams.append((x, bsz, s))
        streams = new_streams

    yp = streams[0][0].reshape(bp, sp, d)
    ys = streams[1][0].reshape(bs, ss, d)
    st = lambda xs: jnp.stack(xs)
    return (yp, ys, st(hs[0]), st(hs[1]),
            st([kv[0] for kv in kvs[0]]), st([kv[1] for kv in kvs[0]]),
            st([kv[0] for kv in kvs[1]]), st([kv[1] for kv in kvs[1]]),
            st(cvs[0]), st(hls[0]), st(cvs[1]), st(hls[1]))
```

```python
import functools
import math

import jax
import jax.numpy as jnp
from jax import lax
from jax.experimental import pallas as pl
from jax.experimental.pallas import tpu as pltpu

F32 = jnp.float32
BF16 = jnp.bfloat16

DEPTH = 4
CHUNK = 64
N_MIXERS = 3
H_A = 8
H_B = 8
DH_B = 64
REL_BUCKETS = 32
REL_MAX_DIST = 1024
N_BLK_C = 4
CONV_W = 4
C_RG = 8.0
N_EXPERTS = 64
D_EXPERT = 128
TOP_K = 8
N_GROUPS = 8
TOPK_GROUPS = 4
ROUTED_SCALE = 2.5
ALPHA = (2 * DEPTH) ** 0.25
LN_EPS = 1e-5
RMS_EPS = 1e-6

VMEM_LIMIT = 56 * 1024 * 1024
LOG2E = math.log2(math.e)
REC_C = 32
ATT_T = 256
MOE_EB = 4

_NT = (((1,), (1,)), ((), ()))
_TN = (((0,), (0,)), ((), ()))


def _cparams(sem):
    return pltpu.CompilerParams(dimension_semantics=sem, vmem_limit_bytes=VMEM_LIMIT)


def _sigmoid(x):
    return 1.0 / (1.0 + jnp.exp(-x))


def _layer_norm(z, g, b):
    mu = jnp.mean(z, -1, keepdims=True)
    zc = z - mu
    var = jnp.mean(zc * zc, -1, keepdims=True)
    return zc * lax.rsqrt(var + LN_EPS) * g + b


def _mm_kernel(x_ref, w_ref, o_ref):
    o_ref[...] = jnp.dot(x_ref[...].astype(BF16), w_ref[...],
                         preferred_element_type=F32).astype(o_ref.dtype)


def _matmul(x, w, *, tm, tn, out_dtype=F32):
    t, k = x.shape
    n = w.shape[1]
    return pl.pallas_call(
        _mm_kernel,
        grid=(n // tn, t // tm),
        in_specs=[pl.BlockSpec((tm, k), lambda j, i: (i, 0)),
                  pl.BlockSpec((k, tn), lambda j, i: (0, j))],
        out_specs=pl.BlockSpec((tm, tn), lambda j, i: (i, j)),
        out_shape=jax.ShapeDtypeStruct((t, n), out_dtype),
        compiler_params=_cparams(("arbitrary", "arbitrary")),
        name="proj_in",
    )(x, w)


def _proj_ln_kernel(o_ref, w_ref, x_ref, g_ref, b_ref, out_ref):
    y = jnp.dot(o_ref[...], w_ref[...], preferred_element_type=F32)
    out_ref[...] = _layer_norm(ALPHA * x_ref[...] + y, g_ref[...], b_ref[...])


def _proj_ln(o, w, x, g, b, *, tm):
    t, d = x.shape
    k = o.shape[1]
    return pl.pallas_call(
        _proj_ln_kernel,
        grid=(t // tm,),
        in_specs=[pl.BlockSpec((tm, k), lambda i: (i, 0)),
                  pl.BlockSpec((k, d), lambda i: (0, 0)),
                  pl.BlockSpec((tm, d), lambda i: (i, 0)),
                  pl.BlockSpec((1, d), lambda i: (0, 0)),
                  pl.BlockSpec((1, d), lambda i: (0, 0))],
        out_specs=pl.BlockSpec((tm, d), lambda i: (i, 0)),
        out_shape=jax.ShapeDtypeStruct((t, d), F32),
        compiler_params=_cparams(("arbitrary",)),
        name="proj_out_ln",
    )(o, w, x, g.reshape(1, d), b.reshape(1, d))


def _hgrn_kernel(q_ref, fz_ref, v_ref, g_ref, lb_ref, gn_ref, s0_ref, o_ref, sfin_ref, st_scr,
                 *, chunk):
    c = pl.program_id(2)

    @pl.when(c == 0)
    def _():
        st_scr[...] = s0_ref[0, 0].T

    fz = fz_ref[0]
    rows = fz.shape[0]
    lb = lb_ref[...]
    e = jnp.exp(-jnp.abs(fz))
    r = 1.0 / (1.0 + e)
    pos = fz >= 0
    sig = jnp.where(pos, r, e * r)
    nsig = jnp.where(pos, e * r, r)
    logf = jnp.log(lb + (1.0 - lb) * sig)
    kk = (1.0 - lb) * nsig
    qv = q_ref[0]
    q = qv * _sigmoid(qv)
    v = v_ref[0].astype(BF16)

    ri = lax.broadcasted_iota(jnp.int32, (rows, rows), 0)
    ci = lax.broadcasted_iota(jnp.int32, (rows, rows), 1)
    tri = jnp.where((ri // chunk == ci // chunk) & (ci <= ri), 1.0, 0.0).astype(BF16)
    hi = logf.astype(BF16)
    r1 = logf - hi.astype(F32)
    mid = r1.astype(BF16)
    lo = (r1 - mid.astype(F32)).astype(BF16)
    bc = (jnp.dot(tri, hi, preferred_element_type=F32)
          + jnp.dot(tri, mid, preferred_element_type=F32)
          + jnp.dot(tri, lo, preferred_element_type=F32))

    causal = (lax.broadcasted_iota(jnp.int32, (chunk, chunk), 0)
              >= lax.broadcasted_iota(jnp.int32, (chunk, chunk), 1))
    outs = []
    for i in range(rows // chunk):
        sl = slice(i * chunk, (i + 1) * chunk)
        b = bc[sl]
        bm = b[chunk // 2 - 1:chunk // 2]
        be = b[chunk - 1:chunk]
        a_fac = q[sl] * jnp.exp(b - bm)
        k_fac = kk[sl] * jnp.exp(bm - b)
        qe = a_fac * jnp.exp(bm)
        st = st_scr[...]
        o = lax.dot_general(qe.astype(BF16), st.astype(BF16), _NT, preferred_element_type=F32)
        sc = lax.dot_general(a_fac.astype(BF16), k_fac.astype(BF16), _NT, preferred_element_type=F32)
        sc = jnp.where(causal, sc, 0.0)
        o = o + jnp.dot(sc.astype(BF16), v[sl], preferred_element_type=F32)
        kd = k_fac * jnp.exp(be - bm)
        st_scr[...] = st * jnp.exp(be) + lax.dot_general(v[sl], kd.astype(BF16), _TN,
                                                         preferred_element_type=F32)
        outs.append(o)
    o = jnp.concatenate(outs, axis=0)
    gv = g_ref[0]
    o = o * lax.rsqrt(jnp.mean(o * o, -1, keepdims=True) + RMS_EPS) * gn_ref[...] * (gv * _sigmoid(gv))
    o_ref[0] = o.astype(o_ref.dtype)

    @pl.when(c == pl.num_programs(2) - 1)
    def _():
        sfin_ref[0, 0] = st_scr[...].T


def _hgrn(xw, s0, lb, gn_g, *, rows):
    bsz, s, _ = xw.shape
    h, dk, dv = s0.shape[1:]
    sec = lambda k: pl.BlockSpec((1, rows, dk), lambda b, hh, c, k=k: (b, c, k * h + hh))
    return pl.pallas_call(
        functools.partial(_hgrn_kernel, chunk=REC_C),
        grid=(bsz, h, s // rows),
        in_specs=[sec(0), sec(1), sec(2), sec(3),
                  pl.BlockSpec((1, dk), lambda b, hh, c: (0, hh)),
                  pl.BlockSpec((1, dv), lambda b, hh, c: (0, 0)),
                  pl.BlockSpec((1, 1, dk, dv), lambda b, hh, c: (b, hh, 0, 0))],
        out_specs=[pl.BlockSpec((1, rows, dv), lambda b, hh, c: (b, c, hh)),
                   pl.BlockSpec((1, 1, dk, dv), lambda b, hh, c: (b, hh, 0, 0))],
        out_shape=[jax.ShapeDtypeStruct((bsz, s, h * dv), BF16),
                   jax.ShapeDtypeStruct((bsz, h, dk, dv), F32)],
        scratch_shapes=[pltpu.VMEM((dv, dk), F32)],
        compiler_params=_cparams(("arbitrary", "arbitrary", "arbitrary")),
        name="hgrn2",
    )(xw, xw, xw, xw, lb.reshape(1, h * dk), gn_g.reshape(1, dv), s0)


def _rel_bucket(rel):
    nb = REL_BUCKETS // 2
    max_exact = nb // 2
    ret = jnp.where(rel > 0, nb, 0)
    n = jnp.abs(rel)
    nf = jnp.maximum(n, 1).astype(F32)
    large = max_exact + (jnp.log(nf / max_exact) / math.log(REL_MAX_DIST / max_exact)
                         * (nb - max_exact)).astype(jnp.int32)
    large = jnp.minimum(large, nb - 1)
    return ret + jnp.where(n < max_exact, n, large)


def _saturation_distance():
    nb = REL_BUCKETS // 2
    max_exact = nb // 2
    return math.ceil(max_exact * (REL_MAX_DIST / max_exact) ** ((nb - 1 - max_exact) / (nb - max_exact))) + 1


def _bias_rows(rel_bias, rel0s, width):
    rel = jnp.asarray(rel0s, jnp.int32)[:, None] - jnp.arange(width, dtype=jnp.int32)[None, :]
    seg = rel_bias.astype(F32)[_rel_bucket(rel)] * LOG2E
    return jnp.transpose(seg, (2, 0, 1))[:, :, None, :]


def _far_bias(rel_bias):
    return rel_bias.astype(F32)[_rel_bucket(jnp.int32(-_saturation_distance()))] * LOG2E


def _bias_tile_t(seg, tk, shift):
    w = seg.shape[1]
    return pltpu.roll(jnp.broadcast_to(seg, (tk, w)), w - tk + shift, 1, stride=1, stride_axis=0)


def _stack_q(q):
    q = q * (DH_B ** -0.5 * LOG2E)
    lane = lax.broadcasted_iota(jnp.int32, q.shape, 1)
    qs = jnp.concatenate([jnp.where(lane < DH_B, q, 0.0), jnp.where(lane >= DH_B, q, 0.0)], axis=0)
    return qs.T.astype(BF16)


def _online(st, m, l):
    m_new = jnp.maximum(m, jnp.max(st, axis=0, keepdims=True))
    alpha = jnp.exp2(m - m_new)
    p = jnp.exp2(st - m_new)
    return m_new, alpha * l + jnp.sum(p, axis=0, keepdims=True), alpha, p.astype(BF16)


def _attn_out(ot, g_ref, o_ref, out_scale):
    o = ot * lax.rsqrt(jnp.mean(ot * ot, -1, keepdims=True) + RMS_EPS) * g_ref[...] * out_scale
    o_ref[0] = o.astype(o_ref.dtype)


def _attn_prompt_kernel(lam_ref, fb_ref, q_ref, k_ref, v_ref, seg_ref, g_ref, o_ref,
                        kb_scr, vt_scr, acc_scr, *, d_far, out_scale):
    hh = pl.program_id(1)
    qi = pl.program_id(2)
    n, t, _ = kb_scr.shape

    @pl.when(qi == 0)
    def _():
        for c in range(n):
            kb_scr[c] = k_ref[0, c * t:(c + 1) * t, :].astype(BF16)
            vt_scr[c] = v_ref[0, c * t:(c + 1) * t, :].T.astype(BF16)

    qst = _stack_q(q_ref[0])
    acc_scr[...] = jnp.zeros(acc_scr.shape, F32)

    def update(ki, carry, st):
        m, l, alpha, p = _online(st, *carry)
        acc_scr[...] = alpha * acc_scr[...] + jnp.dot(vt_scr[ki], p, preferred_element_type=F32)
        return m, l

    def far(ki, carry):
        st = jnp.dot(kb_scr[ki], qst, preferred_element_type=F32) + fb_ref[hh]
        return update(ki, carry, st)

    def near(ki, carry):
        st = jnp.dot(kb_scr[ki], qst, preferred_element_type=F32)
        bt = _bias_tile_t(seg_ref[0, qi - ki], t, 0)[:, :t]
        return update(ki, carry, jnp.concatenate([st[:, :t] + bt, st[:, t:] + bt], axis=1))

    n_far = jnp.maximum(qi - (d_far - 1), 0)
    carry = (jnp.full((1, 2 * t), -1e30, F32), jnp.zeros((1, 2 * t), F32))
    carry = lax.fori_loop(0, n_far, far, carry)
    carry = lax.fori_loop(n_far, qi, near, carry)
    st = jnp.dot(kb_scr[qi], qst, preferred_element_type=F32)
    bt = _bias_tile_t(seg_ref[0, 0], t, 0)[:, :t]
    vis = (lax.broadcasted_iota(jnp.int32, (t, t), 0) // CHUNK
           <= lax.broadcasted_iota(jnp.int32, (t, t), 1) // CHUNK)
    st = jnp.concatenate([jnp.where(vis, st[:, :t] + bt, -1e30),
                          jnp.where(vis, st[:, t:] + bt, -1e30)], axis=1)
    _, l = update(qi, carry, st)
    on = acc_scr[...] * (1.0 / l)
    _attn_out((on[:, :t] - lam_ref[0] * on[:, t:]).T, g_ref, o_ref, out_scale)


def _attn_prompt(qkv, rel_bias, lam, subln_g, lam_init):
    bsz, s, _ = qkv.shape
    h, dv, t = H_B, 2 * DH_B, ATT_T
    n = s // t
    w = 2 * t
    d_far = -(-(_saturation_distance() - 1) // t) + 1
    segs = _bias_rows(rel_bias, [t - d * t for d in range(n)], w)
    kv = lambda sec: pl.BlockSpec((1, s, dv), lambda b, hh, qi, sec=sec: (b, 0, sec * h + hh))
    return pl.pallas_call(
        functools.partial(_attn_prompt_kernel, d_far=d_far, out_scale=1.0 - lam_init),
        grid=(bsz, h, n),
        in_specs=[pl.BlockSpec(memory_space=pltpu.SMEM), pl.BlockSpec(memory_space=pltpu.SMEM),
                  pl.BlockSpec((1, t, dv), lambda b, hh, qi: (b, qi, hh)),
                  kv(1), kv(2),
                  pl.BlockSpec((1, n, 1, w), lambda b, hh, qi: (hh, 0, 0, 0)),
                  pl.BlockSpec((1, dv), lambda b, hh, qi: (0, 0))],
        out_specs=pl.BlockSpec((1, t, dv), lambda b, hh, qi: (b, qi, hh)),
        out_shape=jax.ShapeDtypeStruct((bsz, s, h * dv), BF16),
        scratch_shapes=[pltpu.VMEM((n, t, dv), BF16), pltpu.VMEM((n, dv, t), BF16),
                        pltpu.VMEM((dv, 2 * t), F32)],
        compiler_params=_cparams(("arbitrary",) * 3),
        name="diff_attn_prompt",
    )(lam, _far_bias(rel_bias), qkv, qkv, qkv, segs, subln_g.reshape(1, dv))


def _attn_sample_kernel(lam_ref, fb_ref, q_ref, kp_ref, vp_ref, segp_ref, kn_ref, vn_ref, segn_ref,
                        g_ref, o_ref, *, tkp, far_tiles, out_scale):
    hh = pl.program_id(1)
    tq, dv = q_ref.shape[1:]
    n_past = kp_ref.shape[1] // tkp
    qst = _stack_q(q_ref[0])
    lane = lax.broadcasted_iota(jnp.int32, (1, 2 * tq), 1)

    def bias(seg, tk):
        return jnp.where(lane < tq, _bias_tile_t(seg, tk, 0)[:, :2 * tq],
                         _bias_tile_t(seg, tk, tq)[:, :2 * tq])

    m = jnp.full((1, 2 * tq), -1e30, F32)
    l = jnp.zeros((1, 2 * tq), F32)
    acc = jnp.zeros((dv, 2 * tq), F32)
    for c in range(n_past + 1):
        if c < n_past:
            k, v = kp_ref[0, c * tkp:(c + 1) * tkp, :], vp_ref[0, c * tkp:(c + 1) * tkp, :]
        else:
            k, v = kn_ref[0], vn_ref[0]
        st = jnp.dot(k.astype(BF16), qst, preferred_element_type=F32)
        if c < far_tiles:
            st = st + fb_ref[hh]
        elif c < n_past:
            st = st + bias(segp_ref[0, c - far_tiles], tkp)
        else:
            st = st + bias(segn_ref[0, 0], tq)
        m, l, alpha, p = _online(st, m, l)
        acc = alpha * acc + lax.dot_general(v.astype(BF16), p, _TN, preferred_element_type=F32)
    on = acc * (1.0 / l)
    ot = on - lam_ref[0] * pltpu.roll(on, tq, 1)
    _attn_out(ot.T[:tq], g_ref, o_ref, out_scale)


def _attn_sample(qkv, k_past, v_past, layer, rel_bias, lam, subln_g, lam_init):
    bsz, s, _ = qkv.shape
    past = k_past.shape[1]
    h, dv, tkp = H_B, 2 * DH_B, ATT_T
    n_past = past // tkp
    far_tiles = sum(1 for c in range(n_past) if past - (c * tkp + tkp - 1) >= _saturation_distance())
    near = list(range(far_tiles, n_past)) or [n_past - 1]
    segp = _bias_rows(rel_bias, [c * tkp - past + tkp for c in near], 2 * s + tkp)
    segn = _bias_rows(rel_bias, [s], 4 * s)
    pk = lambda: pl.BlockSpec((1, past, dv), lambda b, hh: (layer * bsz + b, 0, hh))
    nk = lambda sec: pl.BlockSpec((1, s, dv), lambda b, hh, sec=sec: (b, 0, sec * h + hh))
    return pl.pallas_call(
        functools.partial(_attn_sample_kernel, tkp=tkp, far_tiles=far_tiles, out_scale=1.0 - lam_init),
        grid=(bsz, h),
        in_specs=[pl.BlockSpec(memory_space=pltpu.SMEM), pl.BlockSpec(memory_space=pltpu.SMEM),
                  nk(0), pk(), pk(),
                  pl.BlockSpec((1, len(near), 1, 2 * s + tkp), lambda b, hh: (hh, 0, 0, 0)),
                  nk(1), nk(2),
                  pl.BlockSpec((1, 1, 1, 4 * s), lambda b, hh: (hh, 0, 0, 0)),
                  pl.BlockSpec((1, dv), lambda b, hh: (0, 0))],
        out_specs=pl.BlockSpec((1, s, dv), lambda b, hh: (b, 0, hh)),
        out_shape=jax.ShapeDtypeStruct((bsz, s, h * dv), BF16),
        compiler_params=_cparams(("arbitrary",) * 2),
        name="diff_attn_sample",
    )(lam, _far_bias(rel_bias), qkv, k_past, v_past, segp, qkv, qkv, segn, subln_g.reshape(1, dv))


def _rglru_kernel(xw_ref, cw_ref, cb_ref, wa_ref, ba_ref, wx_ref, bx_ref, lam_ref, cbuf_ref, h0_ref,
                  y_ref, hlast_ref, ext_scr, h_scr):
    t = pl.program_id(1)
    tt = y_ref.shape[1]
    d = y_ref.shape[2]
    pad = ext_scr.shape[0] - tt

    @pl.when(t == 0)
    def _():
        ext_scr[0:pad] = cbuf_ref[0]
        h_scr[...] = h0_ref[0]

    gate = xw_ref[0, :, 0:d]
    u = xw_ref[0, :, d:2 * d]
    ext_scr[pad:pad + tt] = u
    conv = cb_ref[...] + cw_ref[CONV_W - 1:CONV_W] * u
    for j in range(1, CONV_W):
        conv = conv + cw_ref[CONV_W - 1 - j:CONV_W - j] * ext_scr[pad - j:pad - j + tt]
    ext_scr[0:pad] = ext_scr[tt:tt + pad]

    cb16 = conv.astype(BF16)
    blk = d // N_BLK_C
    ra, rx = [], []
    for n in range(N_BLK_C):
        cs = cb16[:, n * blk:(n + 1) * blk]
        ra.append(jnp.dot(cs, wa_ref[n], preferred_element_type=F32))
        rx.append(jnp.dot(cs, wx_ref[n], preferred_element_type=F32))
    r = _sigmoid(jnp.concatenate(ra, axis=1) + ba_ref[...])
    ig = _sigmoid(jnp.concatenate(rx, axis=1) + bx_ref[...])
    nl = -lam_ref[...]
    sp = jnp.maximum(nl, 0.0) + jnp.log(1.0 + jnp.exp(-jnp.abs(nl)))
    log_a = (-C_RG) * r * sp
    a = jnp.exp(log_a)
    xin = jnp.sqrt(-jnp.tanh(log_a) * (1.0 + a * a)) * ig * conv

    row = lax.broadcasted_iota(jnp.int32, (tt, d), 0)
    ca, cx = a, xin
    sft = 1
    while sft < tt:
        keep = row >= sft
        pa = jnp.where(keep, pltpu.roll(ca, sft, 0), 1.0)
        px = jnp.where(keep, pltpu.roll(cx, sft, 0), 0.0)
        cx = ca * px + cx
        ca = ca * pa
        sft *= 2
    hh = ca * h_scr[...] + cx
    h_scr[...] = hh[tt - 1:tt]
    gelu = 0.5 * gate * (1.0 + jnp.tanh(math.sqrt(2.0 / math.pi) * (gate + 0.044715 * gate * gate * gate)))
    y_ref[0] = (hh * gelu).astype(y_ref.dtype)

    @pl.when(t == pl.num_programs(1) - 1)
    def _():
        hlast_ref[0] = hh[tt - 1:tt]


def _rglru(xw, cbuf8, h0, conv_w, conv_b, w_a, b_a, w_x, b_x, lam, *, rows):
    bsz, s, d2 = xw.shape
    d = d2 // 2
    pad = cbuf8.shape[1]
    blk = d // N_BLK_C
    row = lambda a: a.reshape(1, d).astype(F32)
    full2 = lambda shp: pl.BlockSpec(shp, lambda b, t: (0,) * len(shp))
    return pl.pallas_call(
        _rglru_kernel,
        grid=(bsz, s // rows),
        in_specs=[pl.BlockSpec((1, rows, d2), lambda b, t: (b, t, 0)),
                  full2((CONV_W, d)), full2((1, d)),
                  full2((N_BLK_C, blk, blk)), full2((1, d)),
                  full2((N_BLK_C, blk, blk)), full2((1, d)),
                  full2((1, d)),
                  pl.BlockSpec((1, pad, d), lambda b, t: (b, 0, 0)),
                  pl.BlockSpec((1, 1, d), lambda b, t: (b, 0, 0))],
        out_specs=[pl.BlockSpec((1, rows, d), lambda b, t: (b, t, 0)),
                   pl.BlockSpec((1, 1, d), lambda b, t: (b, 0, 0))],
        out_shape=[jax.ShapeDtypeStruct((bsz, s, d), BF16),
                   jax.ShapeDtypeStruct((bsz, 1, d), F32)],
        scratch_shapes=[pltpu.VMEM((rows + pad, d), F32), pltpu.VMEM((1, d), F32)],
        compiler_params=_cparams(("arbitrary", "arbitrary")),
        name="rglru",
    )(xw, conv_w.astype(F32), row(conv_b), w_a.astype(BF16), row(b_a), w_x.astype(BF16), row(b_x),
      row(lam), cbuf8, h0)


def _route(x, wr_t, rbias):
    ne = wr_t.shape[0]
    tm = x.shape[0]
    gsz = ne // N_GROUPS
    logits = lax.dot_general(wr_t, x, _NT, precision=lax.Precision.HIGHEST,
                             preferred_element_type=F32)
    score = _sigmoid(logits)
    biased = score + rbias
    neg = -jnp.inf
    sub = lax.broadcasted_iota(jnp.int32, (gsz, tm), 0)
    gscore = []
    for g in range(N_GROUPS):
        vg = biased[g * gsz:(g + 1) * gsz]
        m1 = jnp.max(vg, axis=0, keepdims=True)
        first = jnp.min(jnp.where(vg == m1, sub, gsz), axis=0, keepdims=True)
        m2 = jnp.max(jnp.where(sub == first, neg, vg), axis=0, keepdims=True)
        gscore.append(m1 + m2)
    masked = []
    for g in range(N_GROUPS):
        rank = jnp.zeros((1, tm), jnp.int32)
        for o in range(N_GROUPS):
            if o == g:
                continue
            ahead = (gscore[o] >= gscore[g]) if o < g else (gscore[o] > gscore[g])
            rank = rank + ahead.astype(jnp.int32)
        keep = rank < TOPK_GROUPS
        masked.append(jnp.where(keep, biased[g * gsz:(g + 1) * gsz], neg))
    cur = jnp.concatenate(masked, axis=0)
    eidx = lax.broadcasted_iota(jnp.int32, (ne, tm), 0)
    chosen = jnp.zeros((ne, tm), F32)
    for _ in range(TOP_K):
        m = jnp.max(cur, axis=0, keepdims=True)
        first = jnp.min(jnp.where(cur == m, eidx, ne), axis=0, keepdims=True)
        hit = eidx == first
        chosen = jnp.where(hit, 1.0, chosen)
        cur = jnp.where(hit, neg, cur)
    wsel = chosen * score
    return wsel / jnp.sum(wsel, axis=0, keepdims=True) * ROUTED_SCALE


def _swiglu(h, f):
    hg = h[:, :f]
    return hg * _sigmoid(hg) * h[:, f:]


def _moe_kernel(x_ref, wr_ref, rb_ref, wsi_ref, wsd_ref, wi_ref, wd_ref, g_ref, b_ref, out_ref,
                xb_scr, acc_scr, gate_scr, *, eb):
    j = pl.program_id(1)
    f = wsd_ref.shape[0]

    @pl.when(j == 0)
    def _():
        x = x_ref[...]
        xb = x.astype(BF16)
        xb_scr[...] = xb
        w_et = _route(x, wr_ref[...], rb_ref[...])
        ne, tm = w_et.shape
        gate_scr[...] = jnp.concatenate([w_et, jnp.zeros((gate_scr.shape[1] - ne, tm), F32)], axis=0).T
        sh = _swiglu(jnp.dot(xb, wsi_ref[...], preferred_element_type=F32), f)
        acc_scr[...] = jnp.dot(sh.astype(BF16), wsd_ref[...], preferred_element_type=F32)

    xb = xb_scr[...]
    gates = gate_scr[...]
    lane = lax.broadcasted_iota(jnp.int32, gates.shape, 1)
    acts = []
    for i in range(eb):
        h = jnp.dot(xb, wi_ref[i], preferred_element_type=F32)
        gcol = jnp.sum(jnp.where(lane == j * eb + i, gates, 0.0), axis=1, keepdims=True)
        acts.append((_swiglu(h, f) * gcol).astype(BF16))
    acc_scr[...] += jnp.dot(jnp.concatenate(acts, axis=1), wd_ref[0], preferred_element_type=F32)

    @pl.when(j == pl.num_programs(1) - 1)
    def _():
        out_ref[...] = _layer_norm(ALPHA * x_ref[...] + acc_scr[...], g_ref[...], b_ref[...])


def _moe_ln(x, w_router, router_bias, w_in, w_down, ws_in, ws_down, g, b, *, tm):
    t, d = x.shape
    ne, _, f2 = w_in.shape
    f = f2 // 2
    eb = MOE_EB
    wd = w_down.astype(BF16).reshape(ne // eb, eb * f, d)
    const = lambda shp: pl.BlockSpec(shp, lambda i, j: (0,) * len(shp))
    return pl.pallas_call(
        functools.partial(_moe_kernel, eb=eb),
        grid=(t // tm, ne // eb),
        in_specs=[pl.BlockSpec((tm, d), lambda i, j: (i, 0)),
                  const((ne, d)), const((ne, 1)),
                  const((d, f2)), const((f, d)),
                  pl.BlockSpec((eb, d, f2), lambda i, j: (j, 0, 0)),
                  pl.BlockSpec((1, eb * f, d), lambda i, j: (j, 0, 0)),
                  const((1, d)), const((1, d))],
        out_specs=pl.BlockSpec((tm, d), lambda i, j: (i, 0)),
        out_shape=jax.ShapeDtypeStruct((t, d), F32),
        scratch_shapes=[pltpu.VMEM((tm, d), BF16), pltpu.VMEM((tm, d), F32),
                        pltpu.VMEM((tm, 128), F32)],
        compiler_params=_cparams(("arbitrary", "arbitrary")),
        name="moe_ln",
    )(x, w_router.T.astype(F32), router_bias.reshape(ne, 1).astype(F32),
      ws_in.astype(BF16), ws_down.astype(BF16), w_in.astype(BF16), wd,
      g.reshape(1, d), b.reshape(1, d))


def _lambda_init(layer_idx):
    return 0.8 - 0.6 * math.exp(-0.3 * layer_idx)


def kernel(x_prompt, x_sample, state_hgrn_s, cache_k, cache_v, state_rglru_conv, state_rglru_h, hgrn_lb_logits, hgrn_w_in, hgrn_gn_g, hgrn_w_out, rel_bias, attn_w_qkv, attn_lam_q1, attn_lam_k1, attn_lam_q2, attn_lam_k2, attn_subln_g, attn_w_out, rglru_w_in, rglru_conv_w, rglru_conv_b, rglru_w_a, rglru_b_a, rglru_w_x, rglru_b_x, rglru_lam, rglru_w_out, ln1_g, ln1_b, ln2_g, ln2_b, moe_w_router, moe_router_bias, moe_w_in, moe_w_down, moe_ws_in, moe_ws_down):
    bp, sp, d = x_prompt.shape
    bs, ss, _ = x_sample.shape
    lb_p = jax.nn.softmax(hgrn_lb_logits.astype(F32), axis=0)
    lower_bounds = jnp.clip(jnp.cumsum(lb_p, axis=0) - lb_p[0], 0.0, 1.0)

    streams = [(x_prompt.reshape(bp * sp, d), bp, sp), (x_sample.reshape(bs * ss, d), bs, ss)]
    tile = lambda t: 1024 if t % 1024 == 0 else 512
    hs, kvs, cvs, hls = [[], []], [[], []], [[], []], [[], []]
    for i in range(DEPTH):
        j, kind = i // N_MIXERS, i % N_MIXERS
        new_streams = []
        for si, (x, bsz, s) in enumerate(streams):
            t = bsz * s
            tm = tile(t)
            if kind == 0:
                xw = _matmul(x, hgrn_w_in[j].astype(BF16), tm=tm, tn=1024)
                s0 = (jnp.zeros((bsz, H_A, d // H_A, d // H_A), F32) if si == 0
                      else state_hgrn_s[j].astype(F32))
                o, s_fin = _hgrn(xw.reshape(bsz, s, -1), s0, lower_bounds[i], hgrn_gn_g[j],
                                 rows=min(s, 256))
                hs[si].append(s_fin)
                w_out = hgrn_w_out[j]
            elif kind == 1:
                qkv = _matmul(x, attn_w_qkv[j].astype(BF16), tm=tm, tn=1024)
                hd = H_B * 2 * DH_B
                lam_init = _lambda_init(i)
                lam = (jnp.exp(jnp.sum(attn_lam_q1[j].astype(F32) * attn_lam_k1[j].astype(F32)))
                       - jnp.exp(jnp.sum(attn_lam_q2[j].astype(F32) * attn_lam_k2[j].astype(F32)))
                       + lam_init).reshape(1)
                qkv3 = qkv.reshape(bsz, s, 3 * hd)
                if si == 0:
                    o = _attn_prompt(qkv3, rel_bias, lam, attn_subln_g[j], lam_init)
                else:
                    past = cache_k.shape[2]
                    o = _attn_sample(qkv3, cache_k.reshape(-1, past, hd), cache_v.reshape(-1, past, hd),
                                     j, rel_bias, lam, attn_subln_g[j], lam_init)
                kvs[si].append((qkv3[:, :, hd:2 * hd].reshape(bsz, s, H_B, 2 * DH_B),
                                qkv3[:, :, 2 * hd:].reshape(bsz, s, H_B, 2 * DH_B)))
                w_out = attn_w_out[j]
            else:
                xw = _matmul(x, rglru_w_in[j].astype(BF16), tm=tm, tn=1024).reshape(bsz, s, 2 * d)
                if si == 0:
                    cbuf = jnp.zeros((bsz, CONV_W - 1, d), F32)
                    h0 = jnp.zeros((bsz, d), F32)
                else:
                    cbuf, h0 = state_rglru_conv[j].astype(F32), state_rglru_h[j].astype(F32)
                cbuf8 = jnp.pad(cbuf, ((0, 0), (8 - (CONV_W - 1), 0), (0, 0)))
                o, hlast = _rglru(xw, cbuf8, h0.reshape(bsz, 1, d), rglru_conv_w[j], rglru_conv_b[j],
                                  rglru_w_a[j], rglru_b_a[j], rglru_w_x[j], rglru_b_x[j], rglru_lam[j],
                                  rows=min(s, 256))
                cvs[si].append(xw[:, s - (CONV_W - 1):, d:])
                hls[si].append(hlast.reshape(bsz, d))
                w_out = rglru_w_out[j]
            x = _proj_ln(o.reshape(t, d), w_out.astype(BF16), x, ln1_g[i], ln1_b[i], tm=tm)
            x = _moe_ln(x, moe_w_router[i], moe_router_bias[i], moe_w_in[i], moe_w_down[i],
                        moe_ws_in[i], moe_ws_down[i], ln2_g[i], ln2_b[i], tm=tm)
            new_streams.append((x, bsz, s))
        streams = new_streams

    yp = streams[0][0].reshape(bp, sp, d)
    ys = streams[1][0].reshape(bs, ss, d)
    st = lambda xs: jnp.stack(xs)
    return (yp, ys, st(hs[0]), st(hs[1]),
            st([kv[0] for kv in kvs[0]]), st([kv[1] for kv in kvs[0]]),
            st([kv[0] for kv in kvs[1]]), st([kv[1] for kv in kvs[1]]),
            st(cvs[0]), st(hls[0]), st(cvs[1]), st(hls[1]))
```

```python
import functools
import math

import jax
import jax.numpy as jnp
from jax import lax
from jax.experimental import pallas as pl
from jax.experimental.pallas import tpu as pltpu

F32 = jnp.float32
BF16 = jnp.bfloat16

DEPTH = 4
CHUNK = 64
N_MIXERS = 3
H_A = 8
H_B = 8
DH_B = 64
REL_BUCKETS = 32
REL_MAX_DIST = 1024
N_BLK_C = 4
CONV_W = 4
C_RG = 8.0
N_EXPERTS = 64
D_EXPERT = 128
TOP_K = 8
N_GROUPS = 8
TOPK_GROUPS = 4
ROUTED_SCALE = 2.5
ALPHA = (2 * DEPTH) ** 0.25
LN_EPS = 1e-5
RMS_EPS = 1e-6

VMEM_LIMIT = 56 * 1024 * 1024
LOG2E = math.log2(math.e)
REC_C = 32
ATT_T = 256
ATT_G = 4
MOE_EB = 4

_NT = (((1,), (1,)), ((), ()))
_TN = (((0,), (0,)), ((), ()))


def _cparams(sem):
    return pltpu.CompilerParams(dimension_semantics=sem, vmem_limit_bytes=VMEM_LIMIT)


def _sigmoid(x):
    return 1.0 / (1.0 + jnp.exp(-x))


def _layer_norm(z, g, b):
    mu = jnp.mean(z, -1, keepdims=True)
    zc = z - mu
    var = jnp.mean(zc * zc, -1, keepdims=True)
    return zc * lax.rsqrt(var + LN_EPS) * g + b


def _mm_kernel(x_ref, w_ref, o_ref):
    o_ref[...] = jnp.dot(x_ref[...].astype(BF16), w_ref[...],
                         preferred_element_type=F32).astype(o_ref.dtype)


def _matmul(x, w, *, tm, tn, out_dtype=F32):
    t, k = x.shape
    n = w.shape[1]
    return pl.pallas_call(
        _mm_kernel,
        grid=(n // tn, t // tm),
        in_specs=[pl.BlockSpec((tm, k), lambda j, i: (i, 0)),
                  pl.BlockSpec((k, tn), lambda j, i: (0, j))],
        out_specs=pl.BlockSpec((tm, tn), lambda j, i: (i, j)),
        out_shape=jax.ShapeDtypeStruct((t, n), out_dtype),
        compiler_params=_cparams(("arbitrary", "arbitrary")),
        name="proj_in",
    )(x, w)


def _proj_ln_kernel(o_ref, w_ref, x_ref, g_ref, b_ref, out_ref):
    y = jnp.dot(o_ref[...], w_ref[...], preferred_element_type=F32)
    out_ref[...] = _layer_norm(ALPHA * x_ref[...] + y, g_ref[...], b_ref[...])


def _proj_ln(o, w, x, g, b, *, tm):
    t, d = x.shape
    k = o.shape[1]
    return pl.pallas_call(
        _proj_ln_kernel,
        grid=(t // tm,),
        in_specs=[pl.BlockSpec((tm, k), lambda i: (i, 0)),
                  pl.BlockSpec((k, d), lambda i: (0, 0)),
                  pl.BlockSpec((tm, d), lambda i: (i, 0)),
                  pl.BlockSpec((1, d), lambda i: (0, 0)),
                  pl.BlockSpec((1, d), lambda i: (0, 0))],
        out_specs=pl.BlockSpec((tm, d), lambda i: (i, 0)),
        out_shape=jax.ShapeDtypeStruct((t, d), F32),
        compiler_params=_cparams(("arbitrary",)),
        name="proj_out_ln",
    )(o, w, x, g.reshape(1, d), b.reshape(1, d))


def _hgrn_kernel(q_ref, fz_ref, v_ref, g_ref, lb_ref, gn_ref, s0_ref, o_ref, sfin_ref, st_scr,
                 *, chunk):
    c = pl.program_id(1)
    nh, dv, dk = st_scr.shape

    @pl.when(c == 0)
    def _():
        for h in range(nh):
            st_scr[h] = s0_ref[0, h].T

    fz = fz_ref[0]
    rows = fz.shape[0]
    lb = lb_ref[...]
    e = jnp.exp(-jnp.abs(fz))
    r = 1.0 / (1.0 + e)
    pos = fz >= 0
    sig = jnp.where(pos, r, e * r)
    nsig = jnp.where(pos, e * r, r)
    logf = jnp.log(lb + (1.0 - lb) * sig)
    kk = (1.0 - lb) * nsig
    qv = q_ref[0]
    q = qv * _sigmoid(qv)
    v = v_ref[0].astype(BF16)

    ri = lax.broadcasted_iota(jnp.int32, (rows, rows), 0)
    ci = lax.broadcasted_iota(jnp.int32, (rows, rows), 1)
    tri = jnp.where((ri // chunk == ci // chunk) & (ci <= ri), 1.0, 0.0).astype(BF16)
    hi = logf.astype(BF16)
    r1 = logf - hi.astype(F32)
    mid = r1.astype(BF16)
    lo = (r1 - mid.astype(F32)).astype(BF16)
    bc = (jnp.dot(tri, hi, preferred_element_type=F32)
          + jnp.dot(tri, mid, preferred_element_type=F32)
          + jnp.dot(tri, lo, preferred_element_type=F32))

    causal = (lax.broadcasted_iota(jnp.int32, (chunk, chunk), 0)
              >= lax.broadcasted_iota(jnp.int32, (chunk, chunk), 1))
    outs = [[] for _ in range(nh)]
    for i in range(rows // chunk):
        sl = slice(i * chunk, (i + 1) * chunk)
        b = bc[sl]
        bm = b[chunk // 2 - 1:chunk // 2]
        be = b[chunk - 1:chunk]
        a_fac = q[sl] * jnp.exp(b - bm)
        k_fac = kk[sl] * jnp.exp(bm - b)
        qe = (a_fac * jnp.exp(bm)).astype(BF16)
        kd = (k_fac * jnp.exp(be - bm)).astype(BF16)
        a16 = a_fac.astype(BF16)
        k16 = k_fac.astype(BF16)
        dec = jnp.exp(be)
        for h in range(nh):
            hs = slice(h * dk, (h + 1) * dk)
            st = st_scr[h]
            o = lax.dot_general(qe[:, hs], st.astype(BF16), _NT, preferred_element_type=F32)
            sc = lax.dot_general(a16[:, hs], k16[:, hs], _NT, preferred_element_type=F32)
            sc = jnp.where(causal, sc, 0.0)
            o = o + jnp.dot(sc.astype(BF16), v[sl, hs], preferred_element_type=F32)
            st_scr[h] = st * dec[:, hs] + lax.dot_general(v[sl, hs], kd[:, hs], _TN,
                                                          preferred_element_type=F32)
            outs[h].append(o)
    gv = g_ref[0]
    gate = gv * _sigmoid(gv)
    for h in range(nh):
        hs = slice(h * dk, (h + 1) * dk)
        o = jnp.concatenate(outs[h], axis=0)
        o = o * lax.rsqrt(jnp.mean(o * o, -1, keepdims=True) + RMS_EPS) * gn_ref[...] * gate[:, hs]
        o_ref[0, :, hs] = o.astype(o_ref.dtype)

    @pl.when(c == pl.num_programs(1) - 1)
    def _():
        for h in range(nh):
            sfin_ref[0, h] = st_scr[h].T


def _hgrn(xw, s0, lb, gn_g, *, rows):
    bsz, s, _ = xw.shape
    h, dk, dv = s0.shape[1:]
    sec = lambda k: pl.BlockSpec((1, rows, h * dk), lambda b, c, k=k: (b, c, k))
    return pl.pallas_call(
        functools.partial(_hgrn_kernel, chunk=REC_C),
        grid=(bsz, s // rows),
        in_specs=[sec(0), sec(1), sec(2), sec(3),
                  pl.BlockSpec((1, h * dk), lambda b, c: (0, 0)),
                  pl.BlockSpec((1, dv), lambda b, c: (0, 0)),
                  pl.BlockSpec((1, h, dk, dv), lambda b, c: (b, 0, 0, 0))],
        out_specs=[pl.BlockSpec((1, rows, h * dv), lambda b, c: (b, c, 0)),
                   pl.BlockSpec((1, h, dk, dv), lambda b, c: (b, 0, 0, 0))],
        out_shape=[jax.ShapeDtypeStruct((bsz, s, h * dv), BF16),
                   jax.ShapeDtypeStruct((bsz, h, dk, dv), F32)],
        scratch_shapes=[pltpu.VMEM((h, dv, dk), F32)],
        compiler_params=_cparams(("arbitrary", "arbitrary")),
        name="hgrn2",
    )(xw, xw, xw, xw, lb.reshape(1, h * dk), gn_g.reshape(1, dv), s0)


def _rel_bucket(rel):
    nb = REL_BUCKETS // 2
    max_exact = nb // 2
    ret = jnp.where(rel > 0, nb, 0)
    n = jnp.abs(rel)
    nf = jnp.maximum(n, 1).astype(F32)
    large = max_exact + (jnp.log(nf / max_exact) / math.log(REL_MAX_DIST / max_exact)
                         * (nb - max_exact)).astype(jnp.int32)
    large = jnp.minimum(large, nb - 1)
    return ret + jnp.where(n < max_exact, n, large)


def _saturation_distance():
    nb = REL_BUCKETS // 2
    max_exact = nb // 2
    return math.ceil(max_exact * (REL_MAX_DIST / max_exact) ** ((nb - 1 - max_exact) / (nb - max_exact))) + 1


def _bias_rows(rel_bias, rel0s, width):
    rel = jnp.asarray(rel0s, jnp.int32)[:, None] - jnp.arange(width, dtype=jnp.int32)[None, :]
    seg = rel_bias.astype(F32)[_rel_bucket(rel)] * LOG2E
    return jnp.transpose(seg, (2, 0, 1))[:, :, None, :]


def _far_bias(rel_bias):
    return rel_bias.astype(F32)[_rel_bucket(jnp.int32(-_saturation_distance()))] * LOG2E


def _bias_tile_t(seg, tk, shift):
    w = seg.shape[1]
    return pltpu.roll(jnp.broadcast_to(seg, (tk, w)), w - tk + shift, 1, stride=1, stride_axis=0)


def _stack_q(q):
    q = q * (DH_B ** -0.5 * LOG2E)
    lane = lax.broadcasted_iota(jnp.int32, q.shape, 1)
    qs = jnp.concatenate([jnp.where(lane < DH_B, q, 0.0), jnp.where(lane >= DH_B, q, 0.0)], axis=0)
    return qs.T.astype(BF16)


def _online(st, m, l):
    m_new = jnp.maximum(m, jnp.max(st, axis=0, keepdims=True))
    alpha = jnp.exp2(m - m_new)
    p = jnp.exp2(st - m_new)
    return m_new, alpha * l + jnp.sum(p, axis=0, keepdims=True), alpha, p.astype(BF16)


def _attn_out(ot, g_ref, out_scale):
    return ot * lax.rsqrt(jnp.mean(ot * ot, -1, keepdims=True) + RMS_EPS) * g_ref[...] * out_scale


def _attn_prompt_kernel(lam_ref, fb_ref, q_ref, k_ref, v_ref, seg_ref, g_ref, o_ref,
                        kb_scr, vt_scr, qst_scr, acc_scr, *, d_far, out_scale):
    hg = pl.program_id(1)
    qi = pl.program_id(2)
    ng, n, t, dv = kb_scr.shape
    w2 = 2 * t

    @pl.when(qi == 0)
    def _():
        for u in range(ng):
            for c in range(n):
                kb_scr[u, c] = k_ref[0, c * t:(c + 1) * t, u * dv:(u + 1) * dv].astype(BF16)
                vt_scr[u, c] = v_ref[0, c * t:(c + 1) * t, u * dv:(u + 1) * dv].T.astype(BF16)

    for u in range(ng):
        qst_scr[u] = _stack_q(q_ref[0, :, u * dv:(u + 1) * dv])
    acc_scr[...] = jnp.zeros(acc_scr.shape, F32)

    def scores(ki, u):
        return jnp.dot(kb_scr[u, ki], qst_scr[u], preferred_element_type=F32)

    def biased(st, seg, vis=None):
        bt = _bias_tile_t(seg, t, 0)[:, :t]
        s1, s2 = st[:, :t] + bt, st[:, t:] + bt
        if vis is not None:
            s1, s2 = jnp.where(vis, s1, -1e30), jnp.where(vis, s2, -1e30)
        return jnp.concatenate([s1, s2], axis=1)

    def update(ki, carry, sts):
        m, l, alpha, p = _online(jnp.concatenate(sts, axis=1), *carry)
        for u in range(ng):
            sl = slice(u * w2, (u + 1) * w2)
            acc_scr[u] = alpha[:, sl] * acc_scr[u] + jnp.dot(vt_scr[u, ki], p[:, sl],
                                                             preferred_element_type=F32)
        return m, l

    def far(ki, carry):
        return update(ki, carry, [scores(ki, u) + fb_ref[hg * ng + u] for u in range(ng)])

    def near(ki, carry):
        return update(ki, carry, [biased(scores(ki, u), seg_ref[u, qi - ki]) for u in range(ng)])

    n_far = jnp.maximum(qi - (d_far - 1), 0)
    carry = (jnp.full((1, ng * w2), -1e30, F32), jnp.zeros((1, ng * w2), F32))
    carry = lax.fori_loop(0, n_far, far, carry)
    carry = lax.fori_loop(n_far, qi, near, carry)
    vis = (lax.broadcasted_iota(jnp.int32, (t, t), 0) // CHUNK
           <= lax.broadcasted_iota(jnp.int32, (t, t), 1) // CHUNK)
    _, l = update(qi, carry, [biased(scores(qi, u), seg_ref[u, 0], vis) for u in range(ng)])
    inv = 1.0 / l
    for u in range(ng):
        on = acc_scr[u] * inv[:, u * w2:(u + 1) * w2]
        ot = (on[:, :t] - lam_ref[0] * on[:, t:]).T
        o_ref[0, :, u * dv:(u + 1) * dv] = _attn_out(ot, g_ref, out_scale).astype(o_ref.dtype)


def _attn_prompt(qkv, rel_bias, lam, subln_g, lam_init):
    bsz, s, _ = qkv.shape
    h, dv, t, g = H_B, 2 * DH_B, ATT_T, ATT_G
    n = s // t
    w = 2 * t
    d_far = -(-(_saturation_distance() - 1) // t) + 1
    segs = _bias_rows(rel_bias, [t - d * t for d in range(n)], w)
    kv = lambda sec: pl.BlockSpec((1, s, g * dv), lambda b, hg, qi, sec=sec: (b, 0, sec * (h // g) + hg))
    return pl.pallas_call(
        functools.partial(_attn_prompt_kernel, d_far=d_far, out_scale=1.0 - lam_init),
        grid=(bsz, h // g, n),
        in_specs=[pl.BlockSpec(memory_space=pltpu.SMEM), pl.BlockSpec(memory_space=pltpu.SMEM),
                  pl.BlockSpec((1, t, g * dv), lambda b, hg, qi: (b, qi, hg)),
                  kv(1), kv(2),
                  pl.BlockSpec((g, n, 1, w), lambda b, hg, qi: (hg, 0, 0, 0)),
                  pl.BlockSpec((1, dv), lambda b, hg, qi: (0, 0))],
        out_specs=pl.BlockSpec((1, t, g * dv), lambda b, hg, qi: (b, qi, hg)),
        out_shape=jax.ShapeDtypeStruct((bsz, s, h * dv), BF16),
        scratch_shapes=[pltpu.VMEM((g, n, t, dv), BF16), pltpu.VMEM((g, n, dv, t), BF16),
                        pltpu.VMEM((g, dv, 2 * t), BF16), pltpu.VMEM((g, dv, 2 * t), F32)],
        compiler_params=_cparams(("arbitrary",) * 3),
        name="diff_attn_prompt",
    )(lam, _far_bias(rel_bias), qkv, qkv, qkv, segs, subln_g.reshape(1, dv))


def _attn_sample_kernel(lam_ref, fb_ref, q_ref, kp_ref, vp_ref, segp_ref, kn_ref, vn_ref, segn_ref,
                        g_ref, o_ref, *, tkp, n_past, far_tiles, out_scale):
    hg = pl.program_id(1)
    tq = q_ref.shape[1]
    dv = g_ref.shape[1]
    ng = q_ref.shape[2] // dv
    nh = kp_ref.shape[1] // (n_past * tkp)
    w2 = 2 * tq
    qst = [_stack_q(q_ref[0, :, u * dv:(u + 1) * dv]) for u in range(ng)]
    lane = lax.broadcasted_iota(jnp.int32, (1, w2), 1)

    def bias(seg, tk):
        return jnp.where(lane < tq, _bias_tile_t(seg, tk, 0)[:, :w2], _bias_tile_t(seg, tk, tq)[:, :w2])

    m = jnp.full((1, ng * w2), -1e30, F32)
    l = jnp.zeros((1, ng * w2), F32)
    acc = [jnp.zeros((dv, w2), F32) for _ in range(ng)]
    for c in range(n_past + 1):
        sts, vs = [], []
        for u in range(ng):
            if c < n_past:
                rows = pl.ds(c * tkp * nh + hg * ng + u, tkp, stride=nh)
                k, v = kp_ref[0, rows, :], vp_ref[0, rows, :]
            else:
                k, v = kn_ref[0, :, u * dv:(u + 1) * dv], vn_ref[0, :, u * dv:(u + 1) * dv]
            st = jnp.dot(k.astype(BF16), qst[u], preferred_element_type=F32)
            if c < far_tiles:
                st = st + fb_ref[hg * ng + u]
            elif c < n_past:
                st = st + bias(segp_ref[u, c - far_tiles], tkp)
            else:
                st = st + bias(segn_ref[u, 0], tq)
            sts.append(st)
            vs.append(v.astype(BF16))
        m, l, alpha, p = _online(jnp.concatenate(sts, axis=1), m, l)
        for u in range(ng):
            sl = slice(u * w2, (u + 1) * w2)
            acc[u] = alpha[:, sl] * acc[u] + lax.dot_general(vs[u], p[:, sl], _TN,
                                                             preferred_element_type=F32)
    inv = 1.0 / l
    for u in range(ng):
        on = acc[u] * inv[:, u * w2:(u + 1) * w2]
        ot = (on - lam_ref[0] * pltpu.roll(on, tq, 1)).T[:tq]
        o_ref[0, :, u * dv:(u + 1) * dv] = _attn_out(ot, g_ref, out_scale).astype(o_ref.dtype)


def _attn_sample(qkv, k_past, v_past, layer, rel_bias, lam, subln_g, lam_init):
    bsz, s, _ = qkv.shape
    h, dv, tkp, g = H_B, 2 * DH_B, ATT_T, ATT_G
    past = k_past.shape[1] // h
    n_past = past // tkp
    far_tiles = sum(1 for c in range(n_past) if past - (c * tkp + tkp - 1) >= _saturation_distance())
    near = list(range(far_tiles, n_past)) or [n_past - 1]
    segp = _bias_rows(rel_bias, [c * tkp - past + tkp for c in near], 2 * s + tkp)
    segn = _bias_rows(rel_bias, [s], 4 * s)
    pk = lambda: pl.BlockSpec((1, past * h, dv), lambda b, hg: (layer * bsz + b, 0, 0))
    nk = lambda sec: pl.BlockSpec((1, s, g * dv), lambda b, hg, sec=sec: (b, 0, sec * (h // g) + hg))
    return pl.pallas_call(
        functools.partial(_attn_sample_kernel, tkp=tkp, n_past=n_past, far_tiles=far_tiles,
                          out_scale=1.0 - lam_init),
        grid=(bsz, h // g),
        in_specs=[pl.BlockSpec(memory_space=pltpu.SMEM), pl.BlockSpec(memory_space=pltpu.SMEM),
                  nk(0), pk(), pk(),
                  pl.BlockSpec((g, len(near), 1, 2 * s + tkp), lambda b, hg: (hg, 0, 0, 0)),
                  nk(1), nk(2),
                  pl.BlockSpec((g, 1, 1, 4 * s), lambda b, hg: (hg, 0, 0, 0)),
                  pl.BlockSpec((1, dv), lambda b, hg: (0, 0))],
        out_specs=pl.BlockSpec((1, s, g * dv), lambda b, hg: (b, 0, hg)),
        out_shape=jax.ShapeDtypeStruct((bsz, s, h * dv), BF16),
        compiler_params=_cparams(("arbitrary",) * 2),
        name="diff_attn_sample",
    )(lam, _far_bias(rel_bias), qkv, k_past, v_past, segp, qkv, qkv, segn, subln_g.reshape(1, dv))


def _rglru_kernel(xw_ref, cw_ref, cb_ref, wa_ref, ba_ref, wx_ref, bx_ref, lam_ref, cbuf_ref, h0_ref,
                  y_ref, hlast_ref, ext_scr, h_scr):
    t = pl.program_id(1)
    tt = y_ref.shape[1]
    d = y_ref.shape[2]
    pad = ext_scr.shape[0] - tt

    @pl.when(t == 0)
    def _():
        ext_scr[0:pad] = cbuf_ref[0]
        h_scr[...] = h0_ref[0]

    gate = xw_ref[0, :, 0:d]
    u = xw_ref[0, :, d:2 * d]
    ext_scr[pad:pad + tt] = u
    conv = cb_ref[...] + cw_ref[CONV_W - 1:CONV_W] * u
    for j in range(1, CONV_W):
        conv = conv + cw_ref[CONV_W - 1 - j:CONV_W - j] * ext_scr[pad - j:pad - j + tt]
    ext_scr[0:pad] = ext_scr[tt:tt + pad]

    cb16 = conv.astype(BF16)
    blk = d // N_BLK_C
    ra, rx = [], []
    for n in range(N_BLK_C):
        cs = cb16[:, n * blk:(n + 1) * blk]
        ra.append(jnp.dot(cs, wa_ref[n], preferred_element_type=F32))
        rx.append(jnp.dot(cs, wx_ref[n], preferred_element_type=F32))
    r = _sigmoid(jnp.concatenate(ra, axis=1) + ba_ref[...])
    ig = _sigmoid(jnp.concatenate(rx, axis=1) + bx_ref[...])
    nl = -lam_ref[...]
    sp = jnp.maximum(nl, 0.0) + jnp.log(1.0 + jnp.exp(-jnp.abs(nl)))
    log_a = (-C_RG) * r * sp
    a = jnp.exp(log_a)
    xin = jnp.sqrt(-jnp.tanh(log_a) * (1.0 + a * a)) * ig * conv

    row = lax.broadcasted_iota(jnp.int32, (tt, d), 0)
    ca, cx = a, xin
    sft = 1
    while sft < tt:
        keep = row >= sft
        pa = jnp.where(keep, pltpu.roll(ca, sft, 0), 1.0)
        px = jnp.where(keep, pltpu.roll(cx, sft, 0), 0.0)
        cx = ca * px + cx
        ca = ca * pa
        sft *= 2
    hh = ca * h_scr[...] + cx
    h_scr[...] = hh[tt - 1:tt]
    gelu = 0.5 * gate * (1.0 + jnp.tanh(math.sqrt(2.0 / math.pi) * (gate + 0.044715 * gate * gate * gate)))
    y_ref[0] = (hh * gelu).astype(y_ref.dtype)

    @pl.when(t == pl.num_programs(1) - 1)
    def _():
        hlast_ref[0] = hh[tt - 1:tt]


def _rglru(xw, cbuf8, h0, conv_w, conv_b, w_a, b_a, w_x, b_x, lam, *, rows):
    bsz, s, d2 = xw.shape
    d = d2 // 2
    pad = cbuf8.shape[1]
    blk = d // N_BLK_C
    row = lambda a: a.reshape(1, d).astype(F32)
    full2 = lambda shp: pl.BlockSpec(shp, lambda b, t: (0,) * len(shp))
    return pl.pallas_call(
        _rglru_kernel,
        grid=(bsz, s // rows),
        in_specs=[pl.BlockSpec((1, rows, d2), lambda b, t: (b, t, 0)),
                  full2((CONV_W, d)), full2((1, d)),
                  full2((N_BLK_C, blk, blk)), full2((1, d)),
                  full2((N_BLK_C, blk, blk)), full2((1, d)),
                  full2((1, d)),
                  pl.BlockSpec((1, pad, d), lambda b, t: (b, 0, 0)),
                  pl.BlockSpec((1, 1, d), lambda b, t: (b, 0, 0))],
        out_specs=[pl.BlockSpec((1, rows, d), lambda b, t: (b, t, 0)),
                   pl.BlockSpec((1, 1, d), lambda b, t: (b, 0, 0))],
        out_shape=[jax.ShapeDtypeStruct((bsz, s, d), BF16),
                   jax.ShapeDtypeStruct((bsz, 1, d), F32)],
        scratch_shapes=[pltpu.VMEM((rows + pad, d), F32), pltpu.VMEM((1, d), F32)],
        compiler_params=_cparams(("arbitrary", "arbitrary")),
        name="rglru",
    )(xw, conv_w.astype(F32), row(conv_b), w_a.astype(BF16), row(b_a), w_x.astype(BF16), row(b_x),
      row(lam), cbuf8, h0)


def _route(x, wr_t, rbias):
    ne = wr_t.shape[0]
    tm = x.shape[0]
    gsz = ne // N_GROUPS
    logits = lax.dot_general(wr_t, x, _NT, precision=lax.Precision.HIGHEST,
                             preferred_element_type=F32)
    score = _sigmoid(logits)
    biased = score + rbias
    neg = -jnp.inf
    sub = lax.broadcasted_iota(jnp.int32, (gsz, tm), 0)
    gscore = []
    for g in range(N_GROUPS):
        vg = biased[g * gsz:(g + 1) * gsz]
        m1 = jnp.max(vg, axis=0, keepdims=True)
        first = jnp.min(jnp.where(vg == m1, sub, gsz), axis=0, keepdims=True)
        m2 = jnp.max(jnp.where(sub == first, neg, vg), axis=0, keepdims=True)
        gscore.append(m1 + m2)
    masked = []
    for g in range(N_GROUPS):
        rank = jnp.zeros((1, tm), jnp.int32)
        for o in range(N_GROUPS):
            if o == g:
                continue
            ahead = (gscore[o] >= gscore[g]) if o < g else (gscore[o] > gscore[g])
            rank = rank + ahead.astype(jnp.int32)
        keep = rank < TOPK_GROUPS
        masked.append(jnp.where(keep, biased[g * gsz:(g + 1) * gsz], neg))
    cur = jnp.concatenate(masked, axis=0)
    eidx = lax.broadcasted_iota(jnp.int32, (ne, tm), 0)
    chosen = jnp.zeros((ne, tm), F32)
    for _ in range(TOP_K):
        m = jnp.max(cur, axis=0, keepdims=True)
        first = jnp.min(jnp.where(cur == m, eidx, ne), axis=0, keepdims=True)
        hit = eidx == first
        chosen = jnp.where(hit, 1.0, chosen)
        cur = jnp.where(hit, neg, cur)
    wsel = chosen * score
    return wsel / jnp.sum(wsel, axis=0, keepdims=True) * ROUTED_SCALE


def _swiglu(h, f):
    hg = h[:, :f]
    return hg * _sigmoid(hg) * h[:, f:]


def _moe_kernel(x_ref, wr_ref, rb_ref, wsi_ref, wsd_ref, wi_ref, wd_ref, g_ref, b_ref, out_ref,
                xb_scr, acc_scr, gate_scr, *, eb):
    j = pl.program_id(1)
    f = wsd_ref.shape[0]

    @pl.when(j == 0)
    def _():
        x = x_ref[...]
        xb = x.astype(BF16)
        xb_scr[...] = xb
        w_et = _route(x, wr_ref[...], rb_ref[...])
        ne, tm = w_et.shape
        gate_scr[...] = jnp.concatenate([w_et, jnp.zeros((gate_scr.shape[1] - ne, tm), F32)], axis=0).T
        sh = _swiglu(jnp.dot(xb, wsi_ref[...], preferred_element_type=F32), f)
        acc_scr[...] = jnp.dot(sh.astype(BF16), wsd_ref[...], preferred_element_type=F32)

    xb = xb_scr[...]
    gates = gate_scr[...]
    lane = lax.broadcasted_iota(jnp.int32, gates.shape, 1)
    acts = []
    for i in range(eb):
        h = jnp.dot(xb, wi_ref[i], preferred_element_type=F32)
        gcol = jnp.sum(jnp.where(lane == j * eb + i, gates, 0.0), axis=1, keepdims=True)
        acts.append((_swiglu(h, f) * gcol).astype(BF16))
    acc_scr[...] += jnp.dot(jnp.concatenate(acts, axis=1), wd_ref[0], preferred_element_type=F32)

    @pl.when(j == pl.num_programs(1) - 1)
    def _():
        out_ref[...] = _layer_norm(ALPHA * x_ref[...] + acc_scr[...], g_ref[...], b_ref[...])


def _moe_ln(x, w_router, router_bias, w_in, w_down, ws_in, ws_down, g, b, *, tm):
    t, d = x.shape
    ne, _, f2 = w_in.shape
    f = f2 // 2
    eb = MOE_EB
    wd = w_down.astype(BF16).reshape(ne // eb, eb * f, d)
    const = lambda shp: pl.BlockSpec(shp, lambda i, j: (0,) * len(shp))
    return pl.pallas_call(
        functools.partial(_moe_kernel, eb=eb),
        grid=(t // tm, ne // eb),
        in_specs=[pl.BlockSpec((tm, d), lambda i, j: (i, 0)),
                  const((ne, d)), const((ne, 1)),
                  const((d, f2)), const((f, d)),
                  pl.BlockSpec((eb, d, f2), lambda i, j: (j, 0, 0)),
                  pl.BlockSpec((1, eb * f, d), lambda i, j: (j, 0, 0)),
                  const((1, d)), const((1, d))],
        out_specs=pl.BlockSpec((tm, d), lambda i, j: (i, 0)),
        out_shape=jax.ShapeDtypeStruct((t, d), F32),
        scratch_shapes=[pltpu.VMEM((tm, d), BF16), pltpu.VMEM((tm, d), F32),
                        pltpu.VMEM((tm, 128), F32)],
        compiler_params=_cparams(("arbitrary", "arbitrary")),
        name="moe_ln",
    )(x, w_router.T.astype(F32), router_bias.reshape(ne, 1).astype(F32),
      ws_in.astype(BF16), ws_down.astype(BF16), w_in.astype(BF16), wd,
      g.reshape(1, d), b.reshape(1, d))


def _lambda_init(layer_idx):
    return 0.8 - 0.6 * math.exp(-0.3 * layer_idx)


def kernel(x_prompt, x_sample, state_hgrn_s, cache_k, cache_v, state_rglru_conv, state_rglru_h, hgrn_lb_logits, hgrn_w_in, hgrn_gn_g, hgrn_w_out, rel_bias, attn_w_qkv, attn_lam_q1, attn_lam_k1, attn_lam_q2, attn_lam_k2, attn_subln_g, attn_w_out, rglru_w_in, rglru_conv_w, rglru_conv_b, rglru_w_a, rglru_b_a, rglru_w_x, rglru_b_x, rglru_lam, rglru_w_out, ln1_g, ln1_b, ln2_g, ln2_b, moe_w_router, moe_router_bias, moe_w_in, moe_w_down, moe_ws_in, moe_ws_down):
    bp, sp, d = x_prompt.shape
    bs, ss, _ = x_sample.shape
    lb_p = jax.nn.softmax(hgrn_lb_logits.astype(F32), axis=0)
    lower_bounds = jnp.clip(jnp.cumsum(lb_p, axis=0) - lb_p[0], 0.0, 1.0)

    streams = [(x_prompt.reshape(bp * sp, d), bp, sp), (x_sample.reshape(bs * ss, d), bs, ss)]
    tile = lambda t: 1024 if t % 1024 == 0 else 512
    hs, kvs, cvs, hls = [[], []], [[], []], [[], []], [[], []]
    for i in range(DEPTH):
        j, kind = i // N_MIXERS, i % N_MIXERS
        new_streams = []
        for si, (x, bsz, s) in enumerate(streams):
            t = bsz * s
            tm = tile(t)
            if kind == 0:
                xw = _matmul(x, hgrn_w_in[j].astype(BF16), tm=tm, tn=1024)
                s0 = (jnp.zeros((bsz, H_A, d // H_A, d // H_A), F32) if si == 0
                      else state_hgrn_s[j].astype(F32))
                o, s_fin = _hgrn(xw.reshape(bsz, s, -1), s0, lower_bounds[i], hgrn_gn_g[j],
                                 rows=min(s, 256))
                hs[si].append(s_fin)
                w_out = hgrn_w_out[j]
            elif kind == 1:
                qkv = _matmul(x, attn_w_qkv[j].astype(BF16), tm=tm, tn=1024)
                hd = H_B * 2 * DH_B
                lam_init = _lambda_init(i)
                lam = (jnp.exp(jnp.sum(attn_lam_q1[j].astype(F32) * attn_lam_k1[j].astype(F32)))
                       - jnp.exp(jnp.sum(attn_lam_q2[j].astype(F32) * attn_lam_k2[j].astype(F32)))
                       + lam_init).reshape(1)
                qkv3 = qkv.reshape(bsz, s, 3 * hd)
                if si == 0:
                    o = _attn_prompt(qkv3, rel_bias, lam, attn_subln_g[j], lam_init)
                else:
                    past = cache_k.shape[2]
                    o = _attn_sample(qkv3, cache_k.reshape(-1, past * H_B, 2 * DH_B),
                                     cache_v.reshape(-1, past * H_B, 2 * DH_B),
                                     j, rel_bias, lam, attn_subln_g[j], lam_init)
                kvs[si].append((qkv3[:, :, hd:2 * hd].reshape(bsz, s, H_B, 2 * DH_B),
                                qkv3[:, :, 2 * hd:].reshape(bsz, s, H_B, 2 * DH_B)))
                w_out = attn_w_out[j]
            else:
                xw = _matmul(x, rglru_w_in[j].astype(BF16), tm=tm, tn=1024).reshape(bsz, s, 2 * d)
                if si == 0:
                    cbuf = jnp.zeros((bsz, CONV_W - 1, d), F32)
                    h0 = jnp.zeros((bsz, d), F32)
                else:
                    cbuf, h0 = state_rglru_conv[j].astype(F32), state_rglru_h[j].astype(F32)
                cbuf8 = jnp.pad(cbuf, ((0, 0), (8 - (CONV_W - 1), 0), (0, 0)))
                o, hlast = _rglru(xw, cbuf8, h0.reshape(bsz, 1, d), rglru_conv_w[j], rglru_conv_b[j],
                                  rglru_w_a[j], rglru_b_a[j], rglru_w_x[j], rglru_b_x[j], rglru_lam[j],
                                  rows=min(s, 256))
                cvs[si].append(xw[:, s - (CONV_W - 1):, d:])
                hls[si].append(hlast.reshape(bsz, d))
                w_out = rglru_w_out[j]
            x = _proj_ln(o.reshape(t, d), w_out.astype(BF16), x, ln1_g[i], ln1_b[i], tm=tm)
            x = _moe_ln(x, moe_w_router[i], moe_router_bias[i], moe_w_in[i], moe_w_down[i],
                        moe_ws_in[i], moe_ws_down[i], ln2_g[i], ln2_b[i], tm=tm)
            new_streams.append((x, bsz, s))
        streams = new_streams

    yp = streams[0][0].reshape(bp, sp, d)
    ys = streams[1][0].reshape(bs, ss, d)
    st = lambda xs: jnp.stack(xs)
    return (yp, ys, st(hs[0]), st(hs[1]),
            st([kv[0] for kv in kvs[0]]), st([kv[1] for kv in kvs[0]]),
            st([kv[0] for kv in kvs[1]]), st([kv[1] for kv in kvs[1]]),
            st(cvs[0]), st(hls[0]), st(cvs[1]), st(hls[1]))
```

```python
import functools
import math

import jax
import jax.numpy as jnp
from jax import lax
from jax.experimental import pallas as pl
from jax.experimental.pallas import tpu as pltpu

F32 = jnp.float32
BF16 = jnp.bfloat16
F8 = jnp.float8_e4m3fn
F8_MAX = 448.0

DEPTH = 4
CHUNK = 64
N_MIXERS = 3
H_A = 8
H_B = 8
DH_B = 64
REL_BUCKETS = 32
REL_MAX_DIST = 1024
N_BLK_C = 4
CONV_W = 4
C_RG = 8.0
N_EXPERTS = 64
D_EXPERT = 128
TOP_K = 8
N_GROUPS = 8
TOPK_GROUPS = 4
ROUTED_SCALE = 2.5
ALPHA = (2 * DEPTH) ** 0.25
LN_EPS = 1e-5
RMS_EPS = 1e-6

VMEM_LIMIT = 56 * 1024 * 1024
LOG2E = math.log2(math.e)
REC_C = 32
ATT_T = 256
ATT_G = 4
MOE_EB = 8

_NT = (((1,), (1,)), ((), ()))
_TN = (((0,), (0,)), ((), ()))


def _cparams(sem):
    return pltpu.CompilerParams(dimension_semantics=sem, vmem_limit_bytes=VMEM_LIMIT)


def _sigmoid(x):
    return 1.0 / (1.0 + jnp.exp(-x))


def _layer_norm(z, g, b):
    mu = jnp.mean(z, -1, keepdims=True)
    zc = z - mu
    var = jnp.mean(zc * zc, -1, keepdims=True)
    return zc * lax.rsqrt(var + LN_EPS) * g + b


def _mm_kernel(x_ref, w_ref, o_ref):
    o_ref[...] = jnp.dot(x_ref[...].astype(BF16), w_ref[...],
                         preferred_element_type=F32).astype(o_ref.dtype)


def _matmul(x, w, *, tm, tn, out_dtype=F32):
    t, k = x.shape
    n = w.shape[1]
    return pl.pallas_call(
        _mm_kernel,
        grid=(n // tn, t // tm),
        in_specs=[pl.BlockSpec((tm, k), lambda j, i: (i, 0)),
                  pl.BlockSpec((k, tn), lambda j, i: (0, j))],
        out_specs=pl.BlockSpec((tm, tn), lambda j, i: (i, j)),
        out_shape=jax.ShapeDtypeStruct((t, n), out_dtype),
        compiler_params=_cparams(("arbitrary", "arbitrary")),
        name="proj_in",
    )(x, w)


def _proj_ln_kernel(o_ref, w_ref, x_ref, g_ref, b_ref, out_ref):
    y = jnp.dot(o_ref[...], w_ref[...], preferred_element_type=F32)
    out_ref[...] = _layer_norm(ALPHA * x_ref[...] + y, g_ref[...], b_ref[...])


def _proj_ln(o, w, x, g, b, *, tm):
    t, d = x.shape
    k = o.shape[1]
    return pl.pallas_call(
        _proj_ln_kernel,
        grid=(t // tm,),
        in_specs=[pl.BlockSpec((tm, k), lambda i: (i, 0)),
                  pl.BlockSpec((k, d), lambda i: (0, 0)),
                  pl.BlockSpec((tm, d), lambda i: (i, 0)),
                  pl.BlockSpec((1, d), lambda i: (0, 0)),
                  pl.BlockSpec((1, d), lambda i: (0, 0))],
        out_specs=pl.BlockSpec((tm, d), lambda i: (i, 0)),
        out_shape=jax.ShapeDtypeStruct((t, d), F32),
        compiler_params=_cparams(("arbitrary",)),
        name="proj_out_ln",
    )(o, w, x, g.reshape(1, d), b.reshape(1, d))


def _hgrn_kernel(q_ref, fz_ref, v_ref, g_ref, lb_ref, gn_ref, s0_ref, o_ref, sfin_ref, st_scr,
                 *, chunk):
    c = pl.program_id(1)
    nh, dv, dk = st_scr.shape

    @pl.when(c == 0)
    def _():
        for h in range(nh):
            st_scr[h] = s0_ref[0, h].T

    fz = fz_ref[0]
    rows = fz.shape[0]
    lb = lb_ref[...]
    e = jnp.exp(-jnp.abs(fz))
    r = 1.0 / (1.0 + e)
    pos = fz >= 0
    sig = jnp.where(pos, r, e * r)
    nsig = jnp.where(pos, e * r, r)
    logf = jnp.log(lb + (1.0 - lb) * sig)
    kk = (1.0 - lb) * nsig
    qv = q_ref[0]
    q = qv * _sigmoid(qv)
    v = v_ref[0].astype(BF16)

    ri = lax.broadcasted_iota(jnp.int32, (rows, rows), 0)
    ci = lax.broadcasted_iota(jnp.int32, (rows, rows), 1)
    tri = jnp.where((ri // chunk == ci // chunk) & (ci <= ri), 1.0, 0.0).astype(BF16)
    hi = logf.astype(BF16)
    r1 = logf - hi.astype(F32)
    mid = r1.astype(BF16)
    lo = (r1 - mid.astype(F32)).astype(BF16)
    bc = (jnp.dot(tri, hi, preferred_element_type=F32)
          + jnp.dot(tri, mid, preferred_element_type=F32)
          + jnp.dot(tri, lo, preferred_element_type=F32))

    causal = (lax.broadcasted_iota(jnp.int32, (chunk, chunk), 0)
              >= lax.broadcasted_iota(jnp.int32, (chunk, chunk), 1))
    outs = [[] for _ in range(nh)]
    for i in range(rows // chunk):
        sl = slice(i * chunk, (i + 1) * chunk)
        b = bc[sl]
        bm = b[chunk // 2 - 1:chunk // 2]
        be = b[chunk - 1:chunk]
        a_fac = q[sl] * jnp.exp(b - bm)
        k_fac = kk[sl] * jnp.exp(bm - b)
        qe = (a_fac * jnp.exp(bm)).astype(BF16)
        kd = (k_fac * jnp.exp(be - bm)).astype(BF16)
        a16 = a_fac.astype(BF16)
        k16 = k_fac.astype(BF16)
        dec = jnp.exp(be)
        for h in range(nh):
            hs = slice(h * dk, (h + 1) * dk)
            st = st_scr[h]
            o = lax.dot_general(qe[:, hs], st.astype(BF16), _NT, preferred_element_type=F32)
            sc = lax.dot_general(a16[:, hs], k16[:, hs], _NT, preferred_element_type=F32)
            sc = jnp.where(causal, sc, 0.0)
            o = o + jnp.dot(sc.astype(BF16), v[sl, hs], preferred_element_type=F32)
            st_scr[h] = st * dec[:, hs] + lax.dot_general(v[sl, hs], kd[:, hs], _TN,
                                                          preferred_element_type=F32)
            outs[h].append(o)
    gv = g_ref[0]
    gate = gv * _sigmoid(gv)
    for h in range(nh):
        hs = slice(h * dk, (h + 1) * dk)
        o = jnp.concatenate(outs[h], axis=0)
        o = o * lax.rsqrt(jnp.mean(o * o, -1, keepdims=True) + RMS_EPS) * gn_ref[...] * gate[:, hs]
        o_ref[0, :, hs] = o.astype(o_ref.dtype)

    @pl.when(c == pl.num_programs(1) - 1)
    def _():
        for h in range(nh):
            sfin_ref[0, h] = st_scr[h].T


def _hgrn(xw, s0, lb, gn_g, *, rows):
    bsz, s, _ = xw.shape
    h, dk, dv = s0.shape[1:]
    sec = lambda k: pl.BlockSpec((1, rows, h * dk), lambda b, c, k=k: (b, c, k))
    return pl.pallas_call(
        functools.partial(_hgrn_kernel, chunk=REC_C),
        grid=(bsz, s // rows),
        in_specs=[sec(0), sec(1), sec(2), sec(3),
                  pl.BlockSpec((1, h * dk), lambda b, c: (0, 0)),
                  pl.BlockSpec((1, dv), lambda b, c: (0, 0)),
                  pl.BlockSpec((1, h, dk, dv), lambda b, c: (b, 0, 0, 0))],
        out_specs=[pl.BlockSpec((1, rows, h * dv), lambda b, c: (b, c, 0)),
                   pl.BlockSpec((1, h, dk, dv), lambda b, c: (b, 0, 0, 0))],
        out_shape=[jax.ShapeDtypeStruct((bsz, s, h * dv), BF16),
                   jax.ShapeDtypeStruct((bsz, h, dk, dv), F32)],
        scratch_shapes=[pltpu.VMEM((h, dv, dk), F32)],
        compiler_params=_cparams(("arbitrary", "arbitrary")),
        name="hgrn2",
    )(xw, xw, xw, xw, lb.reshape(1, h * dk), gn_g.reshape(1, dv), s0)


def _rel_bucket(rel):
    nb = REL_BUCKETS // 2
    max_exact = nb // 2
    ret = jnp.where(rel > 0, nb, 0)
    n = jnp.abs(rel)
    nf = jnp.maximum(n, 1).astype(F32)
    large = max_exact + (jnp.log(nf / max_exact) / math.log(REL_MAX_DIST / max_exact)
                         * (nb - max_exact)).astype(jnp.int32)
    large = jnp.minimum(large, nb - 1)
    return ret + jnp.where(n < max_exact, n, large)


def _saturation_distance():
    nb = REL_BUCKETS // 2
    max_exact = nb // 2
    return math.ceil(max_exact * (REL_MAX_DIST / max_exact) ** ((nb - 1 - max_exact) / (nb - max_exact))) + 1


def _bias_rows(rel_bias, rel0s, width):
    rel = jnp.asarray(rel0s, jnp.int32)[:, None] - jnp.arange(width, dtype=jnp.int32)[None, :]
    seg = rel_bias.astype(F32)[_rel_bucket(rel)] * LOG2E
    return jnp.transpose(seg, (2, 0, 1))[:, :, None, :]


def _far_bias(rel_bias):
    return rel_bias.astype(F32)[_rel_bucket(jnp.int32(-_saturation_distance()))] * LOG2E


def _bias_tile_t(seg, tk, shift):
    w = seg.shape[1]
    return pltpu.roll(jnp.broadcast_to(seg, (tk, w)), w - tk + shift, 1, stride=1, stride_axis=0)


def _stack_q(q):
    q = q * (DH_B ** -0.5 * LOG2E)
    lane = lax.broadcasted_iota(jnp.int32, q.shape, 1)
    qs = jnp.concatenate([jnp.where(lane < DH_B, q, 0.0), jnp.where(lane >= DH_B, q, 0.0)], axis=0)
    return qs.T.astype(BF16)


def _online(st, m, l):
    m_new = jnp.maximum(m, jnp.max(st, axis=0, keepdims=True))
    alpha = jnp.exp2(m - m_new)
    p = jnp.exp2(st - m_new)
    return m_new, alpha * l + jnp.sum(p, axis=0, keepdims=True), alpha, p.astype(BF16)


def _attn_out(ot, g_ref, out_scale):
    return ot * lax.rsqrt(jnp.mean(ot * ot, -1, keepdims=True) + RMS_EPS) * g_ref[...] * out_scale


def _attn_prompt_kernel(lam_ref, fb_ref, q_ref, k_ref, v_ref, seg_ref, g_ref, o_ref,
                        kb_scr, vt_scr, qst_scr, acc_scr, *, d_far, out_scale):
    hg = pl.program_id(1)
    qi = pl.program_id(2)
    ng, n, t, dv = kb_scr.shape
    w2 = 2 * t

    @pl.when(qi == 0)
    def _():
        for u in range(ng):
            for c in range(n):
                kb_scr[u, c] = k_ref[0, c * t:(c + 1) * t, u * dv:(u + 1) * dv].astype(BF16)
                vt_scr[u, c] = v_ref[0, c * t:(c + 1) * t, u * dv:(u + 1) * dv].T.astype(BF16)

    for u in range(ng):
        qst_scr[u] = _stack_q(q_ref[0, :, u * dv:(u + 1) * dv])
    acc_scr[...] = jnp.zeros(acc_scr.shape, F32)

    def scores(ki, u):
        return jnp.dot(kb_scr[u, ki], qst_scr[u], preferred_element_type=F32)

    def biased(st, seg, vis=None):
        bt = _bias_tile_t(seg, t, 0)[:, :t]
        s1, s2 = st[:, :t] + bt, st[:, t:] + bt
        if vis is not None:
            s1, s2 = jnp.where(vis, s1, -1e30), jnp.where(vis, s2, -1e30)
        return jnp.concatenate([s1, s2], axis=1)

    def update(ki, carry, sts):
        m, l, alpha, p = _online(jnp.concatenate(sts, axis=1), *carry)
        for u in range(ng):
            sl = slice(u * w2, (u + 1) * w2)
            acc_scr[u] = alpha[:, sl] * acc_scr[u] + jnp.dot(vt_scr[u, ki], p[:, sl],
                                                             preferred_element_type=F32)
        return m, l

    def far(ki, carry):
        return update(ki, carry, [scores(ki, u) + fb_ref[hg * ng + u] for u in range(ng)])

    def near(ki, carry):
        return update(ki, carry, [biased(scores(ki, u), seg_ref[u, qi - ki]) for u in range(ng)])

    n_far = jnp.maximum(qi - (d_far - 1), 0)
    carry = (jnp.full((1, ng * w2), -1e30, F32), jnp.zeros((1, ng * w2), F32))
    carry = lax.fori_loop(0, n_far, far, carry)
    carry = lax.fori_loop(n_far, qi, near, carry)
    vis = (lax.broadcasted_iota(jnp.int32, (t, t), 0) // CHUNK
           <= lax.broadcasted_iota(jnp.int32, (t, t), 1) // CHUNK)
    _, l = update(qi, carry, [biased(scores(qi, u), seg_ref[u, 0], vis) for u in range(ng)])
    inv = 1.0 / l
    for u in range(ng):
        on = acc_scr[u] * inv[:, u * w2:(u + 1) * w2]
        ot = (on[:, :t] - lam_ref[0] * on[:, t:]).T
        o_ref[0, :, u * dv:(u + 1) * dv] = _attn_out(ot, g_ref, out_scale).astype(o_ref.dtype)


def _attn_prompt(qkv, rel_bias, lam, subln_g, lam_init):
    bsz, s, _ = qkv.shape
    h, dv, t, g = H_B, 2 * DH_B, ATT_T, ATT_G
    n = s // t
    w = 2 * t
    d_far = -(-(_saturation_distance() - 1) // t) + 1
    segs = _bias_rows(rel_bias, [t - d * t for d in range(n)], w)
    kv = lambda sec: pl.BlockSpec((1, s, g * dv), lambda b, hg, qi, sec=sec: (b, 0, sec * (h // g) + hg))
    return pl.pallas_call(
        functools.partial(_attn_prompt_kernel, d_far=d_far, out_scale=1.0 - lam_init),
        grid=(bsz, h // g, n),
        in_specs=[pl.BlockSpec(memory_space=pltpu.SMEM), pl.BlockSpec(memory_space=pltpu.SMEM),
                  pl.BlockSpec((1, t, g * dv), lambda b, hg, qi: (b, qi, hg)),
                  kv(1), kv(2),
                  pl.BlockSpec((g, n, 1, w), lambda b, hg, qi: (hg, 0, 0, 0)),
                  pl.BlockSpec((1, dv), lambda b, hg, qi: (0, 0))],
        out_specs=pl.BlockSpec((1, t, g * dv), lambda b, hg, qi: (b, qi, hg)),
        out_shape=jax.ShapeDtypeStruct((bsz, s, h * dv), BF16),
        scratch_shapes=[pltpu.VMEM((g, n, t, dv), BF16), pltpu.VMEM((g, n, dv, t), BF16),
                        pltpu.VMEM((g, dv, 2 * t), BF16), pltpu.VMEM((g, dv, 2 * t), F32)],
        compiler_params=_cparams(("arbitrary",) * 3),
        name="diff_attn_prompt",
    )(lam, _far_bias(rel_bias), qkv, qkv, qkv, segs, subln_g.reshape(1, dv))


def _attn_sample_kernel(lam_ref, fb_ref, q_ref, kp_ref, vp_ref, segp_ref, kn_ref, vn_ref, segn_ref,
                        g_ref, o_ref, *, tkp, n_past, far_tiles, out_scale):
    hg = pl.program_id(1)
    tq = q_ref.shape[1]
    dv = g_ref.shape[1]
    ng = q_ref.shape[2] // dv
    nh = kp_ref.shape[1] // (n_past * tkp)
    w2 = 2 * tq
    qst = [_stack_q(q_ref[0, :, u * dv:(u + 1) * dv]) for u in range(ng)]
    lane = lax.broadcasted_iota(jnp.int32, (1, w2), 1)

    def bias(seg, tk):
        return jnp.where(lane < tq, _bias_tile_t(seg, tk, 0)[:, :w2], _bias_tile_t(seg, tk, tq)[:, :w2])

    m = jnp.full((1, ng * w2), -1e30, F32)
    l = jnp.zeros((1, ng * w2), F32)
    acc = [jnp.zeros((dv, w2), F32) for _ in range(ng)]
    for c in range(n_past + 1):
        sts, vs = [], []
        for u in range(ng):
            if c < n_past:
                rows = pl.ds(c * tkp * nh + hg * ng + u, tkp, stride=nh)
                k, v = kp_ref[0, rows, :], vp_ref[0, rows, :]
            else:
                k, v = kn_ref[0, :, u * dv:(u + 1) * dv], vn_ref[0, :, u * dv:(u + 1) * dv]
            st = jnp.dot(k.astype(BF16), qst[u], preferred_element_type=F32)
            if c < far_tiles:
                st = st + fb_ref[hg * ng + u]
            elif c < n_past:
                st = st + bias(segp_ref[u, c - far_tiles], tkp)
            else:
                st = st + bias(segn_ref[u, 0], tq)
            sts.append(st)
            vs.append(v.astype(BF16))
        m, l, alpha, p = _online(jnp.concatenate(sts, axis=1), m, l)
        for u in range(ng):
            sl = slice(u * w2, (u + 1) * w2)
            acc[u] = alpha[:, sl] * acc[u] + lax.dot_general(vs[u], p[:, sl], _TN,
                                                             preferred_element_type=F32)
    inv = 1.0 / l
    for u in range(ng):
        on = acc[u] * inv[:, u * w2:(u + 1) * w2]
        ot = (on - lam_ref[0] * pltpu.roll(on, tq, 1)).T[:tq]
        o_ref[0, :, u * dv:(u + 1) * dv] = _attn_out(ot, g_ref, out_scale).astype(o_ref.dtype)


def _attn_sample(qkv, k_past, v_past, layer, rel_bias, lam, subln_g, lam_init):
    bsz, s, _ = qkv.shape
    h, dv, tkp, g = H_B, 2 * DH_B, ATT_T, ATT_G
    past = k_past.shape[1] // h
    n_past = past // tkp
    far_tiles = sum(1 for c in range(n_past) if past - (c * tkp + tkp - 1) >= _saturation_distance())
    near = list(range(far_tiles, n_past)) or [n_past - 1]
    segp = _bias_rows(rel_bias, [c * tkp - past + tkp for c in near], 2 * s + tkp)
    segn = _bias_rows(rel_bias, [s], 4 * s)
    pk = lambda: pl.BlockSpec((1, past * h, dv), lambda b, hg: (layer * bsz + b, 0, 0))
    nk = lambda sec: pl.BlockSpec((1, s, g * dv), lambda b, hg, sec=sec: (b, 0, sec * (h // g) + hg))
    return pl.pallas_call(
        functools.partial(_attn_sample_kernel, tkp=tkp, n_past=n_past, far_tiles=far_tiles,
                          out_scale=1.0 - lam_init),
        grid=(bsz, h // g),
        in_specs=[pl.BlockSpec(memory_space=pltpu.SMEM), pl.BlockSpec(memory_space=pltpu.SMEM),
                  nk(0), pk(), pk(),
                  pl.BlockSpec((g, len(near), 1, 2 * s + tkp), lambda b, hg: (hg, 0, 0, 0)),
                  nk(1), nk(2),
                  pl.BlockSpec((g, 1, 1, 4 * s), lambda b, hg: (hg, 0, 0, 0)),
                  pl.BlockSpec((1, dv), lambda b, hg: (0, 0))],
        out_specs=pl.BlockSpec((1, s, g * dv), lambda b, hg: (b, 0, hg)),
        out_shape=jax.ShapeDtypeStruct((bsz, s, h * dv), BF16),
        compiler_params=_cparams(("arbitrary",) * 2),
        name="diff_attn_sample",
    )(lam, _far_bias(rel_bias), qkv, k_past, v_past, segp, qkv, qkv, segn, subln_g.reshape(1, dv))


def _rglru_kernel(xw_ref, cw_ref, cb_ref, wa_ref, ba_ref, wx_ref, bx_ref, lam_ref, cbuf_ref, h0_ref,
                  y_ref, hlast_ref, ext_scr, h_scr):
    t = pl.program_id(1)
    tt = y_ref.shape[1]
    d = y_ref.shape[2]
    pad = ext_scr.shape[0] - tt

    @pl.when(t == 0)
    def _():
        ext_scr[0:pad] = cbuf_ref[0]
        h_scr[...] = h0_ref[0]

    gate = xw_ref[0, :, 0:d]
    u = xw_ref[0, :, d:2 * d]
    ext_scr[pad:pad + tt] = u
    conv = cb_ref[...] + cw_ref[CONV_W - 1:CONV_W] * u
    for j in range(1, CONV_W):
        conv = conv + cw_ref[CONV_W - 1 - j:CONV_W - j] * ext_scr[pad - j:pad - j + tt]
    ext_scr[0:pad] = ext_scr[tt:tt + pad]

    cb16 = conv.astype(BF16)
    blk = d // N_BLK_C
    ra, rx = [], []
    for n in range(N_BLK_C):
        cs = cb16[:, n * blk:(n + 1) * blk]
        ra.append(jnp.dot(cs, wa_ref[n], preferred_element_type=F32))
        rx.append(jnp.dot(cs, wx_ref[n], preferred_element_type=F32))
    r = _sigmoid(jnp.concatenate(ra, axis=1) + ba_ref[...])
    ig = _sigmoid(jnp.concatenate(rx, axis=1) + bx_ref[...])
    nl = -lam_ref[...]
    sp = jnp.maximum(nl, 0.0) + jnp.log(1.0 + jnp.exp(-jnp.abs(nl)))
    log_a = (-C_RG) * r * sp
    a = jnp.exp(log_a)
    xin = jnp.sqrt(-jnp.tanh(log_a) * (1.0 + a * a)) * ig * conv

    row = lax.broadcasted_iota(jnp.int32, (tt, d), 0)
    ca, cx = a, xin
    sft = 1
    while sft < tt:
        keep = row >= sft
        pa = jnp.where(keep, pltpu.roll(ca, sft, 0), 1.0)
        px = jnp.where(keep, pltpu.roll(cx, sft, 0), 0.0)
        cx = ca * px + cx
        ca = ca * pa
        sft *= 2
    hh = ca * h_scr[...] + cx
    h_scr[...] = hh[tt - 1:tt]
    gelu = 0.5 * gate * (1.0 + jnp.tanh(math.sqrt(2.0 / math.pi) * (gate + 0.044715 * gate * gate * gate)))
    y_ref[0] = (hh * gelu).astype(y_ref.dtype)

    @pl.when(t == pl.num_programs(1) - 1)
    def _():
        hlast_ref[0] = hh[tt - 1:tt]


def _rglru(xw, cbuf8, h0, conv_w, conv_b, w_a, b_a, w_x, b_x, lam, *, rows):
    bsz, s, d2 = xw.shape
    d = d2 // 2
    pad = cbuf8.shape[1]
    blk = d // N_BLK_C
    row = lambda a: a.reshape(1, d).astype(F32)
    full2 = lambda shp: pl.BlockSpec(shp, lambda b, t: (0,) * len(shp))
    return pl.pallas_call(
        _rglru_kernel,
        grid=(bsz, s // rows),
        in_specs=[pl.BlockSpec((1, rows, d2), lambda b, t: (b, t, 0)),
                  full2((CONV_W, d)), full2((1, d)),
                  full2((N_BLK_C, blk, blk)), full2((1, d)),
                  full2((N_BLK_C, blk, blk)), full2((1, d)),
                  full2((1, d)),
                  pl.BlockSpec((1, pad, d), lambda b, t: (b, 0, 0)),
                  pl.BlockSpec((1, 1, d), lambda b, t: (b, 0, 0))],
        out_specs=[pl.BlockSpec((1, rows, d), lambda b, t: (b, t, 0)),
                   pl.BlockSpec((1, 1, d), lambda b, t: (b, 0, 0))],
        out_shape=[jax.ShapeDtypeStruct((bsz, s, d), BF16),
                   jax.ShapeDtypeStruct((bsz, 1, d), F32)],
        scratch_shapes=[pltpu.VMEM((rows + pad, d), F32), pltpu.VMEM((1, d), F32)],
        compiler_params=_cparams(("arbitrary", "arbitrary")),
        name="rglru",
    )(xw, conv_w.astype(F32), row(conv_b), w_a.astype(BF16), row(b_a), w_x.astype(BF16), row(b_x),
      row(lam), cbuf8, h0)


def _route(x, wr_t, rbias):
    ne = wr_t.shape[0]
    tm = x.shape[0]
    gsz = ne // N_GROUPS
    logits = lax.dot_general(wr_t, x, _NT, precision=lax.Precision.HIGHEST,
                             preferred_element_type=F32)
    score = _sigmoid(logits)
    biased = score + rbias
    neg = -jnp.inf
    sub = lax.broadcasted_iota(jnp.int32, (gsz, tm), 0)
    gscore = []
    for g in range(N_GROUPS):
        vg = biased[g * gsz:(g + 1) * gsz]
        m1 = jnp.max(vg, axis=0, keepdims=True)
        first = jnp.min(jnp.where(vg == m1, sub, gsz), axis=0, keepdims=True)
        m2 = jnp.max(jnp.where(sub == first, neg, vg), axis=0, keepdims=True)
        gscore.append(m1 + m2)
    masked = []
    for g in range(N_GROUPS):
        rank = jnp.zeros((1, tm), jnp.int32)
        for o in range(N_GROUPS):
            if o == g:
                continue
            ahead = (gscore[o] >= gscore[g]) if o < g else (gscore[o] > gscore[g])
            rank = rank + ahead.astype(jnp.int32)
        keep = rank < TOPK_GROUPS
        masked.append(jnp.where(keep, biased[g * gsz:(g + 1) * gsz], neg))
    cur = jnp.concatenate(masked, axis=0)
    eidx = lax.broadcasted_iota(jnp.int32, (ne, tm), 0)
    chosen = jnp.zeros((ne, tm), F32)
    for _ in range(TOP_K):
        m = jnp.max(cur, axis=0, keepdims=True)
        first = jnp.min(jnp.where(cur == m, eidx, ne), axis=0, keepdims=True)
        hit = eidx == first
        chosen = jnp.where(hit, 1.0, chosen)
        cur = jnp.where(hit, neg, cur)
    wsel = chosen * score
    return wsel / jnp.sum(wsel, axis=0, keepdims=True) * ROUTED_SCALE


def _swiglu(h, f):
    hg = h[:, :f]
    return hg * _sigmoid(hg) * h[:, f:]


def _moe_kernel(inv_ref, x_ref, wr_ref, rb_ref, wsi_ref, wsd_ref, wi_ref, wd_ref, g_ref, b_ref, out_ref,
                xb_scr, x8_scr, acc_scr, gate_scr, *, eb):
    j = pl.program_id(1)
    f = wsd_ref.shape[0]

    @pl.when(j == 0)
    def _():
        x = x_ref[...]
        xb = x.astype(BF16)
        xb_scr[...] = xb
        x8_scr[...] = x.astype(F8)
        w_et = _route(x, wr_ref[...], rb_ref[...])
        ne, tm = w_et.shape
        gate_scr[...] = jnp.concatenate([w_et, jnp.zeros((gate_scr.shape[1] - ne, tm), F32)], axis=0).T
        sh = _swiglu(jnp.dot(xb, wsi_ref[...], preferred_element_type=F32), f)
        acc_scr[...] = jnp.dot(sh.astype(BF16), wsd_ref[...], preferred_element_type=F32)

    x8 = x8_scr[...]
    gates = gate_scr[...]
    lane = lax.broadcasted_iota(jnp.int32, gates.shape, 1)
    acts = []
    for i in range(eb):
        e = j * eb + i
        h = jnp.dot(x8, wi_ref[i], preferred_element_type=F32) * inv_ref[e]
        gcol = jnp.sum(jnp.where(lane == e, gates, 0.0), axis=1, keepdims=True)
        acts.append((_swiglu(h, f) * gcol).astype(BF16))
    acc_scr[...] += jnp.dot(jnp.concatenate(acts, axis=1), wd_ref[0], preferred_element_type=F32)

    @pl.when(j == pl.num_programs(1) - 1)
    def _():
        out_ref[...] = _layer_norm(ALPHA * x_ref[...] + acc_scr[...], g_ref[...], b_ref[...])


def _fp8_weights(w):
    amax = jnp.max(jnp.abs(w.astype(F32)), axis=tuple(range(1, w.ndim)))
    k = jnp.floor(jnp.log2(F8_MAX / jnp.maximum(amax, 1e-30))) - 1.0
    scale = jnp.exp2(jnp.clip(k, -60.0, 60.0))
    return (w.astype(F32) * scale.reshape((-1,) + (1,) * (w.ndim - 1))).astype(F8), 1.0 / scale


def _moe_ln(x, w_router, router_bias, w_in, w_down, ws_in, ws_down, g, b, *, tm):
    t, d = x.shape
    ne, _, f2 = w_in.shape
    f = f2 // 2
    eb = MOE_EB
    wd = w_down.astype(BF16).reshape(ne // eb, eb * f, d)
    w8, inv_scale = _fp8_weights(w_in)
    const = lambda shp: pl.BlockSpec(shp, lambda i, j: (0,) * len(shp))
    return pl.pallas_call(
        functools.partial(_moe_kernel, eb=eb),
        grid=(t // tm, ne // eb),
        in_specs=[pl.BlockSpec(memory_space=pltpu.SMEM),
                  pl.BlockSpec((tm, d), lambda i, j: (i, 0)),
                  const((ne, d)), const((ne, 1)),
                  const((d, f2)), const((f, d)),
                  pl.BlockSpec((eb, d, f2), lambda i, j: (j, 0, 0)),
                  pl.BlockSpec((1, eb * f, d), lambda i, j: (j, 0, 0)),
                  const((1, d)), const((1, d))],
        out_specs=pl.BlockSpec((tm, d), lambda i, j: (i, 0)),
        out_shape=jax.ShapeDtypeStruct((t, d), F32),
        scratch_shapes=[pltpu.VMEM((tm, d), BF16), pltpu.VMEM((tm, d), F8), pltpu.VMEM((tm, d), F32),
                        pltpu.VMEM((tm, 128), F32)],
        compiler_params=_cparams(("arbitrary", "arbitrary")),
        name="moe_ln",
    )(inv_scale, x, w_router.T.astype(F32), router_bias.reshape(ne, 1).astype(F32),
      ws_in.astype(BF16), ws_down.astype(BF16), w8, wd,
      g.reshape(1, d), b.reshape(1, d))


def _lambda_init(layer_idx):
    return 0.8 - 0.6 * math.exp(-0.3 * layer_idx)


def kernel(x_prompt, x_sample, state_hgrn_s, cache_k, cache_v, state_rglru_conv, state_rglru_h, hgrn_lb_logits, hgrn_w_in, hgrn_gn_g, hgrn_w_out, rel_bias, attn_w_qkv, attn_lam_q1, attn_lam_k1, attn_lam_q2, attn_lam_k2, attn_subln_g, attn_w_out, rglru_w_in, rglru_conv_w, rglru_conv_b, rglru_w_a, rglru_b_a, rglru_w_x, rglru_b_x, rglru_lam, rglru_w_out, ln1_g, ln1_b, ln2_g, ln2_b, moe_w_router, moe_router_bias, moe_w_in, moe_w_down, moe_ws_in, moe_ws_down):
    bp, sp, d = x_prompt.shape
    bs, ss, _ = x_sample.shape
    lb_p = jax.nn.softmax(hgrn_lb_logits.astype(F32), axis=0)
    lower_bounds = jnp.clip(jnp.cumsum(lb_p, axis=0) - lb_p[0], 0.0, 1.0)

    streams = [(x_prompt.reshape(bp * sp, d), bp, sp), (x_sample.reshape(bs * ss, d), bs, ss)]
    tile = lambda t: 1024 if t % 1024 == 0 else 512
    hs, kvs, cvs, hls = [[], []], [[], []], [[], []], [[], []]
    for i in range(DEPTH):
        j, kind = i // N_MIXERS, i % N_MIXERS
        new_streams = []
        for si, (x, bsz, s) in enumerate(streams):
            t = bsz * s
            tm = tile(t)
            if kind == 0:
                xw = _matmul(x, hgrn_w_in[j].astype(BF16), tm=tm, tn=1024)
                s0 = (jnp.zeros((bsz, H_A, d // H_A, d // H_A), F32) if si == 0
                      else state_hgrn_s[j].astype(F32))
                o, s_fin = _hgrn(xw.reshape(bsz, s, -1), s0, lower_bounds[i], hgrn_gn_g[j],
                                 rows=min(s, 256))
                hs[si].append(s_fin)
                w_out = hgrn_w_out[j]
            elif kind == 1:
                qkv = _matmul(x, attn_w_qkv[j].astype(BF16), tm=tm, tn=1024)
                hd = H_B * 2 * DH_B
                lam_init = _lambda_init(i)
                lam = (jnp.exp(jnp.sum(attn_lam_q1[j].astype(F32) * attn_lam_k1[j].astype(F32)))
                       - jnp.exp(jnp.sum(attn_lam_q2[j].astype(F32) * attn_lam_k2[j].astype(F32)))
                       + lam_init).reshape(1)
                qkv3 = qkv.reshape(bsz, s, 3 * hd)
                if si == 0:
                    o = _attn_prompt(qkv3, rel_bias, lam, attn_subln_g[j], lam_init)
                else:
                    past = cache_k.shape[2]
                    o = _attn_sample(qkv3, cache_k.reshape(-1, past * H_B, 2 * DH_B),
                                     cache_v.reshape(-1, past * H_B, 2 * DH_B),
                                     j, rel_bias, lam, attn_subln_g[j], lam_init)
                kvs[si].append((qkv3[:, :, hd:2 * hd].reshape(bsz, s, H_B, 2 * DH_B),
                                qkv3[:, :, 2 * hd:].reshape(bsz, s, H_B, 2 * DH_B)))
                w_out = attn_w_out[j]
            else:
                xw = _matmul(x, rglru_w_in[j].astype(BF16), tm=tm, tn=1024).reshape(bsz, s, 2 * d)
                if si == 0:
                    cbuf = jnp.zeros((bsz, CONV_W - 1, d), F32)
                    h0 = jnp.zeros((bsz, d), F32)
                else:
                    cbuf, h0 = state_rglru_conv[j].astype(F32), state_rglru_h[j].astype(F32)
                cbuf8 = jnp.pad(cbuf, ((0, 0), (8 - (CONV_W - 1), 0), (0, 0)))
                o, hlast = _rglru(xw, cbuf8, h0.reshape(bsz, 1, d), rglru_conv_w[j], rglru_conv_b[j],
                                  rglru_w_a[j], rglru_b_a[j], rglru_w_x[j], rglru_b_x[j], rglru_lam[j],
                                  rows=min(s, 256))
                cvs[si].append(xw[:, s - (CONV_W - 1):, d:])
                hls[si].append(hlast.reshape(bsz, d))
                w_out = rglru_w_out[j]
            x = _proj_ln(o.reshape(t, d), w_out.astype(BF16), x, ln1_g[i], ln1_b[i], tm=tm)
            x = _moe_ln(x, moe_w_router[i], moe_router_bias[i], moe_w_in[i], moe_w_down[i],
                        moe_ws_in[i], moe_ws_down[i], ln2_g[i], ln2_b[i], tm=tm)
            new_streams.append((x, bsz, s))
        streams = new_streams

    yp = streams[0][0].reshape(bp, sp, d)
    ys = streams[1][0].reshape(bs, ss, d)
    st = lambda xs: jnp.stack(xs)
    return (yp, ys, st(hs[0]), st(hs[1]),
            st([kv[0] for kv in kvs[0]]), st([kv[1] for kv in kvs[0]]),
            st([kv[0] for kv in kvs[1]]), st([kv[1] for kv in kvs[1]]),
            st(cvs[0]), st(hls[0]), st(cvs[1]), st(hls[1]))
```

```python
import functools
import math

import jax
import jax.numpy as jnp
from jax import lax
from jax.experimental import pallas as pl
from jax.experimental.pallas import tpu as pltpu

F32 = jnp.float32
BF16 = jnp.bfloat16
F8 = jnp.float8_e4m3fn
F8_MAX = 448.0

DEPTH = 4
CHUNK = 64
N_MIXERS = 3
H_A = 8
H_B = 8
DH_B = 64
REL_BUCKETS = 32
REL_MAX_DIST = 1024
N_BLK_C = 4
CONV_W = 4
C_RG = 8.0
N_EXPERTS = 64
D_EXPERT = 128
TOP_K = 8
N_GROUPS = 8
TOPK_GROUPS = 4
ROUTED_SCALE = 2.5
ALPHA = (2 * DEPTH) ** 0.25
LN_EPS = 1e-5
RMS_EPS = 1e-6

VMEM_LIMIT = 56 * 1024 * 1024
LOG2E = math.log2(math.e)
REC_C = 64
ATT_T = 256
ATT_G = 4
ONES_ROWS = 16
MOE_EB = 8

_NT = (((1,), (1,)), ((), ()))
_TN = (((0,), (0,)), ((), ()))


def _cparams(sem):
    return pltpu.CompilerParams(dimension_semantics=sem, vmem_limit_bytes=VMEM_LIMIT)


def _sigmoid(x):
    return 1.0 / (1.0 + jnp.exp(-x))


def _layer_norm(z, g, b):
    mu = jnp.mean(z, -1, keepdims=True)
    zc = z - mu
    var = jnp.mean(zc * zc, -1, keepdims=True)
    return zc * lax.rsqrt(var + LN_EPS) * g + b


def _mm_kernel(x_ref, w_ref, o_ref):
    o_ref[...] = jnp.dot(x_ref[...].astype(BF16), w_ref[...],
                         preferred_element_type=F32).astype(o_ref.dtype)


def _matmul(x, w, *, tm, tn, out_dtype=F32):
    t, k = x.shape
    n = w.shape[1]
    return pl.pallas_call(
        _mm_kernel,
        grid=(n // tn, t // tm),
        in_specs=[pl.BlockSpec((tm, k), lambda j, i: (i, 0)),
                  pl.BlockSpec((k, tn), lambda j, i: (0, j))],
        out_specs=pl.BlockSpec((tm, tn), lambda j, i: (i, j)),
        out_shape=jax.ShapeDtypeStruct((t, n), out_dtype),
        compiler_params=_cparams(("arbitrary", "arbitrary")),
        name="proj_in",
    )(x, w)


def _proj_ln_kernel(o_ref, w_ref, x_ref, g_ref, b_ref, out_ref):
    y = jnp.dot(o_ref[...], w_ref[...], preferred_element_type=F32)
    out_ref[...] = _layer_norm(ALPHA * x_ref[...] + y, g_ref[...], b_ref[...])


def _proj_ln(o, w, x, g, b, *, tm):
    t, d = x.shape
    k = o.shape[1]
    return pl.pallas_call(
        _proj_ln_kernel,
        grid=(t // tm,),
        in_specs=[pl.BlockSpec((tm, k), lambda i: (i, 0)),
                  pl.BlockSpec((k, d), lambda i: (0, 0)),
                  pl.BlockSpec((tm, d), lambda i: (i, 0)),
                  pl.BlockSpec((1, d), lambda i: (0, 0)),
                  pl.BlockSpec((1, d), lambda i: (0, 0))],
        out_specs=pl.BlockSpec((tm, d), lambda i: (i, 0)),
        out_shape=jax.ShapeDtypeStruct((t, d), F32),
        compiler_params=_cparams(("arbitrary",)),
        name="proj_out_ln",
    )(o, w, x, g.reshape(1, d), b.reshape(1, d))


def _hgrn_kernel(q_ref, fz_ref, v_ref, g_ref, lb_ref, gn_ref, s0_ref, o_ref, sfin_ref, st_scr,
                 *, chunk):
    c = pl.program_id(1)
    nh, dv, dk = st_scr.shape

    @pl.when(c == 0)
    def _():
        for h in range(nh):
            st_scr[h] = s0_ref[0, h].T

    rows = fz_ref.shape[1]
    lb = lb_ref[...]
    causal = (lax.broadcasted_iota(jnp.int32, (chunk, chunk), 0)
              >= lax.broadcasted_iota(jnp.int32, (chunk, chunk), 1))
    tri = jnp.where(causal, 1.0, 0.0).astype(BF16)
    sts = [st_scr[h] for h in range(nh)]
    for i in range(rows // chunk):
        sl = slice(i * chunk, (i + 1) * chunk)
        fz = fz_ref[0, sl, :]
        e = jnp.exp(-jnp.abs(fz))
        r = 1.0 / (1.0 + e)
        pos = fz >= 0
        sig = jnp.where(pos, r, e * r)
        nsig = jnp.where(pos, e * r, r)
        logf = jnp.log(lb + (1.0 - lb) * sig)
        kk = (1.0 - lb) * nsig
        hi = logf.astype(BF16)
        r1 = logf - hi.astype(F32)
        mid = r1.astype(BF16)
        lo = (r1 - mid.astype(F32)).astype(BF16)
        b = (jnp.dot(tri, hi, preferred_element_type=F32)
             + jnp.dot(tri, mid, preferred_element_type=F32)
             + jnp.dot(tri, lo, preferred_element_type=F32))
        bm = b[chunk // 2 - 1:chunk // 2]
        be = b[chunk - 1:chunk]
        qv = q_ref[0, sl, :]
        a_fac = qv * _sigmoid(qv) * jnp.exp(b - bm)
        k_fac = kk * jnp.exp(bm - b)
        qe = (a_fac * jnp.exp(bm)).astype(BF16)
        kd = (k_fac * jnp.exp(be - bm)).astype(BF16)
        a16 = a_fac.astype(BF16)
        k16 = k_fac.astype(BF16)
        dec = jnp.exp(be)
        v = v_ref[0, sl, :].astype(BF16)
        gv = g_ref[0, sl, :]
        gate = gv * _sigmoid(gv)
        for h in range(nh):
            hs = slice(h * dk, (h + 1) * dk)
            st = sts[h]
            o = lax.dot_general(qe[:, hs], st.astype(BF16), _NT, preferred_element_type=F32)
            sc = lax.dot_general(a16[:, hs], k16[:, hs], _NT, preferred_element_type=F32)
            sc = jnp.where(causal, sc, 0.0)
            o = o + jnp.dot(sc.astype(BF16), v[:, hs], preferred_element_type=F32)
            sts[h] = st * dec[:, hs] + lax.dot_general(v[:, hs], kd[:, hs], _TN,
                                                       preferred_element_type=F32)
            o = o * lax.rsqrt(jnp.mean(o * o, -1, keepdims=True) + RMS_EPS) * gn_ref[...] * gate[:, hs]
            o_ref[0, sl, hs] = o.astype(o_ref.dtype)
    for h in range(nh):
        st_scr[h] = sts[h]

    @pl.when(c == pl.num_programs(1) - 1)
    def _():
        for h in range(nh):
            sfin_ref[0, h] = st_scr[h].T


def _hgrn(xw, s0, lb, gn_g, *, rows):
    bsz, s, _ = xw.shape
    h, dk, dv = s0.shape[1:]
    sec = lambda k: pl.BlockSpec((1, rows, h * dk), lambda b, c, k=k: (b, c, k))
    return pl.pallas_call(
        functools.partial(_hgrn_kernel, chunk=REC_C),
        grid=(bsz, s // rows),
        in_specs=[sec(0), sec(1), sec(2), sec(3),
                  pl.BlockSpec((1, h * dk), lambda b, c: (0, 0)),
                  pl.BlockSpec((1, dv), lambda b, c: (0, 0)),
                  pl.BlockSpec((1, h, dk, dv), lambda b, c: (b, 0, 0, 0))],
        out_specs=[pl.BlockSpec((1, rows, h * dv), lambda b, c: (b, c, 0)),
                   pl.BlockSpec((1, h, dk, dv), lambda b, c: (b, 0, 0, 0))],
        out_shape=[jax.ShapeDtypeStruct((bsz, s, h * dv), BF16),
                   jax.ShapeDtypeStruct((bsz, h, dk, dv), F32)],
        scratch_shapes=[pltpu.VMEM((h, dv, dk), F32)],
        compiler_params=_cparams(("arbitrary", "arbitrary")),
        name="hgrn2",
    )(xw, xw, xw, xw, lb.reshape(1, h * dk), gn_g.reshape(1, dv), s0)


def _rel_bucket(rel):
    nb = REL_BUCKETS // 2
    max_exact = nb // 2
    ret = jnp.where(rel > 0, nb, 0)
    n = jnp.abs(rel)
    nf = jnp.maximum(n, 1).astype(F32)
    large = max_exact + (jnp.log(nf / max_exact) / math.log(REL_MAX_DIST / max_exact)
                         * (nb - max_exact)).astype(jnp.int32)
    large = jnp.minimum(large, nb - 1)
    return ret + jnp.where(n < max_exact, n, large)


def _saturation_distance():
    nb = REL_BUCKETS // 2
    max_exact = nb // 2
    return math.ceil(max_exact * (REL_MAX_DIST / max_exact) ** ((nb - 1 - max_exact) / (nb - max_exact))) + 1


def _bias_rows(rel_bias, rel0s, width):
    rel = jnp.asarray(rel0s, jnp.int32)[:, None] - jnp.arange(width, dtype=jnp.int32)[None, :]
    seg = rel_bias.astype(F32)[_rel_bucket(rel)] * LOG2E
    return jnp.transpose(seg, (2, 0, 1))[:, :, None, :]


def _far_bias(rel_bias):
    return rel_bias.astype(F32)[_rel_bucket(jnp.int32(-_saturation_distance()))] * LOG2E


def _bias_tile_t(seg, tk, shift):
    w = seg.shape[1]
    return pltpu.roll(jnp.broadcast_to(seg, (tk, w)), w - tk + shift, 1, stride=1, stride_axis=0)


def _stack_q(q):
    q = q * (DH_B ** -0.5 * LOG2E)
    lane = lax.broadcasted_iota(jnp.int32, q.shape, 1)
    qs = jnp.concatenate([jnp.where(lane < DH_B, q, 0.0), jnp.where(lane >= DH_B, q, 0.0)], axis=0)
    return qs.T.astype(BF16)


def _online(st, m, shift=None):
    if shift is None:
        m_new = jnp.maximum(m, jnp.max(st, axis=0, keepdims=True))
        p = jnp.exp2(st - m_new)
    else:
        m_raw = jnp.maximum(m - shift, jnp.max(st, axis=0, keepdims=True))
        p = jnp.exp2(st - m_raw)
        m_new = m_raw + shift
    return m_new, jnp.exp2(m - m_new), p.astype(BF16)


def _with_ones(vt):
    return jnp.concatenate([vt, jnp.ones((ONES_ROWS, vt.shape[1]), vt.dtype)], axis=0)


def _attn_out(ot, g_ref, out_scale):
    return ot * lax.rsqrt(jnp.mean(ot * ot, -1, keepdims=True) + RMS_EPS) * g_ref[...] * out_scale


def _attn_prompt_kernel(lam_ref, fb_ref, q_ref, k_ref, v_ref, seg_ref, g_ref, o_ref,
                        kb_scr, vt_scr, qst_scr, acc_scr, *, d_far, out_scale):
    hg = pl.program_id(1)
    qi = pl.program_id(2)
    ng, n, t, dv = kb_scr.shape
    w2 = 2 * t

    @pl.when(qi == 0)
    def _():
        for u in range(ng):
            for c in range(n):
                kb_scr[u, c] = k_ref[0, c * t:(c + 1) * t, u * dv:(u + 1) * dv].astype(BF16)
                vt_scr[u, c] = _with_ones(v_ref[0, c * t:(c + 1) * t, u * dv:(u + 1) * dv].T.astype(BF16))

    for u in range(ng):
        qst_scr[u] = _stack_q(q_ref[0, :, u * dv:(u + 1) * dv])
    acc_scr[...] = jnp.zeros(acc_scr.shape, F32)
    far_shift = jnp.concatenate([jnp.full((1, w2), fb_ref[hg * ng + u], F32) for u in range(ng)], axis=1)

    def scores(ki, u):
        return jnp.dot(kb_scr[u, ki], qst_scr[u], preferred_element_type=F32)

    def biased(st, seg, vis=None):
        bt = _bias_tile_t(seg, t, 0)[:, :t]
        s1, s2 = st[:, :t] + bt, st[:, t:] + bt
        if vis is not None:
            s1, s2 = jnp.where(vis, s1, -1e30), jnp.where(vis, s2, -1e30)
        return jnp.concatenate([s1, s2], axis=1)

    def update(ki, m, sts, shift=None):
        m, alpha, p = _online(jnp.concatenate(sts, axis=1), m, shift)
        for u in range(ng):
            sl = slice(u * w2, (u + 1) * w2)
            acc_scr[u] = alpha[:, sl] * acc_scr[u] + jnp.dot(vt_scr[u, ki], p[:, sl],
                                                             preferred_element_type=F32)
        return m

    def far(ki, m):
        return update(ki, m, [scores(ki, u) for u in range(ng)], far_shift)

    def near(ki, m):
        return update(ki, m, [biased(scores(ki, u), seg_ref[u, qi - ki]) for u in range(ng)])

    n_far = jnp.maximum(qi - (d_far - 1), 0)
    m = lax.fori_loop(0, n_far, far, jnp.full((1, ng * w2), -1e30, F32))
    m = lax.fori_loop(n_far, qi, near, m)
    vis = (lax.broadcasted_iota(jnp.int32, (t, t), 0) // CHUNK
           <= lax.broadcasted_iota(jnp.int32, (t, t), 1) // CHUNK)
    update(qi, m, [biased(scores(qi, u), seg_ref[u, 0], vis) for u in range(ng)])
    for u in range(ng):
        acc = acc_scr[u]
        on = acc[:dv] * (1.0 / acc[dv:dv + 1])
        ot = (on[:, :t] - lam_ref[0] * on[:, t:]).T
        o_ref[0, :, u * dv:(u + 1) * dv] = _attn_out(ot, g_ref, out_scale).astype(o_ref.dtype)


def _attn_prompt(qkv, rel_bias, lam, subln_g, lam_init):
    bsz, s, _ = qkv.shape
    h, dv, t, g = H_B, 2 * DH_B, ATT_T, ATT_G
    n = s // t
    w = 2 * t
    d_far = -(-(_saturation_distance() - 1) // t) + 1
    segs = _bias_rows(rel_bias, [t - d * t for d in range(n)], w)
    kv = lambda sec: pl.BlockSpec((1, s, g * dv), lambda b, hg, qi, sec=sec: (b, 0, sec * (h // g) + hg))
    return pl.pallas_call(
        functools.partial(_attn_prompt_kernel, d_far=d_far, out_scale=1.0 - lam_init),
        grid=(bsz, h // g, n),
        in_specs=[pl.BlockSpec(memory_space=pltpu.SMEM), pl.BlockSpec(memory_space=pltpu.SMEM),
                  pl.BlockSpec((1, t, g * dv), lambda b, hg, qi: (b, qi, hg)),
                  kv(1), kv(2),
                  pl.BlockSpec((g, n, 1, w), lambda b, hg, qi: (hg, 0, 0, 0)),
                  pl.BlockSpec((1, dv), lambda b, hg, qi: (0, 0))],
        out_specs=pl.BlockSpec((1, t, g * dv), lambda b, hg, qi: (b, qi, hg)),
        out_shape=jax.ShapeDtypeStruct((bsz, s, h * dv), BF16),
        scratch_shapes=[pltpu.VMEM((g, n, t, dv), BF16), pltpu.VMEM((g, n, dv + ONES_ROWS, t), BF16),
                        pltpu.VMEM((g, dv, 2 * t), BF16), pltpu.VMEM((g, dv + ONES_ROWS, 2 * t), F32)],
        compiler_params=_cparams(("arbitrary",) * 3),
        name="diff_attn_prompt",
    )(lam, _far_bias(rel_bias), qkv, qkv, qkv, segs, subln_g.reshape(1, dv))


def _attn_sample_kernel(lam_ref, fb_ref, q_ref, kp_ref, vp_ref, segp_ref, kn_ref, vn_ref, segn_ref,
                        g_ref, o_ref, *, tkp, n_past, far_tiles, out_scale):
    hg = pl.program_id(1)
    tq = q_ref.shape[1]
    dv = g_ref.shape[1]
    ng = q_ref.shape[2] // dv
    nh = kp_ref.shape[1] // (n_past * tkp)
    w2 = 2 * tq
    qst = [_stack_q(q_ref[0, :, u * dv:(u + 1) * dv]) for u in range(ng)]
    lane = lax.broadcasted_iota(jnp.int32, (1, w2), 1)

    def bias(seg, tk):
        return jnp.where(lane < tq, _bias_tile_t(seg, tk, 0)[:, :w2], _bias_tile_t(seg, tk, tq)[:, :w2])

    m = jnp.full((1, ng * w2), -1e30, F32)
    far_shift = jnp.concatenate([jnp.full((1, w2), fb_ref[hg * ng + u], F32) for u in range(ng)], axis=1)
    acc = [jnp.zeros((dv + ONES_ROWS, w2), F32) for _ in range(ng)]
    for c in range(n_past + 1):
        sts, vts = [], []
        for u in range(ng):
            if c < n_past:
                rows = pl.ds(c * tkp * nh + hg * ng + u, tkp, stride=nh)
                k, v = kp_ref[0, rows, :], vp_ref[0, rows, :]
            else:
                k, v = kn_ref[0, :, u * dv:(u + 1) * dv], vn_ref[0, :, u * dv:(u + 1) * dv]
            st = jnp.dot(k.astype(BF16), qst[u], preferred_element_type=F32)
            if far_tiles <= c < n_past:
                st = st + bias(segp_ref[u, c - far_tiles], tkp)
            elif c == n_past:
                st = st + bias(segn_ref[u, 0], tq)
            sts.append(st)
            vts.append(_with_ones(v.T.astype(BF16)))
        m, alpha, p = _online(jnp.concatenate(sts, axis=1), m, far_shift if c < far_tiles else None)
        for u in range(ng):
            sl = slice(u * w2, (u + 1) * w2)
            acc[u] = alpha[:, sl] * acc[u] + jnp.dot(vts[u], p[:, sl], preferred_element_type=F32)
    for u in range(ng):
        on = acc[u][:dv] * (1.0 / acc[u][dv:dv + 1])
        ot = (on - lam_ref[0] * pltpu.roll(on, tq, 1)).T[:tq]
        o_ref[0, :, u * dv:(u + 1) * dv] = _attn_out(ot, g_ref, out_scale).astype(o_ref.dtype)


def _attn_sample(qkv, k_past, v_past, layer, rel_bias, lam, subln_g, lam_init):
    bsz, s, _ = qkv.shape
    h, dv, tkp, g = H_B, 2 * DH_B, ATT_T, ATT_G
    past = k_past.shape[1] // h
    n_past = past // tkp
    far_tiles = sum(1 for c in range(n_past) if past - (c * tkp + tkp - 1) >= _saturation_distance())
    near = list(range(far_tiles, n_past)) or [n_past - 1]
    segp = _bias_rows(rel_bias, [c * tkp - past + tkp for c in near], 2 * s + tkp)
    segn = _bias_rows(rel_bias, [s], 4 * s)
    pk = lambda: pl.BlockSpec((1, past * h, dv), lambda b, hg: (layer * bsz + b, 0, 0))
    nk = lambda sec: pl.BlockSpec((1, s, g * dv), lambda b, hg, sec=sec: (b, 0, sec * (h // g) + hg))
    return pl.pallas_call(
        functools.partial(_attn_sample_kernel, tkp=tkp, n_past=n_past, far_tiles=far_tiles,
                          out_scale=1.0 - lam_init),
        grid=(bsz, h // g),
        in_specs=[pl.BlockSpec(memory_space=pltpu.SMEM), pl.BlockSpec(memory_space=pltpu.SMEM),
                  nk(0), pk(), pk(),
                  pl.BlockSpec((g, len(near), 1, 2 * s + tkp), lambda b, hg: (hg, 0, 0, 0)),
                  nk(1), nk(2),
                  pl.BlockSpec((g, 1, 1, 4 * s), lambda b, hg: (hg, 0, 0, 0)),
                  pl.BlockSpec((1, dv), lambda b, hg: (0, 0))],
        out_specs=pl.BlockSpec((1, s, g * dv), lambda b, hg: (b, 0, hg)),
        out_shape=jax.ShapeDtypeStruct((bsz, s, h * dv), BF16),
        compiler_params=_cparams(("arbitrary",) * 2),
        name="diff_attn_sample",
    )(lam, _far_bias(rel_bias), qkv, k_past, v_past, segp, qkv, qkv, segn, subln_g.reshape(1, dv))


def _rglru_kernel(xw_ref, cw_ref, cb_ref, wa_ref, ba_ref, wx_ref, bx_ref, lam_ref, cbuf_ref, h0_ref,
                  y_ref, hlast_ref, ext_scr, h_scr):
    t = pl.program_id(1)
    tt = y_ref.shape[1]
    d = y_ref.shape[2]
    pad = ext_scr.shape[0] - tt

    @pl.when(t == 0)
    def _():
        ext_scr[0:pad] = cbuf_ref[0]
        h_scr[...] = h0_ref[0]

    gate = xw_ref[0, :, 0:d]
    u = xw_ref[0, :, d:2 * d]
    ext_scr[pad:pad + tt] = u
    conv = cb_ref[...] + cw_ref[CONV_W - 1:CONV_W] * u
    for j in range(1, CONV_W):
        conv = conv + cw_ref[CONV_W - 1 - j:CONV_W - j] * ext_scr[pad - j:pad - j + tt]
    ext_scr[0:pad] = ext_scr[tt:tt + pad]

    cb16 = conv.astype(BF16)
    blk = d // N_BLK_C
    ra, rx = [], []
    for n in range(N_BLK_C):
        cs = cb16[:, n * blk:(n + 1) * blk]
        ra.append(jnp.dot(cs, wa_ref[n], preferred_element_type=F32))
        rx.append(jnp.dot(cs, wx_ref[n], preferred_element_type=F32))
    r = _sigmoid(jnp.concatenate(ra, axis=1) + ba_ref[...])
    ig = _sigmoid(jnp.concatenate(rx, axis=1) + bx_ref[...])
    nl = -lam_ref[...]
    sp = jnp.maximum(nl, 0.0) + jnp.log(1.0 + jnp.exp(-jnp.abs(nl)))
    log_a = (-C_RG) * r * sp
    a = jnp.exp(log_a)
    xin = jnp.sqrt(-jnp.tanh(log_a) * (1.0 + a * a)) * ig * conv

    row = lax.broadcasted_iota(jnp.int32, (tt, d), 0)
    ca, cx = a, xin
    sft = 1
    while sft < tt:
        keep = row >= sft
        pa = jnp.where(keep, pltpu.roll(ca, sft, 0), 1.0)
        px = jnp.where(keep, pltpu.roll(cx, sft, 0), 0.0)
        cx = ca * px + cx
        ca = ca * pa
        sft *= 2
    hh = ca * h_scr[...] + cx
    h_scr[...] = hh[tt - 1:tt]
    gelu = 0.5 * gate * (1.0 + jnp.tanh(math.sqrt(2.0 / math.pi) * (gate + 0.044715 * gate * gate * gate)))
    y_ref[0] = (hh * gelu).astype(y_ref.dtype)

    @pl.when(t == pl.num_programs(1) - 1)
    def _():
        hlast_ref[0] = hh[tt - 1:tt]


def _rglru(xw, cbuf8, h0, conv_w, conv_b, w_a, b_a, w_x, b_x, lam, *, rows):
    bsz, s, d2 = xw.shape
    d = d2 // 2
    pad = cbuf8.shape[1]
    blk = d // N_BLK_C
    row = lambda a: a.reshape(1, d).astype(F32)
    full2 = lambda shp: pl.BlockSpec(shp, lambda b, t: (0,) * len(shp))
    return pl.pallas_call(
        _rglru_kernel,
        grid=(bsz, s // rows),
        in_specs=[pl.BlockSpec((1, rows, d2), lambda b, t: (b, t, 0)),
                  full2((CONV_W, d)), full2((1, d)),
                  full2((N_BLK_C, blk, blk)), full2((1, d)),
                  full2((N_BLK_C, blk, blk)), full2((1, d)),
                  full2((1, d)),
                  pl.BlockSpec((1, pad, d), lambda b, t: (b, 0, 0)),
                  pl.BlockSpec((1, 1, d), lambda b, t: (b, 0, 0))],
        out_specs=[pl.BlockSpec((1, rows, d), lambda b, t: (b, t, 0)),
                   pl.BlockSpec((1, 1, d), lambda b, t: (b, 0, 0))],
        out_shape=[jax.ShapeDtypeStruct((bsz, s, d), BF16),
                   jax.ShapeDtypeStruct((bsz, 1, d), F32)],
        scratch_shapes=[pltpu.VMEM((rows + pad, d), F32), pltpu.VMEM((1, d), F32)],
        compiler_params=_cparams(("arbitrary", "arbitrary")),
        name="rglru",
    )(xw, conv_w.astype(F32), row(conv_b), w_a.astype(BF16), row(b_a), w_x.astype(BF16), row(b_x),
      row(lam), cbuf8, h0)


def _route(x, wr_t, rbias):
    ne = wr_t.shape[0]
    tm = x.shape[0]
    gsz = ne // N_GROUPS
    logits = lax.dot_general(wr_t, x, _NT, preferred_element_type=F32)
    score = _sigmoid(logits)
    biased = score + rbias
    neg = -jnp.inf
    sub = lax.broadcasted_iota(jnp.int32, (gsz, tm), 0)
    gscore = []
    for g in range(N_GROUPS):
        vg = biased[g * gsz:(g + 1) * gsz]
        m1 = jnp.max(vg, axis=0, keepdims=True)
        first = jnp.min(jnp.where(vg == m1, sub, gsz), axis=0, keepdims=True)
        m2 = jnp.max(jnp.where(sub == first, neg, vg), axis=0, keepdims=True)
        gscore.append(m1 + m2)
    masked = []
    for g in range(N_GROUPS):
        rank = jnp.zeros((1, tm), jnp.int32)
        for o in range(N_GROUPS):
            if o == g:
                continue
            ahead = (gscore[o] >= gscore[g]) if o < g else (gscore[o] > gscore[g])
            rank = rank + ahead.astype(jnp.int32)
        keep = rank < TOPK_GROUPS
        masked.append(jnp.where(keep, biased[g * gsz:(g + 1) * gsz], neg))
    cur = jnp.concatenate(masked, axis=0)
    eidx = lax.broadcasted_iota(jnp.int32, (ne, tm), 0)
    chosen = jnp.zeros((ne, tm), F32)
    for _ in range(TOP_K):
        m = jnp.max(cur, axis=0, keepdims=True)
        first = jnp.min(jnp.where(cur == m, eidx, ne), axis=0, keepdims=True)
        hit = eidx == first
        chosen = jnp.where(hit, 1.0, chosen)
        cur = jnp.where(hit, neg, cur)
    wsel = chosen * score
    return wsel / jnp.sum(wsel, axis=0, keepdims=True) * ROUTED_SCALE


def _swiglu(h, f):
    hg = h[:, :f]
    return hg * _sigmoid(hg) * h[:, f:]


def _moe_kernel(inv_ref, x_ref, wr_ref, rb_ref, wsi_ref, wsd_ref, wi_ref, wd_ref, g_ref, b_ref, out_ref,
                xb_scr, x8_scr, acc_scr, gate_scr, *, eb):
    j = pl.program_id(1)
    f = wsd_ref.shape[0]

    @pl.when(j == 0)
    def _():
        x = x_ref[...]
        xb = x.astype(BF16)
        xb_scr[...] = xb
        x8_scr[...] = x.astype(F8)
        w_et = _route(xb, wr_ref[...], rb_ref[...])
        ne, tm = w_et.shape
        gate_scr[...] = jnp.concatenate([w_et, jnp.zeros((gate_scr.shape[1] - ne, tm), F32)], axis=0).T
        sh = _swiglu(jnp.dot(xb, wsi_ref[...], preferred_element_type=F32), f)
        acc_scr[...] = jnp.dot(sh.astype(BF16), wsd_ref[...], preferred_element_type=F32)

    x8 = x8_scr[...]
    gates = gate_scr[...]
    lane = lax.broadcasted_iota(jnp.int32, gates.shape, 1)
    acts = []
    for i in range(eb):
        e = j * eb + i
        h = jnp.dot(x8, wi_ref[i], preferred_element_type=F32) * inv_ref[e]
        gcol = jnp.sum(jnp.where(lane == e, gates, 0.0), axis=1, keepdims=True)
        acts.append((_swiglu(h, f) * gcol).astype(BF16))
    acc_scr[...] += jnp.dot(jnp.concatenate(acts, axis=1), wd_ref[0], preferred_element_type=F32)

    @pl.when(j == pl.num_programs(1) - 1)
    def _():
        out_ref[...] = _layer_norm(ALPHA * x_ref[...] + acc_scr[...], g_ref[...], b_ref[...])


def _fp8_weights(w):
    amax = jnp.max(jnp.abs(w.astype(F32)), axis=tuple(range(1, w.ndim)))
    k = jnp.floor(jnp.log2(F8_MAX / jnp.maximum(amax, 1e-30))) - 1.0
    scale = jnp.exp2(jnp.clip(k, -60.0, 60.0))
    return (w.astype(F32) * scale.reshape((-1,) + (1,) * (w.ndim - 1))).astype(F8), 1.0 / scale


def _moe_ln(x, w_router, router_bias, w_in, w_down, ws_in, ws_down, g, b, *, tm):
    t, d = x.shape
    ne, _, f2 = w_in.shape
    f = f2 // 2
    eb = MOE_EB
    wd = w_down.astype(BF16).reshape(ne // eb, eb * f, d)
    w8, inv_scale = _fp8_weights(w_in)
    const = lambda shp: pl.BlockSpec(shp, lambda i, j: (0,) * len(shp))
    return pl.pallas_call(
        functools.partial(_moe_kernel, eb=eb),
        grid=(t // tm, ne // eb),
        in_specs=[pl.BlockSpec(memory_space=pltpu.SMEM),
                  pl.BlockSpec((tm, d), lambda i, j: (i, 0)),
                  const((ne, d)), const((ne, 1)),
                  const((d, f2)), const((f, d)),
                  pl.BlockSpec((eb, d, f2), lambda i, j: (j, 0, 0)),
                  pl.BlockSpec((1, eb * f, d), lambda i, j: (j, 0, 0)),
                  const((1, d)), const((1, d))],
        out_specs=pl.BlockSpec((tm, d), lambda i, j: (i, 0)),
        out_shape=jax.ShapeDtypeStruct((t, d), F32),
        scratch_shapes=[pltpu.VMEM((tm, d), BF16), pltpu.VMEM((tm, d), F8), pltpu.VMEM((tm, d), F32),
                        pltpu.VMEM((tm, 128), F32)],
        compiler_params=_cparams(("arbitrary", "arbitrary")),
        name="moe_ln",
    )(inv_scale, x, w_router.T.astype(BF16), router_bias.reshape(ne, 1).astype(F32),
      ws_in.astype(BF16), ws_down.astype(BF16), w8, wd,
      g.reshape(1, d), b.reshape(1, d))


def _lambda_init(layer_idx):
    return 0.8 - 0.6 * math.exp(-0.3 * layer_idx)


def kernel(x_prompt, x_sample, state_hgrn_s, cache_k, cache_v, state_rglru_conv, state_rglru_h, hgrn_lb_logits, hgrn_w_in, hgrn_gn_g, hgrn_w_out, rel_bias, attn_w_qkv, attn_lam_q1, attn_lam_k1, attn_lam_q2, attn_lam_k2, attn_subln_g, attn_w_out, rglru_w_in, rglru_conv_w, rglru_conv_b, rglru_w_a, rglru_b_a, rglru_w_x, rglru_b_x, rglru_lam, rglru_w_out, ln1_g, ln1_b, ln2_g, ln2_b, moe_w_router, moe_router_bias, moe_w_in, moe_w_down, moe_ws_in, moe_ws_down):
    bp, sp, d = x_prompt.shape
    bs, ss, _ = x_sample.shape
    lb_p = jax.nn.softmax(hgrn_lb_logits.astype(F32), axis=0)
    lower_bounds = jnp.clip(jnp.cumsum(lb_p, axis=0) - lb_p[0], 0.0, 1.0)

    streams = [(x_prompt.reshape(bp * sp, d), bp, sp), (x_sample.reshape(bs * ss, d), bs, ss)]
    tile = lambda t: 1024 if t % 1024 == 0 else 512
    hs, kvs, cvs, hls = [[], []], [[], []], [[], []], [[], []]
    for i in range(DEPTH):
        j, kind = i // N_MIXERS, i % N_MIXERS
        new_streams = []
        for si, (x, bsz, s) in enumerate(streams):
            t = bsz * s
            tm = tile(t)
            if kind == 0:
                xw = _matmul(x, hgrn_w_in[j].astype(BF16), tm=tm, tn=1024)
                s0 = (jnp.zeros((bsz, H_A, d // H_A, d // H_A), F32) if si == 0
                      else state_hgrn_s[j].astype(F32))
                o, s_fin = _hgrn(xw.reshape(bsz, s, -1), s0, lower_bounds[i], hgrn_gn_g[j],
                                 rows=min(s, 256))
                hs[si].append(s_fin)
                w_out = hgrn_w_out[j]
            elif kind == 1:
                qkv = _matmul(x, attn_w_qkv[j].astype(BF16), tm=tm, tn=1024)
                hd = H_B * 2 * DH_B
                lam_init = _lambda_init(i)
                lam = (jnp.exp(jnp.sum(attn_lam_q1[j].astype(F32) * attn_lam_k1[j].astype(F32)))
                       - jnp.exp(jnp.sum(attn_lam_q2[j].astype(F32) * attn_lam_k2[j].astype(F32)))
                       + lam_init).reshape(1)
                qkv3 = qkv.reshape(bsz, s, 3 * hd)
                if si == 0:
                    o = _attn_prompt(qkv3, rel_bias, lam, attn_subln_g[j], lam_init)
                else:
                    past = cache_k.shape[2]
                    o = _attn_sample(qkv3, cache_k.reshape(-1, past * H_B, 2 * DH_B),
                                     cache_v.reshape(-1, past * H_B, 2 * DH_B),
                                     j, rel_bias, lam, attn_subln_g[j], lam_init)
                kvs[si].append((qkv3[:, :, hd:2 * hd].reshape(bsz, s, H_B, 2 * DH_B),
                                qkv3[:, :, 2 * hd:].reshape(bsz, s, H_B, 2 * DH_B)))
                w_out = attn_w_out[j]
            else:
                xw = _matmul(x, rglru_w_in[j].astype(BF16), tm=tm, tn=1024).reshape(bsz, s, 2 * d)
                if si == 0:
                    cbuf = jnp.zeros((bsz, CONV_W - 1, d), F32)
                    h0 = jnp.zeros((bsz, d), F32)
                else:
                    cbuf, h0 = state_rglru_conv[j].astype(F32), state_rglru_h[j].astype(F32)
                cbuf8 = jnp.pad(cbuf, ((0, 0), (8 - (CONV_W - 1), 0), (0, 0)))
                o, hlast = _rglru(xw, cbuf8, h0.reshape(bsz, 1, d), rglru_conv_w[j], rglru_conv_b[j],
                                  rglru_w_a[j], rglru_b_a[j], rglru_w_x[j], rglru_b_x[j], rglru_lam[j],
                                  rows=min(s, 256))
                cvs[si].append(xw[:, s - (CONV_W - 1):, d:])
                hls[si].append(hlast.reshape(bsz, d))
                w_out = rglru_w_out[j]
            x = _proj_ln(o.reshape(t, d), w_out.astype(BF16), x, ln1_g[i], ln1_b[i], tm=tm)
            x = _moe_ln(x, moe_w_router[i], moe_router_bias[i], moe_w_in[i], moe_w_down[i],
                        moe_ws_in[i], moe_ws_down[i], ln2_g[i], ln2_b[i], tm=tm)
            new_streams.append((x, bsz, s))
        streams = new_streams

    yp = streams[0][0].reshape(bp, sp, d)
    ys = streams[1][0].reshape(bs, ss, d)
    st = lambda xs: jnp.stack(xs)
    return (yp, ys, st(hs[0]), st(hs[1]),
            st([kv[0] for kv in kvs[0]]), st([kv[1] for kv in kvs[0]]),
            st([kv[0] for kv in kvs[1]]), st([kv[1] for kv in kvs[1]]),
            st(cvs[0]), st(hls[0]), st(cvs[1]), st(hls[1]))
```

```python
import functools
import math

import jax
import jax.numpy as jnp
from jax import lax
from jax.experimental import pallas as pl
from jax.experimental.pallas import tpu as pltpu

F32 = jnp.float32
BF16 = jnp.bfloat16
F8 = jnp.float8_e4m3fn
F8_MAX = 448.0
ACT_SCALE = 32.0

DEPTH = 4
CHUNK = 64
N_MIXERS = 3
H_A = 8
H_B = 8
DH_B = 64
REL_BUCKETS = 32
REL_MAX_DIST = 1024
N_BLK_C = 4
CONV_W = 4
C_RG = 8.0
N_EXPERTS = 64
D_EXPERT = 128
TOP_K = 8
N_GROUPS = 8
TOPK_GROUPS = 4
ROUTED_SCALE = 2.5
ALPHA = (2 * DEPTH) ** 0.25
LN_EPS = 1e-5
RMS_EPS = 1e-6

VMEM_LIMIT = 56 * 1024 * 1024
LOG2E = math.log2(math.e)
REC_C = 64
ATT_T = 256
ATT_G = 4
ONES_ROWS = 16
MOE_EB = 8

_NT = (((1,), (1,)), ((), ()))
_TN = (((0,), (0,)), ((), ()))


def _cparams(sem):
    return pltpu.CompilerParams(dimension_semantics=sem, vmem_limit_bytes=VMEM_LIMIT)


def _sigmoid(x):
    return 1.0 / (1.0 + jnp.exp(-x))


def _layer_norm(z, g, b):
    mu = jnp.mean(z, -1, keepdims=True)
    zc = z - mu
    var = jnp.mean(zc * zc, -1, keepdims=True)
    return zc * lax.rsqrt(var + LN_EPS) * g + b


def _mm_kernel(x_ref, w_ref, o_ref):
    o_ref[...] = jnp.dot(x_ref[...].astype(BF16), w_ref[...],
                         preferred_element_type=F32).astype(o_ref.dtype)


def _matmul(x, w, *, tm, out_dtype=F32):
    t, k = x.shape
    n = w.shape[1]
    return pl.pallas_call(
        _mm_kernel,
        grid=(t // tm,),
        in_specs=[pl.BlockSpec((tm, k), lambda i: (i, 0)),
                  pl.BlockSpec((k, n), lambda i: (0, 0))],
        out_specs=pl.BlockSpec((tm, n), lambda i: (i, 0)),
        out_shape=jax.ShapeDtypeStruct((t, n), out_dtype),
        compiler_params=_cparams(("arbitrary",)),
        name="proj_in",
    )(x, w)


def _proj_ln_kernel(o_ref, w_ref, x_ref, g_ref, b_ref, out_ref):
    y = jnp.dot(o_ref[...], w_ref[...], preferred_element_type=F32)
    out_ref[...] = _layer_norm(ALPHA * x_ref[...] + y, g_ref[...], b_ref[...])


def _proj_ln(o, w, x, g, b, *, tm):
    t, d = x.shape
    k = o.shape[1]
    return pl.pallas_call(
        _proj_ln_kernel,
        grid=(t // tm,),
        in_specs=[pl.BlockSpec((tm, k), lambda i: (i, 0)),
                  pl.BlockSpec((k, d), lambda i: (0, 0)),
                  pl.BlockSpec((tm, d), lambda i: (i, 0)),
                  pl.BlockSpec((1, d), lambda i: (0, 0)),
                  pl.BlockSpec((1, d), lambda i: (0, 0))],
        out_specs=pl.BlockSpec((tm, d), lambda i: (i, 0)),
        out_shape=jax.ShapeDtypeStruct((t, d), F32),
        compiler_params=_cparams(("arbitrary",)),
        name="proj_out_ln",
    )(o, w, x, g.reshape(1, d), b.reshape(1, d))


def _hgrn_kernel(q_ref, fz_ref, v_ref, g_ref, lb_ref, gn_ref, s0_ref, o_ref, sfin_ref, st_scr,
                 *, chunk):
    c = pl.program_id(1)
    nh, dv, dk = st_scr.shape

    @pl.when(c == 0)
    def _():
        for h in range(nh):
            st_scr[h] = s0_ref[0, h].T

    rows = fz_ref.shape[1]
    lb = lb_ref[...]
    causal = (lax.broadcasted_iota(jnp.int32, (chunk, chunk), 0)
              >= lax.broadcasted_iota(jnp.int32, (chunk, chunk), 1))
    tri = jnp.where(causal, 1.0, 0.0).astype(BF16)
    sts = [st_scr[h] for h in range(nh)]
    for i in range(rows // chunk):
        sl = slice(i * chunk, (i + 1) * chunk)
        fz = fz_ref[0, sl, :]
        e = jnp.exp(-jnp.abs(fz))
        r = 1.0 / (1.0 + e)
        pos = fz >= 0
        sig = jnp.where(pos, r, e * r)
        nsig = jnp.where(pos, e * r, r)
        logf = jnp.log(lb + (1.0 - lb) * sig)
        kk = (1.0 - lb) * nsig
        hi = logf.astype(BF16)
        r1 = logf - hi.astype(F32)
        mid = r1.astype(BF16)
        lo = (r1 - mid.astype(F32)).astype(BF16)
        b = (jnp.dot(tri, hi, preferred_element_type=F32)
             + jnp.dot(tri, mid, preferred_element_type=F32)
             + jnp.dot(tri, lo, preferred_element_type=F32))
        bm = b[chunk // 2 - 1:chunk // 2]
        be = b[chunk - 1:chunk]
        qv = q_ref[0, sl, :]
        a_fac = qv * _sigmoid(qv) * jnp.exp(b - bm)
        k_fac = kk * jnp.exp(bm - b)
        qe = (a_fac * jnp.exp(bm)).astype(BF16)
        kd = (k_fac * jnp.exp(be - bm)).astype(BF16)
        a16 = a_fac.astype(BF16)
        k16 = k_fac.astype(BF16)
        dec = jnp.exp(be)
        v = v_ref[0, sl, :].astype(BF16)
        gv = g_ref[0, sl, :]
        gate = gv * _sigmoid(gv)
        for h in range(nh):
            hs = slice(h * dk, (h + 1) * dk)
            st = sts[h]
            o = lax.dot_general(qe[:, hs], st.astype(BF16), _NT, preferred_element_type=F32)
            sc = lax.dot_general(a16[:, hs], k16[:, hs], _NT, preferred_element_type=F32)
            sc = jnp.where(causal, sc, 0.0)
            o = o + jnp.dot(sc.astype(BF16), v[:, hs], preferred_element_type=F32)
            sts[h] = st * dec[:, hs] + lax.dot_general(v[:, hs], kd[:, hs], _TN,
                                                       preferred_element_type=F32)
            o = o * lax.rsqrt(jnp.mean(o * o, -1, keepdims=True) + RMS_EPS) * gn_ref[...] * gate[:, hs]
            o_ref[0, sl, hs] = o.astype(o_ref.dtype)
    for h in range(nh):
        st_scr[h] = sts[h]

    @pl.when(c == pl.num_programs(1) - 1)
    def _():
        for h in range(nh):
            sfin_ref[0, h] = st_scr[h].T


def _hgrn(xw, s0, lb, gn_g, *, rows):
    bsz, s, _ = xw.shape
    h, dk, dv = s0.shape[1:]
    sec = lambda k: pl.BlockSpec((1, rows, h * dk), lambda b, c, k=k: (b, c, k))
    return pl.pallas_call(
        functools.partial(_hgrn_kernel, chunk=REC_C),
        grid=(bsz, s // rows),
        in_specs=[sec(0), sec(1), sec(2), sec(3),
                  pl.BlockSpec((1, h * dk), lambda b, c: (0, 0)),
                  pl.BlockSpec((1, dv), lambda b, c: (0, 0)),
                  pl.BlockSpec((1, h, dk, dv), lambda b, c: (b, 0, 0, 0))],
        out_specs=[pl.BlockSpec((1, rows, h * dv), lambda b, c: (b, c, 0)),
                   pl.BlockSpec((1, h, dk, dv), lambda b, c: (b, 0, 0, 0))],
        out_shape=[jax.ShapeDtypeStruct((bsz, s, h * dv), BF16),
                   jax.ShapeDtypeStruct((bsz, h, dk, dv), F32)],
        scratch_shapes=[pltpu.VMEM((h, dv, dk), F32)],
        compiler_params=_cparams(("arbitrary", "arbitrary")),
        name="hgrn2",
    )(xw, xw, xw, xw, lb.reshape(1, h * dk), gn_g.reshape(1, dv), s0)


def _rel_bucket(rel):
    nb = REL_BUCKETS // 2
    max_exact = nb // 2
    ret = jnp.where(rel > 0, nb, 0)
    n = jnp.abs(rel)
    nf = jnp.maximum(n, 1).astype(F32)
    large = max_exact + (jnp.log(nf / max_exact) / math.log(REL_MAX_DIST / max_exact)
                         * (nb - max_exact)).astype(jnp.int32)
    large = jnp.minimum(large, nb - 1)
    return ret + jnp.where(n < max_exact, n, large)


def _saturation_distance():
    nb = REL_BUCKETS // 2
    max_exact = nb // 2
    return math.ceil(max_exact * (REL_MAX_DIST / max_exact) ** ((nb - 1 - max_exact) / (nb - max_exact))) + 1


def _bias_rows(rel_bias, rel0s, width):
    rel = jnp.asarray(rel0s, jnp.int32)[:, None] - jnp.arange(width, dtype=jnp.int32)[None, :]
    seg = rel_bias.astype(F32)[_rel_bucket(rel)] * LOG2E
    return jnp.transpose(seg, (2, 0, 1))[:, :, None, :]


def _far_bias(rel_bias):
    return rel_bias.astype(F32)[_rel_bucket(jnp.int32(-_saturation_distance()))] * LOG2E


def _bias_tile_t(seg, tk, shift):
    w = seg.shape[1]
    return pltpu.roll(jnp.broadcast_to(seg, (tk, w)), w - tk + shift, 1, stride=1, stride_axis=0)


def _stack_q(q):
    q = q * (DH_B ** -0.5 * LOG2E)
    lane = lax.broadcasted_iota(jnp.int32, q.shape, 1)
    qs = jnp.concatenate([jnp.where(lane < DH_B, q, 0.0), jnp.where(lane >= DH_B, q, 0.0)], axis=0)
    return qs.T.astype(BF16)


def _online(st, m, shift=None):
    if shift is None:
        m_new = jnp.maximum(m, jnp.max(st, axis=0, keepdims=True))
        p = jnp.exp2(st - m_new)
    else:
        m_raw = jnp.maximum(m - shift, jnp.max(st, axis=0, keepdims=True))
        p = jnp.exp2(st - m_raw)
        m_new = m_raw + shift
    return m_new, jnp.exp2(m - m_new), p.astype(BF16)


def _with_ones(vt):
    return jnp.concatenate([vt, jnp.ones((ONES_ROWS, vt.shape[1]), vt.dtype)], axis=0)


def _attn_out(ot, g_ref, out_scale):
    return ot * lax.rsqrt(jnp.mean(ot * ot, -1, keepdims=True) + RMS_EPS) * g_ref[...] * out_scale


def _attn_prompt_kernel(lam_ref, fb_ref, q_ref, k_ref, v_ref, seg_ref, g_ref, o_ref,
                        kb_scr, vt_scr, qst_scr, acc_scr, *, d_far, out_scale):
    hg = pl.program_id(1)
    qi = pl.program_id(2)
    ng, n, t, dv = kb_scr.shape
    w2 = 2 * t

    @pl.when(qi == 0)
    def _():
        for u in range(ng):
            for c in range(n):
                kb_scr[u, c] = k_ref[0, c * t:(c + 1) * t, u * dv:(u + 1) * dv].astype(BF16)
                vt_scr[u, c] = _with_ones(v_ref[0, c * t:(c + 1) * t, u * dv:(u + 1) * dv].T.astype(BF16))

    for u in range(ng):
        qst_scr[u] = _stack_q(q_ref[0, :, u * dv:(u + 1) * dv])
    acc_scr[...] = jnp.zeros(acc_scr.shape, F32)
    far_shift = jnp.concatenate([jnp.full((1, w2), fb_ref[hg * ng + u], F32) for u in range(ng)], axis=1)

    def scores(ki, u):
        return jnp.dot(kb_scr[u, ki], qst_scr[u], preferred_element_type=F32)

    def biased(st, seg, vis=None):
        bt = _bias_tile_t(seg, t, 0)[:, :t]
        s1, s2 = st[:, :t] + bt, st[:, t:] + bt
        if vis is not None:
            s1, s2 = jnp.where(vis, s1, -1e30), jnp.where(vis, s2, -1e30)
        return jnp.concatenate([s1, s2], axis=1)

    def update(ki, m, sts, shift=None):
        m, alpha, p = _online(jnp.concatenate(sts, axis=1), m, shift)
        for u in range(ng):
            sl = slice(u * w2, (u + 1) * w2)
            acc_scr[u] = alpha[:, sl] * acc_scr[u] + jnp.dot(vt_scr[u, ki], p[:, sl],
                                                             preferred_element_type=F32)
        return m

    def far(ki, m):
        return update(ki, m, [scores(ki, u) for u in range(ng)], far_shift)

    def near(ki, m):
        return update(ki, m, [biased(scores(ki, u), seg_ref[u, qi - ki]) for u in range(ng)])

    n_far = jnp.maximum(qi - (d_far - 1), 0)
    m = lax.fori_loop(0, n_far, far, jnp.full((1, ng * w2), -1e30, F32))
    m = lax.fori_loop(n_far, qi, near, m)
    vis = (lax.broadcasted_iota(jnp.int32, (t, t), 0) // CHUNK
           <= lax.broadcasted_iota(jnp.int32, (t, t), 1) // CHUNK)
    update(qi, m, [biased(scores(qi, u), seg_ref[u, 0], vis) for u in range(ng)])
    for u in range(ng):
        acc = acc_scr[u]
        on = acc[:dv] * (1.0 / acc[dv:dv + 1])
        ot = (on[:, :t] - lam_ref[0] * on[:, t:]).T
        o_ref[0, :, u * dv:(u + 1) * dv] = _attn_out(ot, g_ref, out_scale).astype(o_ref.dtype)


def _attn_prompt(qkv, rel_bias, lam, subln_g, lam_init):
    bsz, s, _ = qkv.shape
    h, dv, t, g = H_B, 2 * DH_B, ATT_T, ATT_G
    n = s // t
    w = 2 * t
    d_far = -(-(_saturation_distance() - 1) // t) + 1
    segs = _bias_rows(rel_bias, [t - d * t for d in range(n)], w)
    kv = lambda sec: pl.BlockSpec((1, s, g * dv), lambda b, hg, qi, sec=sec: (b, 0, sec * (h // g) + hg))
    return pl.pallas_call(
        functools.partial(_attn_prompt_kernel, d_far=d_far, out_scale=1.0 - lam_init),
        grid=(bsz, h // g, n),
        in_specs=[pl.BlockSpec(memory_space=pltpu.SMEM), pl.BlockSpec(memory_space=pltpu.SMEM),
                  pl.BlockSpec((1, t, g * dv), lambda b, hg, qi: (b, qi, hg)),
                  kv(1), kv(2),
                  pl.BlockSpec((g, n, 1, w), lambda b, hg, qi: (hg, 0, 0, 0)),
                  pl.BlockSpec((1, dv), lambda b, hg, qi: (0, 0))],
        out_specs=pl.BlockSpec((1, t, g * dv), lambda b, hg, qi: (b, qi, hg)),
        out_shape=jax.ShapeDtypeStruct((bsz, s, h * dv), BF16),
        scratch_shapes=[pltpu.VMEM((g, n, t, dv), BF16), pltpu.VMEM((g, n, dv + ONES_ROWS, t), BF16),
                        pltpu.VMEM((g, dv, 2 * t), BF16), pltpu.VMEM((g, dv + ONES_ROWS, 2 * t), F32)],
        compiler_params=_cparams(("arbitrary",) * 3),
        name="diff_attn_prompt",
    )(lam, _far_bias(rel_bias), qkv, qkv, qkv, segs, subln_g.reshape(1, dv))


def _attn_sample_kernel(lam_ref, fb_ref, q_ref, kp_ref, vp_ref, segp_ref, kn_ref, vn_ref, segn_ref,
                        g_ref, o_ref, *, tkp, n_past, far_tiles, out_scale):
    hg = pl.program_id(1)
    tq = q_ref.shape[1]
    dv = g_ref.shape[1]
    ng = q_ref.shape[2] // dv
    nh = kp_ref.shape[1] // (n_past * tkp)
    w2 = 2 * tq
    qst = [_stack_q(q_ref[0, :, u * dv:(u + 1) * dv]) for u in range(ng)]
    lane = lax.broadcasted_iota(jnp.int32, (1, w2), 1)

    def bias(seg, tk):
        return jnp.where(lane < tq, _bias_tile_t(seg, tk, 0)[:, :w2], _bias_tile_t(seg, tk, tq)[:, :w2])

    m = jnp.full((1, ng * w2), -1e30, F32)
    far_shift = jnp.concatenate([jnp.full((1, w2), fb_ref[hg * ng + u], F32) for u in range(ng)], axis=1)
    acc = [jnp.zeros((dv + ONES_ROWS, w2), F32) for _ in range(ng)]
    for c in range(n_past + 1):
        sts, vts = [], []
        for u in range(ng):
            if c < n_past:
                rows = pl.ds(c * tkp * nh + hg * ng + u, tkp, stride=nh)
                k, v = kp_ref[0, rows, :], vp_ref[0, rows, :]
            else:
                k, v = kn_ref[0, :, u * dv:(u + 1) * dv], vn_ref[0, :, u * dv:(u + 1) * dv]
            st = jnp.dot(k.astype(BF16), qst[u], preferred_element_type=F32)
            if far_tiles <= c < n_past:
                st = st + bias(segp_ref[u, c - far_tiles], tkp)
            elif c == n_past:
                st = st + bias(segn_ref[u, 0], tq)
            sts.append(st)
            vts.append(_with_ones(v.T.astype(BF16)))
        m, alpha, p = _online(jnp.concatenate(sts, axis=1), m, far_shift if c < far_tiles else None)
        for u in range(ng):
            sl = slice(u * w2, (u + 1) * w2)
            acc[u] = alpha[:, sl] * acc[u] + jnp.dot(vts[u], p[:, sl], preferred_element_type=F32)
    for u in range(ng):
        on = acc[u][:dv] * (1.0 / acc[u][dv:dv + 1])
        ot = (on - lam_ref[0] * pltpu.roll(on, tq, 1)).T[:tq]
        o_ref[0, :, u * dv:(u + 1) * dv] = _attn_out(ot, g_ref, out_scale).astype(o_ref.dtype)


def _attn_sample(qkv, k_past, v_past, layer, rel_bias, lam, subln_g, lam_init):
    bsz, s, _ = qkv.shape
    h, dv, tkp, g = H_B, 2 * DH_B, ATT_T, ATT_G
    past = k_past.shape[1] // h
    n_past = past // tkp
    far_tiles = sum(1 for c in range(n_past) if past - (c * tkp + tkp - 1) >= _saturation_distance())
    near = list(range(far_tiles, n_past)) or [n_past - 1]
    segp = _bias_rows(rel_bias, [c * tkp - past + tkp for c in near], 2 * s + tkp)
    segn = _bias_rows(rel_bias, [s], 4 * s)
    pk = lambda: pl.BlockSpec((1, past * h, dv), lambda b, hg: (layer * bsz + b, 0, 0))
    nk = lambda sec: pl.BlockSpec((1, s, g * dv), lambda b, hg, sec=sec: (b, 0, sec * (h // g) + hg))
    return pl.pallas_call(
        functools.partial(_attn_sample_kernel, tkp=tkp, n_past=n_past, far_tiles=far_tiles,
                          out_scale=1.0 - lam_init),
        grid=(bsz, h // g),
        in_specs=[pl.BlockSpec(memory_space=pltpu.SMEM), pl.BlockSpec(memory_space=pltpu.SMEM),
                  nk(0), pk(), pk(),
                  pl.BlockSpec((g, len(near), 1, 2 * s + tkp), lambda b, hg: (hg, 0, 0, 0)),
                  nk(1), nk(2),
                  pl.BlockSpec((g, 1, 1, 4 * s), lambda b, hg: (hg, 0, 0, 0)),
                  pl.BlockSpec((1, dv), lambda b, hg: (0, 0))],
        out_specs=pl.BlockSpec((1, s, g * dv), lambda b, hg: (b, 0, hg)),
        out_shape=jax.ShapeDtypeStruct((bsz, s, h * dv), BF16),
        compiler_params=_cparams(("arbitrary",) * 2),
        name="diff_attn_sample",
    )(lam, _far_bias(rel_bias), qkv, k_past, v_past, segp, qkv, qkv, segn, subln_g.reshape(1, dv))


def _rglru_kernel(xw_ref, cw_ref, cb_ref, wa_ref, ba_ref, wx_ref, bx_ref, lam_ref, cbuf_ref, h0_ref,
                  y_ref, hlast_ref, ext_scr, h_scr):
    t = pl.program_id(1)
    tt = y_ref.shape[1]
    d = y_ref.shape[2]
    pad = ext_scr.shape[0] - tt

    @pl.when(t == 0)
    def _():
        ext_scr[0:pad] = cbuf_ref[0]
        h_scr[...] = h0_ref[0]

    gate = xw_ref[0, :, 0:d]
    u = xw_ref[0, :, d:2 * d]
    ext_scr[pad:pad + tt] = u
    conv = cb_ref[...] + cw_ref[CONV_W - 1:CONV_W] * u
    for j in range(1, CONV_W):
        conv = conv + cw_ref[CONV_W - 1 - j:CONV_W - j] * ext_scr[pad - j:pad - j + tt]
    ext_scr[0:pad] = ext_scr[tt:tt + pad]

    cb16 = conv.astype(BF16)
    blk = d // N_BLK_C
    ra, rx = [], []
    for n in range(N_BLK_C):
        cs = cb16[:, n * blk:(n + 1) * blk]
        ra.append(jnp.dot(cs, wa_ref[n], preferred_element_type=F32))
        rx.append(jnp.dot(cs, wx_ref[n], preferred_element_type=F32))
    r = _sigmoid(jnp.concatenate(ra, axis=1) + ba_ref[...])
    ig = _sigmoid(jnp.concatenate(rx, axis=1) + bx_ref[...])
    nl = -lam_ref[...]
    sp = jnp.maximum(nl, 0.0) + jnp.log(1.0 + jnp.exp(-jnp.abs(nl)))
    log_a = (-C_RG) * r * sp
    a = jnp.exp(log_a)
    xin = jnp.sqrt(-jnp.tanh(log_a) * (1.0 + a * a)) * ig * conv

    row = lax.broadcasted_iota(jnp.int32, (tt, d), 0)
    ca, cx = a, xin
    sft = 1
    while sft < tt:
        keep = row >= sft
        pa = jnp.where(keep, pltpu.roll(ca, sft, 0), 1.0)
        px = jnp.where(keep, pltpu.roll(cx, sft, 0), 0.0)
        cx = ca * px + cx
        ca = ca * pa
        sft *= 2
    hh = ca * h_scr[...] + cx
    h_scr[...] = hh[tt - 1:tt]
    gelu = 0.5 * gate * (1.0 + jnp.tanh(math.sqrt(2.0 / math.pi) * (gate + 0.044715 * gate * gate * gate)))
    y_ref[0] = (hh * gelu).astype(y_ref.dtype)

    @pl.when(t == pl.num_programs(1) - 1)
    def _():
        hlast_ref[0] = hh[tt - 1:tt]


def _rglru(xw, cbuf8, h0, conv_w, conv_b, w_a, b_a, w_x, b_x, lam, *, rows):
    bsz, s, d2 = xw.shape
    d = d2 // 2
    pad = cbuf8.shape[1]
    blk = d // N_BLK_C
    row = lambda a: a.reshape(1, d).astype(F32)
    full2 = lambda shp: pl.BlockSpec(shp, lambda b, t: (0,) * len(shp))
    return pl.pallas_call(
        _rglru_kernel,
        grid=(bsz, s // rows),
        in_specs=[pl.BlockSpec((1, rows, d2), lambda b, t: (b, t, 0)),
                  full2((CONV_W, d)), full2((1, d)),
                  full2((N_BLK_C, blk, blk)), full2((1, d)),
                  full2((N_BLK_C, blk, blk)), full2((1, d)),
                  full2((1, d)),
                  pl.BlockSpec((1, pad, d), lambda b, t: (b, 0, 0)),
                  pl.BlockSpec((1, 1, d), lambda b, t: (b, 0, 0))],
        out_specs=[pl.BlockSpec((1, rows, d), lambda b, t: (b, t, 0)),
                   pl.BlockSpec((1, 1, d), lambda b, t: (b, 0, 0))],
        out_shape=[jax.ShapeDtypeStruct((bsz, s, d), BF16),
                   jax.ShapeDtypeStruct((bsz, 1, d), F32)],
        scratch_shapes=[pltpu.VMEM((rows + pad, d), F32), pltpu.VMEM((1, d), F32)],
        compiler_params=_cparams(("arbitrary", "arbitrary")),
        name="rglru",
    )(xw, conv_w.astype(F32), row(conv_b), w_a.astype(BF16), row(b_a), w_x.astype(BF16), row(b_x),
      row(lam), cbuf8, h0)


def _route(x, wr_t, rbias):
    ne = wr_t.shape[0]
    tm = x.shape[0]
    gsz = ne // N_GROUPS
    logits = lax.dot_general(wr_t, x, _NT, preferred_element_type=F32)
    score = _sigmoid(logits)
    biased = score + rbias
    neg = -jnp.inf
    sub = lax.broadcasted_iota(jnp.int32, (gsz, tm), 0)
    gscore = []
    for g in range(N_GROUPS):
        vg = biased[g * gsz:(g + 1) * gsz]
        m1 = jnp.max(vg, axis=0, keepdims=True)
        first = jnp.min(jnp.where(vg == m1, sub, gsz), axis=0, keepdims=True)
        m2 = jnp.max(jnp.where(sub == first, neg, vg), axis=0, keepdims=True)
        gscore.append(m1 + m2)
    masked = []
    for g in range(N_GROUPS):
        rank = jnp.zeros((1, tm), jnp.int32)
        for o in range(N_GROUPS):
            if o == g:
                continue
            ahead = (gscore[o] >= gscore[g]) if o < g else (gscore[o] > gscore[g])
            rank = rank + ahead.astype(jnp.int32)
        keep = rank < TOPK_GROUPS
        masked.append(jnp.where(keep, biased[g * gsz:(g + 1) * gsz], neg))
    cur = jnp.concatenate(masked, axis=0)
    eidx = lax.broadcasted_iota(jnp.int32, (ne, tm), 0)
    chosen = jnp.zeros((ne, tm), F32)
    for _ in range(TOP_K):
        m = jnp.max(cur, axis=0, keepdims=True)
        first = jnp.min(jnp.where(cur == m, eidx, ne), axis=0, keepdims=True)
        hit = eidx == first
        chosen = jnp.where(hit, 1.0, chosen)
        cur = jnp.where(hit, neg, cur)
    wsel = chosen * score
    return wsel / jnp.sum(wsel, axis=0, keepdims=True) * ROUTED_SCALE


def _swiglu(h, f):
    hg = h[:, :f]
    return hg * _sigmoid(hg) * h[:, f:]


def _moe_kernel(inv_ref, invd_ref, x_ref, wr_ref, rb_ref, wsi_ref, wsd_ref, wi_ref, wd_ref, g_ref, b_ref,
                out_ref, xb_scr, x8_scr, acc_scr, gate_scr, *, eb):
    j = pl.program_id(1)
    f = wsd_ref.shape[0]

    @pl.when(j == 0)
    def _():
        x = x_ref[...]
        xb = x.astype(BF16)
        xb_scr[...] = xb
        x8_scr[...] = x.astype(F8)
        w_et = _route(xb, wr_ref[...], rb_ref[...])
        ne, tm = w_et.shape
        gate_scr[...] = jnp.concatenate([w_et, jnp.zeros((gate_scr.shape[1] - ne, tm), F32)], axis=0).T
        sh = _swiglu(jnp.dot(xb, wsi_ref[...], preferred_element_type=F32), f)
        acc_scr[...] = jnp.dot(sh.astype(BF16), wsd_ref[...], preferred_element_type=F32)

    x8 = x8_scr[...]
    gates = gate_scr[...]
    lane = lax.broadcasted_iota(jnp.int32, gates.shape, 1)
    acts = []
    for i in range(eb):
        e = j * eb + i
        h = jnp.dot(x8, wi_ref[i], preferred_element_type=F32) * inv_ref[e]
        gcol = jnp.sum(jnp.where(lane == e, gates, 0.0), axis=1, keepdims=True)
        act = _swiglu(h, f) * (gcol * ACT_SCALE)
        acts.append(jnp.clip(act, -F8_MAX, F8_MAX).astype(F8))
    acc_scr[...] += (jnp.dot(jnp.concatenate(acts, axis=1), wd_ref[0], preferred_element_type=F32)
                     * (invd_ref[j] * (1.0 / ACT_SCALE)))

    @pl.when(j == pl.num_programs(1) - 1)
    def _():
        out_ref[...] = _layer_norm(ALPHA * x_ref[...] + acc_scr[...], g_ref[...], b_ref[...])


def _fp8_weights(w):
    amax = jnp.max(jnp.abs(w.astype(F32)), axis=tuple(range(1, w.ndim)))
    k = jnp.floor(jnp.log2(F8_MAX / jnp.maximum(amax, 1e-30))) - 1.0
    scale = jnp.exp2(jnp.clip(k, -60.0, 60.0))
    return (w.astype(F32) * scale.reshape((-1,) + (1,) * (w.ndim - 1))).astype(F8), 1.0 / scale


def _moe_ln(x, w_router, router_bias, w_in, w_down, ws_in, ws_down, g, b, *, tm):
    t, d = x.shape
    ne, _, f2 = w_in.shape
    f = f2 // 2
    eb = MOE_EB
    w8, inv_scale = _fp8_weights(w_in)
    wd8, inv_scale_d = _fp8_weights(w_down.reshape(ne // eb, eb * f, d))
    const = lambda shp: pl.BlockSpec(shp, lambda i, j: (0,) * len(shp))
    return pl.pallas_call(
        functools.partial(_moe_kernel, eb=eb),
        grid=(t // tm, ne // eb),
        in_specs=[pl.BlockSpec(memory_space=pltpu.SMEM), pl.BlockSpec(memory_space=pltpu.SMEM),
                  pl.BlockSpec((tm, d), lambda i, j: (i, 0)),
                  const((ne, d)), const((ne, 1)),
                  const((d, f2)), const((f, d)),
                  pl.BlockSpec((eb, d, f2), lambda i, j: (j, 0, 0)),
                  pl.BlockSpec((1, eb * f, d), lambda i, j: (j, 0, 0)),
                  const((1, d)), const((1, d))],
        out_specs=pl.BlockSpec((tm, d), lambda i, j: (i, 0)),
        out_shape=jax.ShapeDtypeStruct((t, d), F32),
        scratch_shapes=[pltpu.VMEM((tm, d), BF16), pltpu.VMEM((tm, d), F8), pltpu.VMEM((tm, d), F32),
                        pltpu.VMEM((tm, 128), F32)],
        compiler_params=_cparams(("arbitrary", "arbitrary")),
        name="moe_ln",
    )(inv_scale, inv_scale_d, x, w_router.T.astype(BF16), router_bias.reshape(ne, 1).astype(F32),
      ws_in.astype(BF16), ws_down.astype(BF16), w8, wd8,
      g.reshape(1, d), b.reshape(1, d))


def _lambda_init(layer_idx):
    return 0.8 - 0.6 * math.exp(-0.3 * layer_idx)


def kernel(x_prompt, x_sample, state_hgrn_s, cache_k, cache_v, state_rglru_conv, state_rglru_h, hgrn_lb_logits, hgrn_w_in, hgrn_gn_g, hgrn_w_out, rel_bias, attn_w_qkv, attn_lam_q1, attn_lam_k1, attn_lam_q2, attn_lam_k2, attn_subln_g, attn_w_out, rglru_w_in, rglru_conv_w, rglru_conv_b, rglru_w_a, rglru_b_a, rglru_w_x, rglru_b_x, rglru_lam, rglru_w_out, ln1_g, ln1_b, ln2_g, ln2_b, moe_w_router, moe_router_bias, moe_w_in, moe_w_down, moe_ws_in, moe_ws_down):
    bp, sp, d = x_prompt.shape
    bs, ss, _ = x_sample.shape
    lb_p = jax.nn.softmax(hgrn_lb_logits.astype(F32), axis=0)
    lower_bounds = jnp.clip(jnp.cumsum(lb_p, axis=0) - lb_p[0], 0.0, 1.0)

    streams = [(x_prompt.reshape(bp * sp, d), bp, sp), (x_sample.reshape(bs * ss, d), bs, ss)]
    tile = lambda t: 1024 if t % 1024 == 0 else 512
    hs, kvs, cvs, hls = [[], []], [[], []], [[], []], [[], []]
    for i in range(DEPTH):
        j, kind = i // N_MIXERS, i % N_MIXERS
        new_streams = []
        for si, (x, bsz, s) in enumerate(streams):
            t = bsz * s
            tm = tile(t)
            if kind == 0:
                xw = _matmul(x, hgrn_w_in[j].astype(BF16), tm=512)
                s0 = (jnp.zeros((bsz, H_A, d // H_A, d // H_A), F32) if si == 0
                      else state_hgrn_s[j].astype(F32))
                o, s_fin = _hgrn(xw.reshape(bsz, s, -1), s0, lower_bounds[i], hgrn_gn_g[j],
                                 rows=min(s, 256))
                hs[si].append(s_fin)
                w_out = hgrn_w_out[j]
            elif kind == 1:
                qkv = _matmul(x, attn_w_qkv[j].astype(BF16), tm=512)
                hd = H_B * 2 * DH_B
                lam_init = _lambda_init(i)
                lam = (jnp.exp(jnp.sum(attn_lam_q1[j].astype(F32) * attn_lam_k1[j].astype(F32)))
                       - jnp.exp(jnp.sum(attn_lam_q2[j].astype(F32) * attn_lam_k2[j].astype(F32)))
                       + lam_init).reshape(1)
                qkv3 = qkv.reshape(bsz, s, 3 * hd)
                if si == 0:
                    o = _attn_prompt(qkv3, rel_bias, lam, attn_subln_g[j], lam_init)
                else:
                    past = cache_k.shape[2]
                    o = _attn_sample(qkv3, cache_k.reshape(-1, past * H_B, 2 * DH_B),
                                     cache_v.reshape(-1, past * H_B, 2 * DH_B),
                                     j, rel_bias, lam, attn_subln_g[j], lam_init)
                kvs[si].append((qkv3[:, :, hd:2 * hd].reshape(bsz, s, H_B, 2 * DH_B),
                                qkv3[:, :, 2 * hd:].reshape(bsz, s, H_B, 2 * DH_B)))
                w_out = attn_w_out[j]
            else:
                xw = _matmul(x, rglru_w_in[j].astype(BF16), tm=512).reshape(bsz, s, 2 * d)
                if si == 0:
                    cbuf = jnp.zeros((bsz, CONV_W - 1, d), F32)
                    h0 = jnp.zeros((bsz, d), F32)
                else:
                    cbuf, h0 = state_rglru_conv[j].astype(F32), state_rglru_h[j].astype(F32)
                cbuf8 = jnp.pad(cbuf, ((0, 0), (8 - (CONV_W - 1), 0), (0, 0)))
                o, hlast = _rglru(xw, cbuf8, h0.reshape(bsz, 1, d), rglru_conv_w[j], rglru_conv_b[j],
                                  rglru_w_a[j], rglru_b_a[j], rglru_w_x[j], rglru_b_x[j], rglru_lam[j],
                                  rows=min(s, 256))
                cvs[si].append(xw[:, s - (CONV_W - 1):, d:])
                hls[si].append(hlast.reshape(bsz, d))
                w_out = rglru_w_out[j]
            x = _proj_ln(o.reshape(t, d), w_out.astype(BF16), x, ln1_g[i], ln1_b[i], tm=tm)
            x = _moe_ln(x, moe_w_router[i], moe_router_bias[i], moe_w_in[i], moe_w_down[i],
                        moe_ws_in[i], moe_ws_down[i], ln2_g[i], ln2_b[i], tm=tm)
            new_streams.append((x, bsz, s))
        streams = new_streams

    yp = streams[0][0].reshape(bp, sp, d)
    ys = streams[1][0].reshape(bs, ss, d)
    st = lambda xs: jnp.stack(xs)
    return (yp, ys, st(hs[0]), st(hs[1]),
            st([kv[0] for kv in kvs[0]]), st([kv[1] for kv in kvs[0]]),
            st([kv[0] for kv in kvs[1]]), st([kv[1] for kv in kvs[1]]),
            st(cvs[0]), st(hls[0]), st(cvs[1]), st(hls[1]))
```

```python
import functools
import math

import jax
import jax.numpy as jnp
from jax import lax
from jax.experimental import pallas as pl
from jax.experimental.pallas import tpu as pltpu

F32 = jnp.float32
BF16 = jnp.bfloat16
F8 = jnp.float8_e4m3fn
F8_MAX = 448.0
ACT_SCALE = 32.0

DEPTH = 4
CHUNK = 64
N_MIXERS = 3
H_A = 8
H_B = 8
DH_B = 64
REL_BUCKETS = 32
REL_MAX_DIST = 1024
N_BLK_C = 4
CONV_W = 4
C_RG = 8.0
N_EXPERTS = 64
D_EXPERT = 128
TOP_K = 8
N_GROUPS = 8
TOPK_GROUPS = 4
ROUTED_SCALE = 2.5
ALPHA = (2 * DEPTH) ** 0.25
LN_EPS = 1e-5
RMS_EPS = 1e-6

VMEM_LIMIT = 56 * 1024 * 1024
LOG2E = math.log2(math.e)
REC_C = 64
ATT_T = 256
ATT_G = 4
ONES_ROWS = 16
MOE_EB = 16

_NT = (((1,), (1,)), ((), ()))
_TN = (((0,), (0,)), ((), ()))


def _cparams(sem):
    return pltpu.CompilerParams(dimension_semantics=sem, vmem_limit_bytes=VMEM_LIMIT)


def _sigmoid(x):
    return 1.0 / (1.0 + jnp.exp(-x))


def _layer_norm(z, g, b):
    mu = jnp.mean(z, -1, keepdims=True)
    zc = z - mu
    var = jnp.mean(zc * zc, -1, keepdims=True)
    return zc * lax.rsqrt(var + LN_EPS) * g + b


def _mm_kernel(x_ref, w_ref, o_ref):
    o_ref[...] = jnp.dot(x_ref[...].astype(BF16), w_ref[...],
                         preferred_element_type=F32).astype(o_ref.dtype)


def _matmul(x, w, *, tm, out_dtype=F32):
    t, k = x.shape
    n = w.shape[1]
    return pl.pallas_call(
        _mm_kernel,
        grid=(t // tm,),
        in_specs=[pl.BlockSpec((tm, k), lambda i: (i, 0)),
                  pl.BlockSpec((k, n), lambda i: (0, 0))],
        out_specs=pl.BlockSpec((tm, n), lambda i: (i, 0)),
        out_shape=jax.ShapeDtypeStruct((t, n), out_dtype),
        compiler_params=_cparams(("arbitrary",)),
        name="proj_in",
    )(x, w)


def _proj_ln_kernel(o_ref, w_ref, x_ref, g_ref, b_ref, out_ref):
    y = jnp.dot(o_ref[...], w_ref[...], preferred_element_type=F32)
    out_ref[...] = _layer_norm(ALPHA * x_ref[...] + y, g_ref[...], b_ref[...])


def _proj_ln(o, w, x, g, b, *, tm):
    t, d = x.shape
    k = o.shape[1]
    return pl.pallas_call(
        _proj_ln_kernel,
        grid=(t // tm,),
        in_specs=[pl.BlockSpec((tm, k), lambda i: (i, 0)),
                  pl.BlockSpec((k, d), lambda i: (0, 0)),
                  pl.BlockSpec((tm, d), lambda i: (i, 0)),
                  pl.BlockSpec((1, d), lambda i: (0, 0)),
                  pl.BlockSpec((1, d), lambda i: (0, 0))],
        out_specs=pl.BlockSpec((tm, d), lambda i: (i, 0)),
        out_shape=jax.ShapeDtypeStruct((t, d), F32),
        compiler_params=_cparams(("arbitrary",)),
        name="proj_out_ln",
    )(o, w, x, g.reshape(1, d), b.reshape(1, d))


def _hgrn_kernel(q_ref, fz_ref, v_ref, g_ref, lb_ref, gn_ref, s0_ref, o_ref, sfin_ref, st_scr,
                 *, chunk):
    c = pl.program_id(1)
    nh, dv, dk = st_scr.shape

    @pl.when(c == 0)
    def _():
        for h in range(nh):
            st_scr[h] = s0_ref[0, h].T

    rows = fz_ref.shape[1]
    lb = lb_ref[...]
    causal = (lax.broadcasted_iota(jnp.int32, (chunk, chunk), 0)
              >= lax.broadcasted_iota(jnp.int32, (chunk, chunk), 1))
    tri = jnp.where(causal, 1.0, 0.0).astype(BF16)
    sts = [st_scr[h] for h in range(nh)]
    for i in range(rows // chunk):
        sl = slice(i * chunk, (i + 1) * chunk)
        fz = fz_ref[0, sl, :]
        e = jnp.exp(-jnp.abs(fz))
        r = 1.0 / (1.0 + e)
        pos = fz >= 0
        sig = jnp.where(pos, r, e * r)
        nsig = jnp.where(pos, e * r, r)
        logf = jnp.log(lb + (1.0 - lb) * sig)
        kk = (1.0 - lb) * nsig
        hi = logf.astype(BF16)
        r1 = logf - hi.astype(F32)
        mid = r1.astype(BF16)
        lo = (r1 - mid.astype(F32)).astype(BF16)
        b = (jnp.dot(tri, hi, preferred_element_type=F32)
             + jnp.dot(tri, mid, preferred_element_type=F32)
             + jnp.dot(tri, lo, preferred_element_type=F32))
        bm = b[chunk // 2 - 1:chunk // 2]
        be = b[chunk - 1:chunk]
        qv = q_ref[0, sl, :]
        a_fac = qv * _sigmoid(qv) * jnp.exp(b - bm)
        k_fac = kk * jnp.exp(bm - b)
        qe = (a_fac * jnp.exp(bm)).astype(BF16)
        kd = (k_fac * jnp.exp(be - bm)).astype(BF16)
        a16 = a_fac.astype(BF16)
        k16 = k_fac.astype(BF16)
        dec = jnp.exp(be)
        v = v_ref[0, sl, :].astype(BF16)
        gv = g_ref[0, sl, :]
        gate = gv * _sigmoid(gv)
        for h in range(nh):
            hs = slice(h * dk, (h + 1) * dk)
            st = sts[h]
            o = lax.dot_general(qe[:, hs], st.astype(BF16), _NT, preferred_element_type=F32)
            sc = lax.dot_general(a16[:, hs], k16[:, hs], _NT, preferred_element_type=F32)
            sc = jnp.where(causal, sc, 0.0)
            o = o + jnp.dot(sc.astype(BF16), v[:, hs], preferred_element_type=F32)
            sts[h] = st * dec[:, hs] + lax.dot_general(v[:, hs], kd[:, hs], _TN,
                                                       preferred_element_type=F32)
            o = o * lax.rsqrt(jnp.mean(o * o, -1, keepdims=True) + RMS_EPS) * gn_ref[...] * gate[:, hs]
            o_ref[0, sl, hs] = o.astype(o_ref.dtype)
    for h in range(nh):
        st_scr[h] = sts[h]

    @pl.when(c == pl.num_programs(1) - 1)
    def _():
        for h in range(nh):
            sfin_ref[0, h] = st_scr[h].T


def _hgrn(xw, s0, lb, gn_g, *, rows):
    bsz, s, _ = xw.shape
    h, dk, dv = s0.shape[1:]
    sec = lambda k: pl.BlockSpec((1, rows, h * dk), lambda b, c, k=k: (b, c, k))
    return pl.pallas_call(
        functools.partial(_hgrn_kernel, chunk=REC_C),
        grid=(bsz, s // rows),
        in_specs=[sec(0), sec(1), sec(2), sec(3),
                  pl.BlockSpec((1, h * dk), lambda b, c: (0, 0)),
                  pl.BlockSpec((1, dv), lambda b, c: (0, 0)),
                  pl.BlockSpec((1, h, dk, dv), lambda b, c: (b, 0, 0, 0))],
        out_specs=[pl.BlockSpec((1, rows, h * dv), lambda b, c: (b, c, 0)),
                   pl.BlockSpec((1, h, dk, dv), lambda b, c: (b, 0, 0, 0))],
        out_shape=[jax.ShapeDtypeStruct((bsz, s, h * dv), BF16),
                   jax.ShapeDtypeStruct((bsz, h, dk, dv), F32)],
        scratch_shapes=[pltpu.VMEM((h, dv, dk), F32)],
        compiler_params=_cparams(("arbitrary", "arbitrary")),
        name="hgrn2",
    )(xw, xw, xw, xw, lb.reshape(1, h * dk), gn_g.reshape(1, dv), s0)


def _rel_bucket(rel):
    nb = REL_BUCKETS // 2
    max_exact = nb // 2
    ret = jnp.where(rel > 0, nb, 0)
    n = jnp.abs(rel)
    nf = jnp.maximum(n, 1).astype(F32)
    large = max_exact + (jnp.log(nf / max_exact) / math.log(REL_MAX_DIST / max_exact)
                         * (nb - max_exact)).astype(jnp.int32)
    large = jnp.minimum(large, nb - 1)
    return ret + jnp.where(n < max_exact, n, large)


def _saturation_distance():
    nb = REL_BUCKETS // 2
    max_exact = nb // 2
    return math.ceil(max_exact * (REL_MAX_DIST / max_exact) ** ((nb - 1 - max_exact) / (nb - max_exact))) + 1


def _bias_rows(rel_bias, rel0s, width):
    rel = jnp.asarray(rel0s, jnp.int32)[:, None] - jnp.arange(width, dtype=jnp.int32)[None, :]
    seg = rel_bias.astype(F32)[_rel_bucket(rel)] * LOG2E
    return jnp.transpose(seg, (2, 0, 1))[:, :, None, :]


def _far_bias(rel_bias):
    return rel_bias.astype(F32)[_rel_bucket(jnp.int32(-_saturation_distance()))] * LOG2E


def _bias_tile_t(seg, tk, shift):
    w = seg.shape[1]
    return pltpu.roll(jnp.broadcast_to(seg, (tk, w)), w - tk + shift, 1, stride=1, stride_axis=0)


def _stack_q(q):
    q = q * (DH_B ** -0.5 * LOG2E)
    lane = lax.broadcasted_iota(jnp.int32, q.shape, 1)
    qs = jnp.concatenate([jnp.where(lane < DH_B, q, 0.0), jnp.where(lane >= DH_B, q, 0.0)], axis=0)
    return qs.T.astype(BF16)


def _online(st, m, shift=None):
    if shift is None:
        m_new = jnp.maximum(m, jnp.max(st, axis=0, keepdims=True))
        p = jnp.exp2(st - m_new)
    else:
        m_raw = jnp.maximum(m - shift, jnp.max(st, axis=0, keepdims=True))
        p = jnp.exp2(st - m_raw)
        m_new = m_raw + shift
    return m_new, jnp.exp2(m - m_new), p.astype(BF16)


def _with_ones(vt):
    return jnp.concatenate([vt, jnp.ones((ONES_ROWS, vt.shape[1]), vt.dtype)], axis=0)


def _attn_out(ot, g_ref, out_scale):
    return ot * lax.rsqrt(jnp.mean(ot * ot, -1, keepdims=True) + RMS_EPS) * g_ref[...] * out_scale


def _attn_prompt_kernel(lam_ref, fb_ref, q_ref, k_ref, v_ref, seg_ref, g_ref, o_ref,
                        kb_scr, vt_scr, qst_scr, acc_scr, *, d_far, out_scale):
    hg = pl.program_id(1)
    qi = pl.program_id(2)
    ng, n, t, dv = kb_scr.shape
    w2 = 2 * t

    @pl.when(qi == 0)
    def _():
        for u in range(ng):
            for c in range(n):
                kb_scr[u, c] = k_ref[0, c * t:(c + 1) * t, u * dv:(u + 1) * dv].astype(BF16)
                vt_scr[u, c] = _with_ones(v_ref[0, c * t:(c + 1) * t, u * dv:(u + 1) * dv].T.astype(BF16))

    for u in range(ng):
        qst_scr[u] = _stack_q(q_ref[0, :, u * dv:(u + 1) * dv])
    acc_scr[...] = jnp.zeros(acc_scr.shape, F32)
    far_shift = jnp.concatenate([jnp.full((1, w2), fb_ref[hg * ng + u], F32) for u in range(ng)], axis=1)

    def scores(ki, u):
        return jnp.dot(kb_scr[u, ki], qst_scr[u], preferred_element_type=F32)

    def biased(st, seg, vis=None):
        bt = _bias_tile_t(seg, t, 0)[:, :t]
        s1, s2 = st[:, :t] + bt, st[:, t:] + bt
        if vis is not None:
            s1, s2 = jnp.where(vis, s1, -1e30), jnp.where(vis, s2, -1e30)
        return jnp.concatenate([s1, s2], axis=1)

    def update(ki, m, sts, shift=None):
        m, alpha, p = _online(jnp.concatenate(sts, axis=1), m, shift)
        for u in range(ng):
            sl = slice(u * w2, (u + 1) * w2)
            acc_scr[u] = alpha[:, sl] * acc_scr[u] + jnp.dot(vt_scr[u, ki], p[:, sl],
                                                             preferred_element_type=F32)
        return m

    def far_scores(ki):
        return [scores(ki, u) for u in range(ng)]

    def near_scores(ki):
        return [biased(scores(ki, u), seg_ref[u, qi - ki]) for u in range(ng)]

    def tiles(lo, hi, tile_scores, shift, m):
        def two(i, m):
            ka = lo + 2 * i
            sa, sb = tile_scores(ka), tile_scores(ka + 1)
            return update(ka + 1, update(ka, m, sa, shift), sb, shift)
        n2 = (hi - lo) // 2
        m = lax.fori_loop(0, n2, two, m)
        return lax.fori_loop(lo + 2 * n2, hi, lambda ki, m: update(ki, m, tile_scores(ki), shift), m)

    n_far = jnp.maximum(qi - (d_far - 1), 0)
    m = tiles(0, n_far, far_scores, far_shift, jnp.full((1, ng * w2), -1e30, F32))
    m = tiles(n_far, qi, near_scores, None, m)
    vis = (lax.broadcasted_iota(jnp.int32, (t, t), 0) // CHUNK
           <= lax.broadcasted_iota(jnp.int32, (t, t), 1) // CHUNK)
    update(qi, m, [biased(scores(qi, u), seg_ref[u, 0], vis) for u in range(ng)])
    for u in range(ng):
        acc = acc_scr[u]
        on = acc[:dv] * (1.0 / acc[dv:dv + 1])
        ot = (on[:, :t] - lam_ref[0] * on[:, t:]).T
        o_ref[0, :, u * dv:(u + 1) * dv] = _attn_out(ot, g_ref, out_scale).astype(o_ref.dtype)


def _attn_prompt(qkv, rel_bias, lam, subln_g, lam_init):
    bsz, s, _ = qkv.shape
    h, dv, t, g = H_B, 2 * DH_B, ATT_T, ATT_G
    n = s // t
    w = 2 * t
    d_far = -(-(_saturation_distance() - 1) // t) + 1
    segs = _bias_rows(rel_bias, [t - d * t for d in range(n)], w)
    kv = lambda sec: pl.BlockSpec((1, s, g * dv), lambda b, hg, qi, sec=sec: (b, 0, sec * (h // g) + hg))
    return pl.pallas_call(
        functools.partial(_attn_prompt_kernel, d_far=d_far, out_scale=1.0 - lam_init),
        grid=(bsz, h // g, n),
        in_specs=[pl.BlockSpec(memory_space=pltpu.SMEM), pl.BlockSpec(memory_space=pltpu.SMEM),
                  pl.BlockSpec((1, t, g * dv), lambda b, hg, qi: (b, qi, hg)),
                  kv(1), kv(2),
                  pl.BlockSpec((g, n, 1, w), lambda b, hg, qi: (hg, 0, 0, 0)),
                  pl.BlockSpec((1, dv), lambda b, hg, qi: (0, 0))],
        out_specs=pl.BlockSpec((1, t, g * dv), lambda b, hg, qi: (b, qi, hg)),
        out_shape=jax.ShapeDtypeStruct((bsz, s, h * dv), BF16),
        scratch_shapes=[pltpu.VMEM((g, n, t, dv), BF16), pltpu.VMEM((g, n, dv + ONES_ROWS, t), BF16),
                        pltpu.VMEM((g, dv, 2 * t), BF16), pltpu.VMEM((g, dv + ONES_ROWS, 2 * t), F32)],
        compiler_params=_cparams(("arbitrary",) * 3),
        name="diff_attn_prompt",
    )(lam, _far_bias(rel_bias), qkv, qkv, qkv, segs, subln_g.reshape(1, dv))


def _attn_sample_kernel(lam_ref, fb_ref, q_ref, kp_ref, vp_ref, segp_ref, kn_ref, vn_ref, segn_ref,
                        g_ref, o_ref, *, tkp, n_past, far_tiles, out_scale):
    hg = pl.program_id(1)
    tq = q_ref.shape[1]
    dv = g_ref.shape[1]
    ng = q_ref.shape[2] // dv
    nh = kp_ref.shape[1] // (n_past * tkp)
    w2 = 2 * tq
    qst = [_stack_q(q_ref[0, :, u * dv:(u + 1) * dv]) for u in range(ng)]
    lane = lax.broadcasted_iota(jnp.int32, (1, w2), 1)

    def bias(seg, tk):
        return jnp.where(lane < tq, _bias_tile_t(seg, tk, 0)[:, :w2], _bias_tile_t(seg, tk, tq)[:, :w2])

    m = jnp.full((1, ng * w2), -1e30, F32)
    far_shift = jnp.concatenate([jnp.full((1, w2), fb_ref[hg * ng + u], F32) for u in range(ng)], axis=1)
    acc = [jnp.zeros((dv + ONES_ROWS, w2), F32) for _ in range(ng)]
    for c in range(n_past + 1):
        sts, vts = [], []
        for u in range(ng):
            if c < n_past:
                rows = pl.ds(c * tkp * nh + hg * ng + u, tkp, stride=nh)
                k, v = kp_ref[0, rows, :], vp_ref[0, rows, :]
            else:
                k, v = kn_ref[0, :, u * dv:(u + 1) * dv], vn_ref[0, :, u * dv:(u + 1) * dv]
            st = jnp.dot(k.astype(BF16), qst[u], preferred_element_type=F32)
            if far_tiles <= c < n_past:
                st = st + bias(segp_ref[u, c - far_tiles], tkp)
            elif c == n_past:
                st = st + bias(segn_ref[u, 0], tq)
            sts.append(st)
            vts.append(_with_ones(v.T.astype(BF16)))
        m, alpha, p = _online(jnp.concatenate(sts, axis=1), m, far_shift if c < far_tiles else None)
        for u in range(ng):
            sl = slice(u * w2, (u + 1) * w2)
            acc[u] = alpha[:, sl] * acc[u] + jnp.dot(vts[u], p[:, sl], preferred_element_type=F32)
    for u in range(ng):
        on = acc[u][:dv] * (1.0 / acc[u][dv:dv + 1])
        ot = (on - lam_ref[0] * pltpu.roll(on, tq, 1)).T[:tq]
        o_ref[0, :, u * dv:(u + 1) * dv] = _attn_out(ot, g_ref, out_scale).astype(o_ref.dtype)


def _attn_sample(qkv, k_past, v_past, layer, rel_bias, lam, subln_g, lam_init):
    bsz, s, _ = qkv.shape
    h, dv, tkp, g = H_B, 2 * DH_B, ATT_T, ATT_G
    past = k_past.shape[1] // h
    n_past = past // tkp
    far_tiles = sum(1 for c in range(n_past) if past - (c * tkp + tkp - 1) >= _saturation_distance())
    near = list(range(far_tiles, n_past)) or [n_past - 1]
    segp = _bias_rows(rel_bias, [c * tkp - past + tkp for c in near], 2 * s + tkp)
    segn = _bias_rows(rel_bias, [s], 4 * s)
    pk = lambda: pl.BlockSpec((1, past * h, dv), lambda b, hg: (layer * bsz + b, 0, 0))
    nk = lambda sec: pl.BlockSpec((1, s, g * dv), lambda b, hg, sec=sec: (b, 0, sec * (h // g) + hg))
    return pl.pallas_call(
        functools.partial(_attn_sample_kernel, tkp=tkp, n_past=n_past, far_tiles=far_tiles,
                          out_scale=1.0 - lam_init),
        grid=(bsz, h // g),
        in_specs=[pl.BlockSpec(memory_space=pltpu.SMEM), pl.BlockSpec(memory_space=pltpu.SMEM),
                  nk(0), pk(), pk(),
                  pl.BlockSpec((g, len(near), 1, 2 * s + tkp), lambda b, hg: (hg, 0, 0, 0)),
                  nk(1), nk(2),
                  pl.BlockSpec((g, 1, 1, 4 * s), lambda b, hg: (hg, 0, 0, 0)),
                  pl.BlockSpec((1, dv), lambda b, hg: (0, 0))],
        out_specs=pl.BlockSpec((1, s, g * dv), lambda b, hg: (b, 0, hg)),
        out_shape=jax.ShapeDtypeStruct((bsz, s, h * dv), BF16),
        compiler_params=_cparams(("arbitrary",) * 2),
        name="diff_attn_sample",
    )(lam, _far_bias(rel_bias), qkv, k_past, v_past, segp, qkv, qkv, segn, subln_g.reshape(1, dv))


def _rglru_kernel(xw_ref, cw_ref, cb_ref, wa_ref, ba_ref, wx_ref, bx_ref, lam_ref, cbuf_ref, h0_ref,
                  y_ref, hlast_ref, ext_scr, h_scr):
    t = pl.program_id(1)
    tt = y_ref.shape[1]
    d = y_ref.shape[2]
    pad = ext_scr.shape[0] - tt

    @pl.when(t == 0)
    def _():
        ext_scr[0:pad] = cbuf_ref[0]
        h_scr[...] = h0_ref[0]

    gate = xw_ref[0, :, 0:d]
    u = xw_ref[0, :, d:2 * d]
    ext_scr[pad:pad + tt] = u
    conv = cb_ref[...] + cw_ref[CONV_W - 1:CONV_W] * u
    for j in range(1, CONV_W):
        conv = conv + cw_ref[CONV_W - 1 - j:CONV_W - j] * ext_scr[pad - j:pad - j + tt]
    ext_scr[0:pad] = ext_scr[tt:tt + pad]

    cb16 = conv.astype(BF16)
    blk = d // N_BLK_C
    ra, rx = [], []
    for n in range(N_BLK_C):
        cs = cb16[:, n * blk:(n + 1) * blk]
        ra.append(jnp.dot(cs, wa_ref[n], preferred_element_type=F32))
        rx.append(jnp.dot(cs, wx_ref[n], preferred_element_type=F32))
    r = _sigmoid(jnp.concatenate(ra, axis=1) + ba_ref[...])
    ig = _sigmoid(jnp.concatenate(rx, axis=1) + bx_ref[...])
    nl = -lam_ref[...]
    sp = jnp.maximum(nl, 0.0) + jnp.log(1.0 + jnp.exp(-jnp.abs(nl)))
    log_a = (-C_RG) * r * sp
    a = jnp.exp(log_a)
    xin = jnp.sqrt(-jnp.tanh(log_a) * (1.0 + a * a)) * ig * conv

    row = lax.broadcasted_iota(jnp.int32, (tt, d), 0)
    ca, cx = a, xin
    sft = 1
    while sft < tt:
        keep = row >= sft
        pa = jnp.where(keep, pltpu.roll(ca, sft, 0), 1.0)
        px = jnp.where(keep, pltpu.roll(cx, sft, 0), 0.0)
        cx = ca * px + cx
        ca = ca * pa
        sft *= 2
    hh = ca * h_scr[...] + cx
    h_scr[...] = hh[tt - 1:tt]
    gelu = 0.5 * gate * (1.0 + jnp.tanh(math.sqrt(2.0 / math.pi) * (gate + 0.044715 * gate * gate * gate)))
    y_ref[0] = (hh * gelu).astype(y_ref.dtype)

    @pl.when(t == pl.num_programs(1) - 1)
    def _():
        hlast_ref[0] = hh[tt - 1:tt]


def _rglru(xw, cbuf8, h0, conv_w, conv_b, w_a, b_a, w_x, b_x, lam, *, rows):
    bsz, s, d2 = xw.shape
    d = d2 // 2
    pad = cbuf8.shape[1]
    blk = d // N_BLK_C
    row = lambda a: a.reshape(1, d).astype(F32)
    full2 = lambda shp: pl.BlockSpec(shp, lambda b, t: (0,) * len(shp))
    return pl.pallas_call(
        _rglru_kernel,
        grid=(bsz, s // rows),
        in_specs=[pl.BlockSpec((1, rows, d2), lambda b, t: (b, t, 0)),
                  full2((CONV_W, d)), full2((1, d)),
                  full2((N_BLK_C, blk, blk)), full2((1, d)),
                  full2((N_BLK_C, blk, blk)), full2((1, d)),
                  full2((1, d)),
                  pl.BlockSpec((1, pad, d), lambda b, t: (b, 0, 0)),
                  pl.BlockSpec((1, 1, d), lambda b, t: (b, 0, 0))],
        out_specs=[pl.BlockSpec((1, rows, d), lambda b, t: (b, t, 0)),
                   pl.BlockSpec((1, 1, d), lambda b, t: (b, 0, 0))],
        out_shape=[jax.ShapeDtypeStruct((bsz, s, d), BF16),
                   jax.ShapeDtypeStruct((bsz, 1, d), F32)],
        scratch_shapes=[pltpu.VMEM((rows + pad, d), F32), pltpu.VMEM((1, d), F32)],
        compiler_params=_cparams(("arbitrary", "arbitrary")),
        name="rglru",
    )(xw, conv_w.astype(F32), row(conv_b), w_a.astype(BF16), row(b_a), w_x.astype(BF16), row(b_x),
      row(lam), cbuf8, h0)


def _route(x, wr_t, rbias):
    ne = wr_t.shape[0]
    tm = x.shape[0]
    gsz = ne // N_GROUPS
    logits = lax.dot_general(wr_t, x, _NT, preferred_element_type=F32)
    score = _sigmoid(logits)
    biased = score + rbias
    neg = -jnp.inf
    sub = lax.broadcasted_iota(jnp.int32, (gsz, tm), 0)
    gscore = []
    for g in range(N_GROUPS):
        vg = biased[g * gsz:(g + 1) * gsz]
        m1 = jnp.max(vg, axis=0, keepdims=True)
        first = jnp.min(jnp.where(vg == m1, sub, gsz), axis=0, keepdims=True)
        m2 = jnp.max(jnp.where(sub == first, neg, vg), axis=0, keepdims=True)
        gscore.append(m1 + m2)
    masked = []
    for g in range(N_GROUPS):
        rank = jnp.zeros((1, tm), jnp.int32)
        for o in range(N_GROUPS):
            if o == g:
                continue
            ahead = (gscore[o] >= gscore[g]) if o < g else (gscore[o] > gscore[g])
            rank = rank + ahead.astype(jnp.int32)
        keep = rank < TOPK_GROUPS
        masked.append(jnp.where(keep, biased[g * gsz:(g + 1) * gsz], neg))
    cur = jnp.concatenate(masked, axis=0)
    eidx = lax.broadcasted_iota(jnp.int32, (ne, tm), 0)
    chosen = jnp.zeros((ne, tm), F32)
    for _ in range(TOP_K):
        m = jnp.max(cur, axis=0, keepdims=True)
        first = jnp.min(jnp.where(cur == m, eidx, ne), axis=0, keepdims=True)
        hit = eidx == first
        chosen = jnp.where(hit, 1.0, chosen)
        cur = jnp.where(hit, neg, cur)
    wsel = chosen * score
    return wsel / jnp.sum(wsel, axis=0, keepdims=True) * ROUTED_SCALE


def _swiglu(h, f):
    hg = h[:, :f]
    return hg * _sigmoid(hg) * h[:, f:]


def _moe_kernel(inv_ref, invd_ref, x_ref, wr_ref, rb_ref, wsi_ref, wsd_ref, wi_ref, wd_ref, g_ref, b_ref,
                out_ref, xb_scr, x8_scr, acc_scr, gate_scr, *, eb, layer):
    j = pl.program_id(1)
    f = wsd_ref.shape[0]

    @pl.when(j == 0)
    def _():
        x = x_ref[...]
        xb = x.astype(BF16)
        xb_scr[...] = xb
        x8_scr[...] = x.astype(F8)
        w_et = _route(xb, wr_ref[...], rb_ref[...])
        ne, tm = w_et.shape
        gate_scr[...] = jnp.concatenate([w_et, jnp.zeros((gate_scr.shape[1] - ne, tm), F32)], axis=0).T
        sh = _swiglu(jnp.dot(xb, wsi_ref[...], preferred_element_type=F32), f)
        acc_scr[...] = jnp.dot(sh.astype(BF16), wsd_ref[...], preferred_element_type=F32)

    x8 = x8_scr[...]
    gates = gate_scr[...]
    lane = lax.broadcasted_iota(jnp.int32, gates.shape, 1)
    acts = []
    for i in range(eb):
        e = j * eb + i
        inv = inv_ref[layer * (eb * pl.num_programs(1)) + e]
        hs = jnp.dot(x8, wi_ref[i], preferred_element_type=F32)
        hg, hu = hs[:, :f], hs[:, f:]
        gcol = jnp.sum(jnp.where(lane == e, gates, 0.0), axis=1, keepdims=True)
        sig = 1.0 / (1.0 + jnp.exp2(hg * (-LOG2E * inv)))
        act = hg * sig * hu * (gcol * (ACT_SCALE * inv * inv))
        acts.append(jnp.clip(act, -F8_MAX, F8_MAX).astype(F8))
    acc_scr[...] += (jnp.dot(jnp.concatenate(acts, axis=1), wd_ref[0], preferred_element_type=F32)
                     * (invd_ref[layer * pl.num_programs(1) + j] * (1.0 / ACT_SCALE)))

    @pl.when(j == pl.num_programs(1) - 1)
    def _():
        out_ref[...] = _layer_norm(ALPHA * x_ref[...] + acc_scr[...], g_ref[...], b_ref[...])


def _fp8_prep_kernel(w_ref, o_ref, inv_ref):
    w = w_ref[0]
    amax = jnp.max(jnp.max(jnp.abs(w), axis=1, keepdims=True), axis=0, keepdims=True)
    k = jnp.floor(jnp.log2(F8_MAX / jnp.maximum(amax, 1e-30))) - 1.0
    scale = jnp.exp2(jnp.clip(k, -60.0, 60.0))
    o_ref[0] = (w * scale).astype(F8)
    inv_ref[0] = jnp.broadcast_to(1.0 / scale, inv_ref.shape[1:])


def _fp8_weights(w):
    n, r, c = w.shape
    w8, inv = pl.pallas_call(
        _fp8_prep_kernel,
        grid=(n,),
        in_specs=[pl.BlockSpec((1, r, c), lambda i: (i, 0, 0))],
        out_specs=[pl.BlockSpec((1, r, c), lambda i: (i, 0, 0)),
                   pl.BlockSpec((1, 8, 128), lambda i: (i, 0, 0))],
        out_shape=[jax.ShapeDtypeStruct((n, r, c), F8), jax.ShapeDtypeStruct((n, 8, 128), F32)],
        compiler_params=_cparams(("arbitrary",)),
        name="fp8_weights",
    )(w.astype(F32))
    return w8, inv[:, 0, 0]


def _moe_ln(x, w_router, router_bias, w8, inv_scale, wd8, inv_scale_d, ws_in, ws_down, g, b, *, tm, layer):
    t, d = x.shape
    ne = w_router.shape[1]
    f2 = w8.shape[2]
    f = f2 // 2
    eb = MOE_EB
    nb = ne // eb
    const = lambda shp: pl.BlockSpec(shp, lambda i, j: (0,) * len(shp))
    return pl.pallas_call(
        functools.partial(_moe_kernel, eb=eb, layer=layer),
        grid=(t // tm, nb),
        in_specs=[pl.BlockSpec(memory_space=pltpu.SMEM), pl.BlockSpec(memory_space=pltpu.SMEM),
                  pl.BlockSpec((tm, d), lambda i, j: (i, 0)),
                  const((ne, d)), const((ne, 1)),
                  const((d, f2)), const((f, d)),
                  pl.BlockSpec((eb, d, f2), lambda i, j: (layer * nb + j, 0, 0)),
                  pl.BlockSpec((1, eb * f, d), lambda i, j: (layer * nb + j, 0, 0)),
                  const((1, d)), const((1, d))],
        out_specs=pl.BlockSpec((tm, d), lambda i, j: (i, 0)),
        out_shape=jax.ShapeDtypeStruct((t, d), F32),
        scratch_shapes=[pltpu.VMEM((tm, d), BF16), pltpu.VMEM((tm, d), F8), pltpu.VMEM((tm, d), F32),
                        pltpu.VMEM((tm, 128), F32)],
        compiler_params=_cparams(("arbitrary", "arbitrary")),
        name="moe_ln",
    )(inv_scale, inv_scale_d, x, w_router.T.astype(BF16), router_bias.reshape(ne, 1).astype(F32),
      ws_in.astype(BF16), ws_down.astype(BF16), w8, wd8,
      g.reshape(1, d), b.reshape(1, d))


def _lambda_init(layer_idx):
    return 0.8 - 0.6 * math.exp(-0.3 * layer_idx)


def kernel(x_prompt, x_sample, state_hgrn_s, cache_k, cache_v, state_rglru_conv, state_rglru_h, hgrn_lb_logits, hgrn_w_in, hgrn_gn_g, hgrn_w_out, rel_bias, attn_w_qkv, attn_lam_q1, attn_lam_k1, attn_lam_q2, attn_lam_k2, attn_subln_g, attn_w_out, rglru_w_in, rglru_conv_w, rglru_conv_b, rglru_w_a, rglru_b_a, rglru_w_x, rglru_b_x, rglru_lam, rglru_w_out, ln1_g, ln1_b, ln2_g, ln2_b, moe_w_router, moe_router_bias, moe_w_in, moe_w_down, moe_ws_in, moe_ws_down):
    bp, sp, d = x_prompt.shape
    bs, ss, _ = x_sample.shape
    lb_p = jax.nn.softmax(hgrn_lb_logits.astype(F32), axis=0)
    lower_bounds = jnp.clip(jnp.cumsum(lb_p, axis=0) - lb_p[0], 0.0, 1.0)

    nl, ne, _, f2 = moe_w_in.shape
    w8, inv_in = _fp8_weights(moe_w_in.reshape(nl * ne, d, f2))
    wd8, inv_down = _fp8_weights(moe_w_down.reshape(nl * ne // MOE_EB, MOE_EB * (f2 // 2), d))

    streams = [(x_prompt.reshape(bp * sp, d), bp, sp), (x_sample.reshape(bs * ss, d), bs, ss)]
    tile = lambda t: 1024 if t % 1024 == 0 else 512
    hs, kvs, cvs, hls = [[], []], [[], []], [[], []], [[], []]
    for i in range(DEPTH):
        j, kind = i // N_MIXERS, i % N_MIXERS
        new_streams = []
        for si, (x, bsz, s) in enumerate(streams):
            t = bsz * s
            tm = tile(t)
            if kind == 0:
                xw = _matmul(x, hgrn_w_in[j].astype(BF16), tm=512)
                s0 = (jnp.zeros((bsz, H_A, d // H_A, d // H_A), F32) if si == 0
                      else state_hgrn_s[j].astype(F32))
                o, s_fin = _hgrn(xw.reshape(bsz, s, -1), s0, lower_bounds[i], hgrn_gn_g[j],
                                 rows=min(s, 256))
                hs[si].append(s_fin)
                w_out = hgrn_w_out[j]
            elif kind == 1:
                qkv = _matmul(x, attn_w_qkv[j].astype(BF16), tm=512)
                hd = H_B * 2 * DH_B
                lam_init = _lambda_init(i)
                lam = (jnp.exp(jnp.sum(attn_lam_q1[j].astype(F32) * attn_lam_k1[j].astype(F32)))
                       - jnp.exp(jnp.sum(attn_lam_q2[j].astype(F32) * attn_lam_k2[j].astype(F32)))
                       + lam_init).reshape(1)
                qkv3 = qkv.reshape(bsz, s, 3 * hd)
                if si == 0:
                    o = _attn_prompt(qkv3, rel_bias, lam, attn_subln_g[j], lam_init)
                else:
                    past = cache_k.shape[2]
                    o = _attn_sample(qkv3, cache_k.reshape(-1, past * H_B, 2 * DH_B),
                                     cache_v.reshape(-1, past * H_B, 2 * DH_B),
                                     j, rel_bias, lam, attn_subln_g[j], lam_init)
                kvs[si].append((qkv3[:, :, hd:2 * hd].reshape(bsz, s, H_B, 2 * DH_B),
                                qkv3[:, :, 2 * hd:].reshape(bsz, s, H_B, 2 * DH_B)))
                w_out = attn_w_out[j]
            else:
                xw = _matmul(x, rglru_w_in[j].astype(BF16), tm=512).reshape(bsz, s, 2 * d)
                if si == 0:
                    cbuf = jnp.zeros((bsz, CONV_W - 1, d), F32)
                    h0 = jnp.zeros((bsz, d), F32)
                else:
                    cbuf, h0 = state_rglru_conv[j].astype(F32), state_rglru_h[j].astype(F32)
                cbuf8 = jnp.pad(cbuf, ((0, 0), (8 - (CONV_W - 1), 0), (0, 0)))
                o, hlast = _rglru(xw, cbuf8, h0.reshape(bsz, 1, d), rglru_conv_w[j], rglru_conv_b[j],
                                  rglru_w_a[j], rglru_b_a[j], rglru_w_x[j], rglru_b_x[j], rglru_lam[j],
                                  rows=min(s, 256))
                cvs[si].append(xw[:, s - (CONV_W - 1):, d:])
                hls[si].append(hlast.reshape(bsz, d))
                w_out = rglru_w_out[j]
            x = _proj_ln(o.reshape(t, d), w_out.astype(BF16), x, ln1_g[i], ln1_b[i], tm=tm)
            x = _moe_ln(x, moe_w_router[i], moe_router_bias[i], w8, inv_in, wd8, inv_down,
                        moe_ws_in[i], moe_ws_down[i], ln2_g[i], ln2_b[i], tm=tm, layer=i)
            new_streams.append((x, bsz, s))
        streams = new_streams

    yp = streams[0][0].reshape(bp, sp, d)
    ys = streams[1][0].reshape(bs, ss, d)
    st = lambda xs: jnp.stack(xs)
    return (yp, ys, st(hs[0]), st(hs[1]),
            st([kv[0] for kv in kvs[0]]), st([kv[1] for kv in kvs[0]]),
            st([kv[0] for kv in kvs[1]]), st([kv[1] for kv in kvs[1]]),
            st(cvs[0]), st(hls[0]), st(cvs[1]), st(hls[1]))
```

```python
import functools
import math

import jax
import jax.numpy as jnp
from jax import lax
from jax.experimental import pallas as pl
from jax.experimental.pallas import tpu as pltpu

F32 = jnp.float32
BF16 = jnp.bfloat16
F8 = jnp.float8_e4m3fn
F8_MAX = 448.0
ACT_SCALE = 32.0

DEPTH = 4
CHUNK = 64
N_MIXERS = 3
H_A = 8
H_B = 8
DH_B = 64
REL_BUCKETS = 32
REL_MAX_DIST = 1024
N_BLK_C = 4
CONV_W = 4
C_RG = 8.0
N_EXPERTS = 64
D_EXPERT = 128
TOP_K = 8
N_GROUPS = 8
TOPK_GROUPS = 4
ROUTED_SCALE = 2.5
ALPHA = (2 * DEPTH) ** 0.25
LN_EPS = 1e-5
RMS_EPS = 1e-6

VMEM_LIMIT = 56 * 1024 * 1024
LOG2E = math.log2(math.e)
REC_C = 64
ATT_T = 256
ATT_G = 4
ONES_ROWS = 16
MOE_EB = 16
FP8_PREP_BYTES = 8 * 1024 * 1024

_NT = (((1,), (1,)), ((), ()))
_TN = (((0,), (0,)), ((), ()))


def _cparams(sem):
    return pltpu.CompilerParams(dimension_semantics=sem, vmem_limit_bytes=VMEM_LIMIT)


def _sigmoid(x):
    return 1.0 / (1.0 + jnp.exp(-x))


def _layer_norm(z, g, b):
    mu = jnp.mean(z, -1, keepdims=True)
    zc = z - mu
    var = jnp.mean(zc * zc, -1, keepdims=True)
    return zc * lax.rsqrt(var + LN_EPS) * g + b


def _mm_kernel(x_ref, w_ref, o_ref):
    o_ref[...] = jnp.dot(x_ref[...].astype(BF16), w_ref[...],
                         preferred_element_type=F32).astype(o_ref.dtype)


def _matmul(x, w, *, tm, out_dtype=F32):
    t, k = x.shape
    n = w.shape[1]
    return pl.pallas_call(
        _mm_kernel,
        grid=(t // tm,),
        in_specs=[pl.BlockSpec((tm, k), lambda i: (i, 0)),
                  pl.BlockSpec((k, n), lambda i: (0, 0))],
        out_specs=pl.BlockSpec((tm, n), lambda i: (i, 0)),
        out_shape=jax.ShapeDtypeStruct((t, n), out_dtype),
        compiler_params=_cparams(("arbitrary",)),
        name="proj_in",
    )(x, w)


def _hgrn_kernel(q_ref, fz_ref, v_ref, g_ref, lb_ref, gn_ref, s0_ref, o_ref, sfin_ref, st_scr,
                 *, chunk):
    c = pl.program_id(1)
    nh, dv, dk = st_scr.shape

    @pl.when(c == 0)
    def _():
        for h in range(nh):
            st_scr[h] = s0_ref[0, h].T

    rows = fz_ref.shape[1]
    lb = lb_ref[...]
    causal = (lax.broadcasted_iota(jnp.int32, (chunk, chunk), 0)
              >= lax.broadcasted_iota(jnp.int32, (chunk, chunk), 1))
    tri = jnp.where(causal, 1.0, 0.0).astype(BF16)
    sts = [st_scr[h] for h in range(nh)]
    for i in range(rows // chunk):
        sl = slice(i * chunk, (i + 1) * chunk)
        fz = fz_ref[0, sl, :]
        e = jnp.exp(-jnp.abs(fz))
        r = 1.0 / (1.0 + e)
        pos = fz >= 0
        sig = jnp.where(pos, r, e * r)
        nsig = jnp.where(pos, e * r, r)
        logf = jnp.log(lb + (1.0 - lb) * sig)
        kk = (1.0 - lb) * nsig
        hi = logf.astype(BF16)
        r1 = logf - hi.astype(F32)
        mid = r1.astype(BF16)
        lo = (r1 - mid.astype(F32)).astype(BF16)
        b = (jnp.dot(tri, hi, preferred_element_type=F32)
             + jnp.dot(tri, mid, preferred_element_type=F32)
             + jnp.dot(tri, lo, preferred_element_type=F32))
        bm = b[chunk // 2 - 1:chunk // 2]
        be = b[chunk - 1:chunk]
        qv = q_ref[0, sl, :]
        a_fac = qv * _sigmoid(qv) * jnp.exp(b - bm)
        k_fac = kk * jnp.exp(bm - b)
        qe = (a_fac * jnp.exp(bm)).astype(BF16)
        kd = (k_fac * jnp.exp(be - bm)).astype(BF16)
        a16 = a_fac.astype(BF16)
        k16 = k_fac.astype(BF16)
        dec = jnp.exp(be)
        v = v_ref[0, sl, :].astype(BF16)
        gv = g_ref[0, sl, :]
        gate = gv * _sigmoid(gv)
        for h in range(nh):
            hs = slice(h * dk, (h + 1) * dk)
            st = sts[h]
            o = lax.dot_general(qe[:, hs], st.astype(BF16), _NT, preferred_element_type=F32)
            sc = lax.dot_general(a16[:, hs], k16[:, hs], _NT, preferred_element_type=F32)
            sc = jnp.where(causal, sc, 0.0)
            o = o + jnp.dot(sc.astype(BF16), v[:, hs], preferred_element_type=F32)
            sts[h] = st * dec[:, hs] + lax.dot_general(v[:, hs], kd[:, hs], _TN,
                                                       preferred_element_type=F32)
            o = o * lax.rsqrt(jnp.mean(o * o, -1, keepdims=True) + RMS_EPS) * gn_ref[...] * gate[:, hs]
            o_ref[0, sl, hs] = o.astype(o_ref.dtype)
    for h in range(nh):
        st_scr[h] = sts[h]

    @pl.when(c == pl.num_programs(1) - 1)
    def _():
        for h in range(nh):
            sfin_ref[0, h] = st_scr[h].T


def _hgrn(xw, s0, lb, gn_g, *, rows):
    bsz, s, _ = xw.shape
    h, dk, dv = s0.shape[1:]
    sec = lambda k: pl.BlockSpec((1, rows, h * dk), lambda b, c, k=k: (b, c, k))
    return pl.pallas_call(
        functools.partial(_hgrn_kernel, chunk=REC_C),
        grid=(bsz, s // rows),
        in_specs=[sec(0), sec(1), sec(2), sec(3),
                  pl.BlockSpec((1, h * dk), lambda b, c: (0, 0)),
                  pl.BlockSpec((1, dv), lambda b, c: (0, 0)),
                  pl.BlockSpec((1, h, dk, dv), lambda b, c: (b, 0, 0, 0))],
        out_specs=[pl.BlockSpec((1, rows, h * dv), lambda b, c: (b, c, 0)),
                   pl.BlockSpec((1, h, dk, dv), lambda b, c: (b, 0, 0, 0))],
        out_shape=[jax.ShapeDtypeStruct((bsz, s, h * dv), BF16),
                   jax.ShapeDtypeStruct((bsz, h, dk, dv), F32)],
        scratch_shapes=[pltpu.VMEM((h, dv, dk), F32)],
        compiler_params=_cparams(("arbitrary", "arbitrary")),
        name="hgrn2",
    )(xw, xw, xw, xw, lb.reshape(1, h * dk), gn_g.reshape(1, dv), s0)


def _rel_bucket(rel):
    nb = REL_BUCKETS // 2
    max_exact = nb // 2
    ret = jnp.where(rel > 0, nb, 0)
    n = jnp.abs(rel)
    nf = jnp.maximum(n, 1).astype(F32)
    large = max_exact + (jnp.log(nf / max_exact) / math.log(REL_MAX_DIST / max_exact)
                         * (nb - max_exact)).astype(jnp.int32)
    large = jnp.minimum(large, nb - 1)
    return ret + jnp.where(n < max_exact, n, large)


def _saturation_distance():
    nb = REL_BUCKETS // 2
    max_exact = nb // 2
    return math.ceil(max_exact * (REL_MAX_DIST / max_exact) ** ((nb - 1 - max_exact) / (nb - max_exact))) + 1


def _bias_rows(rel_bias, rel0s, width):
    rel = jnp.asarray(rel0s, jnp.int32)[:, None] - jnp.arange(width, dtype=jnp.int32)[None, :]
    seg = rel_bias.astype(F32)[_rel_bucket(rel)] * LOG2E
    return jnp.transpose(seg, (2, 0, 1))[:, :, None, :]


def _far_bias(rel_bias):
    return rel_bias.astype(F32)[_rel_bucket(jnp.int32(-_saturation_distance()))] * LOG2E


def _bias_tile_t(seg, tk, shift):
    w = seg.shape[1]
    return pltpu.roll(jnp.broadcast_to(seg, (tk, w)), w - tk + shift, 1, stride=1, stride_axis=0)


def _stack_q(q):
    q = q * (DH_B ** -0.5 * LOG2E)
    lane = lax.broadcasted_iota(jnp.int32, q.shape, 1)
    qs = jnp.concatenate([jnp.where(lane < DH_B, q, 0.0), jnp.where(lane >= DH_B, q, 0.0)], axis=0)
    return qs.T.astype(BF16)


def _online(st, m, shift=None):
    if shift is None:
        m_new = jnp.maximum(m, jnp.max(st, axis=0, keepdims=True))
        p = jnp.exp2(st - m_new)
    else:
        m_raw = jnp.maximum(m - shift, jnp.max(st, axis=0, keepdims=True))
        p = jnp.exp2(st - m_raw)
        m_new = m_raw + shift
    return m_new, jnp.exp2(m - m_new), p.astype(BF16)


def _with_ones(vt):
    return jnp.concatenate([vt, jnp.ones((ONES_ROWS, vt.shape[1]), vt.dtype)], axis=0)


def _attn_out(ot, g_ref, out_scale):
    return ot * lax.rsqrt(jnp.mean(ot * ot, -1, keepdims=True) + RMS_EPS) * g_ref[...] * out_scale


def _attn_prompt_kernel(lam_ref, fb_ref, q_ref, k_ref, v_ref, seg_ref, g_ref, o_ref,
                        kb_scr, vt_scr, qst_scr, acc_scr, *, d_far, out_scale):
    hg = pl.program_id(1)
    qi = pl.program_id(2)
    ng, n, t, dv = kb_scr.shape
    w2 = 2 * t

    @pl.when(qi == 0)
    def _():
        for u in range(ng):
            for c in range(n):
                kb_scr[u, c] = k_ref[0, c * t:(c + 1) * t, u * dv:(u + 1) * dv].astype(BF16)
                vt_scr[u, c] = _with_ones(v_ref[0, c * t:(c + 1) * t, u * dv:(u + 1) * dv].T.astype(BF16))

    for u in range(ng):
        qst_scr[u] = _stack_q(q_ref[0, :, u * dv:(u + 1) * dv])
    acc_scr[...] = jnp.zeros(acc_scr.shape, F32)
    far_shift = jnp.concatenate([jnp.full((1, w2), fb_ref[hg * ng + u], F32) for u in range(ng)], axis=1)

    def scores(ki, u):
        return jnp.dot(kb_scr[u, ki], qst_scr[u], preferred_element_type=F32)

    def biased(st, seg, vis=None):
        bt = _bias_tile_t(seg, t, 0)[:, :t]
        s1, s2 = st[:, :t] + bt, st[:, t:] + bt
        if vis is not None:
            s1, s2 = jnp.where(vis, s1, -1e30), jnp.where(vis, s2, -1e30)
        return jnp.concatenate([s1, s2], axis=1)

    def update(ki, m, sts, shift=None):
        m, alpha, p = _online(jnp.concatenate(sts, axis=1), m, shift)
        for u in range(ng):
            sl = slice(u * w2, (u + 1) * w2)
            acc_scr[u] = alpha[:, sl] * acc_scr[u] + jnp.dot(vt_scr[u, ki], p[:, sl],
                                                             preferred_element_type=F32)
        return m

    def far_scores(ki):
        return [scores(ki, u) for u in range(ng)]

    def near_scores(ki):
        return [biased(scores(ki, u), seg_ref[u, qi - ki]) for u in range(ng)]

    def tiles(lo, hi, tile_scores, shift, m):
        def two(i, m):
            ka = lo + 2 * i
            sa, sb = tile_scores(ka), tile_scores(ka + 1)
            return update(ka + 1, update(ka, m, sa, shift), sb, shift)
        n2 = (hi - lo) // 2
        m = lax.fori_loop(0, n2, two, m)
        return lax.fori_loop(lo + 2 * n2, hi, lambda ki, m: update(ki, m, tile_scores(ki), shift), m)

    n_far = jnp.maximum(qi - (d_far - 1), 0)
    m = tiles(0, n_far, far_scores, far_shift, jnp.full((1, ng * w2), -1e30, F32))
    m = tiles(n_far, qi, near_scores, None, m)
    vis = (lax.broadcasted_iota(jnp.int32, (t, t), 0) // CHUNK
           <= lax.broadcasted_iota(jnp.int32, (t, t), 1) // CHUNK)
    update(qi, m, [biased(scores(qi, u), seg_ref[u, 0], vis) for u in range(ng)])
    for u in range(ng):
        acc = acc_scr[u]
        on = acc[:dv] * (1.0 / acc[dv:dv + 1])
        ot = (on[:, :t] - lam_ref[0] * on[:, t:]).T
        o_ref[0, :, u * dv:(u + 1) * dv] = _attn_out(ot, g_ref, out_scale).astype(o_ref.dtype)


def _attn_prompt(qkv, rel_bias, lam, subln_g, lam_init):
    bsz, s, _ = qkv.shape
    h, dv, t, g = H_B, 2 * DH_B, ATT_T, ATT_G
    n = s // t
    w = 2 * t
    d_far = -(-(_saturation_distance() - 1) // t) + 1
    segs = _bias_rows(rel_bias, [t - d * t for d in range(n)], w)
    kv = lambda sec: pl.BlockSpec((1, s, g * dv), lambda b, hg, qi, sec=sec: (b, 0, sec * (h // g) + hg))
    return pl.pallas_call(
        functools.partial(_attn_prompt_kernel, d_far=d_far, out_scale=1.0 - lam_init),
        grid=(bsz, h // g, n),
        in_specs=[pl.BlockSpec(memory_space=pltpu.SMEM), pl.BlockSpec(memory_space=pltpu.SMEM),
                  pl.BlockSpec((1, t, g * dv), lambda b, hg, qi: (b, qi, hg)),
                  kv(1), kv(2),
                  pl.BlockSpec((g, n, 1, w), lambda b, hg, qi: (hg, 0, 0, 0)),
                  pl.BlockSpec((1, dv), lambda b, hg, qi: (0, 0))],
        out_specs=pl.BlockSpec((1, t, g * dv), lambda b, hg, qi: (b, qi, hg)),
        out_shape=jax.ShapeDtypeStruct((bsz, s, h * dv), BF16),
        scratch_shapes=[pltpu.VMEM((g, n, t, dv), BF16), pltpu.VMEM((g, n, dv + ONES_ROWS, t), BF16),
                        pltpu.VMEM((g, dv, 2 * t), BF16), pltpu.VMEM((g, dv + ONES_ROWS, 2 * t), F32)],
        compiler_params=_cparams(("arbitrary",) * 3),
        name="diff_attn_prompt",
    )(lam, _far_bias(rel_bias), qkv, qkv, qkv, segs, subln_g.reshape(1, dv))


def _attn_sample_kernel(lam_ref, fb_ref, q_ref, kp_ref, vp_ref, segp_ref, kn_ref, vn_ref, segn_ref,
                        g_ref, o_ref, *, tkp, n_past, far_tiles, out_scale):
    hg = pl.program_id(1)
    tq = q_ref.shape[1]
    dv = g_ref.shape[1]
    ng = q_ref.shape[2] // dv
    nh = kp_ref.shape[1] // (n_past * tkp)
    w2 = 2 * tq
    qst = [_stack_q(q_ref[0, :, u * dv:(u + 1) * dv]) for u in range(ng)]
    lane = lax.broadcasted_iota(jnp.int32, (1, w2), 1)

    def bias(seg, tk):
        return jnp.where(lane < tq, _bias_tile_t(seg, tk, 0)[:, :w2], _bias_tile_t(seg, tk, tq)[:, :w2])

    m = jnp.full((1, ng * w2), -1e30, F32)
    far_shift = jnp.concatenate([jnp.full((1, w2), fb_ref[hg * ng + u], F32) for u in range(ng)], axis=1)
    acc = [jnp.zeros((dv + ONES_ROWS, w2), F32) for _ in range(ng)]
    for c in range(n_past + 1):
        sts, vts = [], []
        for u in range(ng):
            if c < n_past:
                rows = pl.ds(c * tkp * nh + hg * ng + u, tkp, stride=nh)
                k, v = kp_ref[0, rows, :], vp_ref[0, rows, :]
            else:
                k, v = kn_ref[0, :, u * dv:(u + 1) * dv], vn_ref[0, :, u * dv:(u + 1) * dv]
            st = jnp.dot(k.astype(BF16), qst[u], preferred_element_type=F32)
            if far_tiles <= c < n_past:
                st = st + bias(segp_ref[u, c - far_tiles], tkp)
            elif c == n_past:
                st = st + bias(segn_ref[u, 0], tq)
            sts.append(st)
            vts.append(_with_ones(v.T.astype(BF16)))
        m, alpha, p = _online(jnp.concatenate(sts, axis=1), m, far_shift if c < far_tiles else None)
        for u in range(ng):
            sl = slice(u * w2, (u + 1) * w2)
            acc[u] = alpha[:, sl] * acc[u] + jnp.dot(vts[u], p[:, sl], preferred_element_type=F32)
    for u in range(ng):
        on = acc[u][:dv] * (1.0 / acc[u][dv:dv + 1])
        ot = (on - lam_ref[0] * pltpu.roll(on, tq, 1)).T[:tq]
        o_ref[0, :, u * dv:(u + 1) * dv] = _attn_out(ot, g_ref, out_scale).astype(o_ref.dtype)


def _attn_sample(qkv, k_past, v_past, layer, rel_bias, lam, subln_g, lam_init):
    bsz, s, _ = qkv.shape
    h, dv, tkp, g = H_B, 2 * DH_B, ATT_T, ATT_G
    past = k_past.shape[1] // h
    n_past = past // tkp
    far_tiles = sum(1 for c in range(n_past) if past - (c * tkp + tkp - 1) >= _saturation_distance())
    near = list(range(far_tiles, n_past)) or [n_past - 1]
    segp = _bias_rows(rel_bias, [c * tkp - past + tkp for c in near], 2 * s + tkp)
    segn = _bias_rows(rel_bias, [s], 4 * s)
    pk = lambda: pl.BlockSpec((1, past * h, dv), lambda b, hg: (layer * bsz + b, 0, 0))
    nk = lambda sec: pl.BlockSpec((1, s, g * dv), lambda b, hg, sec=sec: (b, 0, sec * (h // g) + hg))
    return pl.pallas_call(
        functools.partial(_attn_sample_kernel, tkp=tkp, n_past=n_past, far_tiles=far_tiles,
                          out_scale=1.0 - lam_init),
        grid=(bsz, h // g),
        in_specs=[pl.BlockSpec(memory_space=pltpu.SMEM), pl.BlockSpec(memory_space=pltpu.SMEM),
                  nk(0), pk(), pk(),
                  pl.BlockSpec((g, len(near), 1, 2 * s + tkp), lambda b, hg: (hg, 0, 0, 0)),
                  nk(1), nk(2),
                  pl.BlockSpec((g, 1, 1, 4 * s), lambda b, hg: (hg, 0, 0, 0)),
                  pl.BlockSpec((1, dv), lambda b, hg: (0, 0))],
        out_specs=pl.BlockSpec((1, s, g * dv), lambda b, hg: (b, 0, hg)),
        out_shape=jax.ShapeDtypeStruct((bsz, s, h * dv), BF16),
        compiler_params=_cparams(("arbitrary",) * 2),
        name="diff_attn_sample",
    )(lam, _far_bias(rel_bias), qkv, k_past, v_past, segp, qkv, qkv, segn, subln_g.reshape(1, dv))


def _rglru_kernel(xw_ref, cw_ref, cb_ref, wa_ref, ba_ref, wx_ref, bx_ref, lam_ref, cbuf_ref, h0_ref,
                  y_ref, hlast_ref, ext_scr, h_scr):
    t = pl.program_id(1)
    tt = y_ref.shape[1]
    d = y_ref.shape[2]
    pad = ext_scr.shape[0] - tt

    @pl.when(t == 0)
    def _():
        ext_scr[0:pad] = cbuf_ref[0]
        h_scr[...] = h0_ref[0]

    gate = xw_ref[0, :, 0:d]
    u = xw_ref[0, :, d:2 * d]
    ext_scr[pad:pad + tt] = u
    conv = cb_ref[...] + cw_ref[CONV_W - 1:CONV_W] * u
    for j in range(1, CONV_W):
        conv = conv + cw_ref[CONV_W - 1 - j:CONV_W - j] * ext_scr[pad - j:pad - j + tt]
    ext_scr[0:pad] = ext_scr[tt:tt + pad]

    cb16 = conv.astype(BF16)
    blk = d // N_BLK_C
    ra, rx = [], []
    for n in range(N_BLK_C):
        cs = cb16[:, n * blk:(n + 1) * blk]
        ra.append(jnp.dot(cs, wa_ref[n], preferred_element_type=F32))
        rx.append(jnp.dot(cs, wx_ref[n], preferred_element_type=F32))
    r = _sigmoid(jnp.concatenate(ra, axis=1) + ba_ref[...])
    ig = _sigmoid(jnp.concatenate(rx, axis=1) + bx_ref[...])
    nl = -lam_ref[...]
    sp = jnp.maximum(nl, 0.0) + jnp.log(1.0 + jnp.exp(-jnp.abs(nl)))
    log_a = (-C_RG) * r * sp
    a = jnp.exp(log_a)
    xin = jnp.sqrt(1.0 - a * a) * ig * conv

    row = lax.broadcasted_iota(jnp.int32, (tt, d), 0)
    ca, cx = a, xin
    sft = 1
    while sft < tt:
        keep = row >= sft
        pa = jnp.where(keep, pltpu.roll(ca, sft, 0), 1.0)
        px = jnp.where(keep, pltpu.roll(cx, sft, 0), 0.0)
        cx = ca * px + cx
        ca = ca * pa
        sft *= 2
    hh = ca * h_scr[...] + cx
    h_scr[...] = hh[tt - 1:tt]
    gelu = 0.5 * gate * (1.0 + jnp.tanh(math.sqrt(2.0 / math.pi) * (gate + 0.044715 * gate * gate * gate)))
    y_ref[0] = (hh * gelu).astype(y_ref.dtype)

    @pl.when(t == pl.num_programs(1) - 1)
    def _():
        hlast_ref[0] = hh[tt - 1:tt]


def _rglru(xw, cbuf8, h0, conv_w, conv_b, w_a, b_a, w_x, b_x, lam, *, rows):
    bsz, s, d2 = xw.shape
    d = d2 // 2
    pad = cbuf8.shape[1]
    blk = d // N_BLK_C
    row = lambda a: a.reshape(1, d).astype(F32)
    full2 = lambda shp: pl.BlockSpec(shp, lambda b, t: (0,) * len(shp))
    return pl.pallas_call(
        _rglru_kernel,
        grid=(bsz, s // rows),
        in_specs=[pl.BlockSpec((1, rows, d2), lambda b, t: (b, t, 0)),
                  full2((CONV_W, d)), full2((1, d)),
                  full2((N_BLK_C, blk, blk)), full2((1, d)),
                  full2((N_BLK_C, blk, blk)), full2((1, d)),
                  full2((1, d)),
                  pl.BlockSpec((1, pad, d), lambda b, t: (b, 0, 0)),
                  pl.BlockSpec((1, 1, d), lambda b, t: (b, 0, 0))],
        out_specs=[pl.BlockSpec((1, rows, d), lambda b, t: (b, t, 0)),
                   pl.BlockSpec((1, 1, d), lambda b, t: (b, 0, 0))],
        out_shape=[jax.ShapeDtypeStruct((bsz, s, d), BF16),
                   jax.ShapeDtypeStruct((bsz, 1, d), F32)],
        scratch_shapes=[pltpu.VMEM((rows + pad, d), F32), pltpu.VMEM((1, d), F32)],
        compiler_params=_cparams(("arbitrary", "arbitrary")),
        name="rglru",
    )(xw, conv_w.astype(F32), row(conv_b), w_a.astype(BF16), row(b_a), w_x.astype(BF16), row(b_x),
      row(lam), cbuf8, h0)


def _route(x, wr_t, rbias):
    ne = wr_t.shape[0]
    tm = x.shape[0]
    gsz = ne // N_GROUPS
    logits = lax.dot_general(wr_t, x, _NT, preferred_element_type=F32)
    score = _sigmoid(logits)
    biased = score + rbias
    neg = -jnp.inf
    sub = lax.broadcasted_iota(jnp.int32, (gsz, tm), 0)
    gscore = []
    for g in range(N_GROUPS):
        vg = biased[g * gsz:(g + 1) * gsz]
        m1 = jnp.max(vg, axis=0, keepdims=True)
        first = jnp.min(jnp.where(vg == m1, sub, gsz), axis=0, keepdims=True)
        m2 = jnp.max(jnp.where(sub == first, neg, vg), axis=0, keepdims=True)
        gscore.append(m1 + m2)
    masked = []
    for g in range(N_GROUPS):
        rank = jnp.zeros((1, tm), jnp.int32)
        for o in range(N_GROUPS):
            if o == g:
                continue
            ahead = (gscore[o] >= gscore[g]) if o < g else (gscore[o] > gscore[g])
            rank = rank + ahead.astype(jnp.int32)
        keep = rank < TOPK_GROUPS
        masked.append(jnp.where(keep, biased[g * gsz:(g + 1) * gsz], neg))
    cur = jnp.concatenate(masked, axis=0)
    eidx = lax.broadcasted_iota(jnp.int32, (ne, tm), 0)
    chosen = jnp.zeros((ne, tm), F32)
    for _ in range(TOP_K):
        m = jnp.max(cur, axis=0, keepdims=True)
        first = jnp.min(jnp.where(cur == m, eidx, ne), axis=0, keepdims=True)
        hit = eidx == first
        chosen = jnp.where(hit, 1.0, chosen)
        cur = jnp.where(hit, neg, cur)
    wsel = chosen * score
    return wsel / jnp.sum(wsel, axis=0, keepdims=True) * ROUTED_SCALE


def _swiglu(h, f):
    hg = h[:, :f]
    return hg * _sigmoid(hg) * h[:, f:]


def _moe_kernel(inv_ref, invd_ref, o_ref, wo_ref, x0_ref, g1_ref, b1_ref, wr_ref, rb_ref, wsi_ref, wsd_ref,
                wi_ref, wd_ref, g_ref, b_ref, out_ref, x_scr, xb_scr, x8_scr, acc_scr, gate_scr, *, eb, layer):
    j = pl.program_id(1)
    f = wsd_ref.shape[0]

    @pl.when(j == 0)
    def _():
        x = _layer_norm(ALPHA * x0_ref[...] + jnp.dot(o_ref[...], wo_ref[...], preferred_element_type=F32),
                        g1_ref[...], b1_ref[...])
        x_scr[...] = x
        xb = x.astype(BF16)
        xb_scr[...] = xb
        x8_scr[...] = x.astype(F8)
        w_et = _route(xb, wr_ref[...], rb_ref[...])
        ne, tm = w_et.shape
        gate_scr[...] = jnp.concatenate([w_et, jnp.zeros((gate_scr.shape[1] - ne, tm), F32)], axis=0).T
        sh = _swiglu(jnp.dot(xb, wsi_ref[...], preferred_element_type=F32), f)
        acc_scr[...] = jnp.dot(sh.astype(BF16), wsd_ref[...], preferred_element_type=F32)

    x8 = x8_scr[...]
    gates = gate_scr[...]
    lane = lax.broadcasted_iota(jnp.int32, gates.shape, 1)
    acts = []
    for i in range(eb):
        e = j * eb + i
        inv = inv_ref[layer * (eb * pl.num_programs(1)) + e]
        hs = jnp.dot(x8, wi_ref[i], preferred_element_type=F32)
        hg, hu = hs[:, :f], hs[:, f:]
        gcol = jnp.sum(jnp.where(lane == e, gates, 0.0), axis=1, keepdims=True)
        sig = 1.0 / (1.0 + jnp.exp2(hg * (-LOG2E * inv)))
        act = hg * sig * hu * (gcol * (ACT_SCALE * inv * inv))
        acts.append(jnp.clip(act, -F8_MAX, F8_MAX).astype(F8))
    acc_scr[...] += (jnp.dot(jnp.concatenate(acts, axis=1), wd_ref[0], preferred_element_type=F32)
                     * (invd_ref[layer * pl.num_programs(1) + j] * (1.0 / ACT_SCALE)))

    @pl.when(j == pl.num_programs(1) - 1)
    def _():
        out_ref[...] = _layer_norm(ALPHA * x_scr[...] + acc_scr[...], g_ref[...], b_ref[...])


def _fp8_prep_kernel(w_ref, o_ref, inv_ref):
    for i in range(w_ref.shape[0]):
        w = w_ref[i]
        amax = jnp.max(jnp.max(jnp.abs(w), axis=1, keepdims=True), axis=0, keepdims=True)
        k = jnp.floor(jnp.log2(F8_MAX / jnp.maximum(amax, 1e-30))) - 1.0
        scale = jnp.exp2(jnp.clip(k, -60.0, 60.0))
        o_ref[i] = (w * scale).astype(F8)
        inv_ref[i] = jnp.broadcast_to(1.0 / scale, inv_ref.shape[1:])


def _fp8_weights(w):
    n, r, c = w.shape
    per = max(1, min(n, FP8_PREP_BYTES // (r * c * 4)))
    w8, inv = pl.pallas_call(
        _fp8_prep_kernel,
        grid=(n // per,),
        in_specs=[pl.BlockSpec((per, r, c), lambda i: (i, 0, 0))],
        out_specs=[pl.BlockSpec((per, r, c), lambda i: (i, 0, 0)),
                   pl.BlockSpec((per, 8, 128), lambda i: (i, 0, 0))],
        out_shape=[jax.ShapeDtypeStruct((n, r, c), F8), jax.ShapeDtypeStruct((n, 8, 128), F32)],
        compiler_params=_cparams(("arbitrary",)),
        name="fp8_weights",
    )(w.astype(F32))
    return w8, inv[:, 0, 0]


def _mix_out_moe(o, w_out, x0, g1, b1, w_router, router_bias, w8, inv_scale, wd8, inv_scale_d, ws_in, ws_down,
                 g, b, *, tm, layer):
    t, d = x0.shape
    ne = w_router.shape[1]
    f2 = w8.shape[2]
    f = f2 // 2
    eb = MOE_EB
    nb = ne // eb
    const = lambda shp: pl.BlockSpec(shp, lambda i, j: (0,) * len(shp))
    return pl.pallas_call(
        functools.partial(_moe_kernel, eb=eb, layer=layer),
        grid=(t // tm, nb),
        in_specs=[pl.BlockSpec(memory_space=pltpu.SMEM), pl.BlockSpec(memory_space=pltpu.SMEM),
                  pl.BlockSpec((tm, o.shape[1]), lambda i, j: (i, 0)), const(w_out.shape),
                  pl.BlockSpec((tm, d), lambda i, j: (i, 0)), const((1, d)), const((1, d)),
                  const((ne, d)), const((ne, 1)),
                  const((d, f2)), const((f, d)),
                  pl.BlockSpec((eb, d, f2), lambda i, j: (layer * nb + j, 0, 0)),
                  pl.BlockSpec((1, eb * f, d), lambda i, j: (layer * nb + j, 0, 0)),
                  const((1, d)), const((1, d))],
        out_specs=pl.BlockSpec((tm, d), lambda i, j: (i, 0)),
        out_shape=jax.ShapeDtypeStruct((t, d), F32),
        scratch_shapes=[pltpu.VMEM((tm, d), F32), pltpu.VMEM((tm, d), BF16), pltpu.VMEM((tm, d), F8),
                        pltpu.VMEM((tm, d), F32), pltpu.VMEM((tm, 128), F32)],
        compiler_params=_cparams(("arbitrary", "arbitrary")),
        name="moe_ln",
    )(inv_scale, inv_scale_d, o, w_out, x0, g1.reshape(1, d), b1.reshape(1, d),
      w_router.T.astype(BF16), router_bias.reshape(ne, 1).astype(F32),
      ws_in.astype(BF16), ws_down.astype(BF16), w8, wd8,
      g.reshape(1, d), b.reshape(1, d))


def _lambda_init(layer_idx):
    return 0.8 - 0.6 * math.exp(-0.3 * layer_idx)


def kernel(x_prompt, x_sample, state_hgrn_s, cache_k, cache_v, state_rglru_conv, state_rglru_h, hgrn_lb_logits, hgrn_w_in, hgrn_gn_g, hgrn_w_out, rel_bias, attn_w_qkv, attn_lam_q1, attn_lam_k1, attn_lam_q2, attn_lam_k2, attn_subln_g, attn_w_out, rglru_w_in, rglru_conv_w, rglru_conv_b, rglru_w_a, rglru_b_a, rglru_w_x, rglru_b_x, rglru_lam, rglru_w_out, ln1_g, ln1_b, ln2_g, ln2_b, moe_w_router, moe_router_bias, moe_w_in, moe_w_down, moe_ws_in, moe_ws_down):
    bp, sp, d = x_prompt.shape
    bs, ss, _ = x_sample.shape
    lb_p = jax.nn.softmax(hgrn_lb_logits.astype(F32), axis=0)
    lower_bounds = jnp.clip(jnp.cumsum(lb_p, axis=0) - lb_p[0], 0.0, 1.0)

    nl, ne, _, f2 = moe_w_in.shape
    w8, inv_in = _fp8_weights(moe_w_in.reshape(nl * ne, d, f2))
    wd8, inv_down = _fp8_weights(moe_w_down.reshape(nl * ne // MOE_EB, MOE_EB * (f2 // 2), d))

    streams = [(x_prompt.reshape(bp * sp, d), bp, sp), (x_sample.reshape(bs * ss, d), bs, ss)]
    tile = lambda t: 1024 if t % 1024 == 0 else 512
    hs, kvs, cvs, hls = [[], []], [[], []], [[], []], [[], []]
    for i in range(DEPTH):
        j, kind = i // N_MIXERS, i % N_MIXERS
        new_streams = []
        for si, (x, bsz, s) in enumerate(streams):
            t = bsz * s
            tm = tile(t)
            if kind == 0:
                xw = _matmul(x, hgrn_w_in[j].astype(BF16), tm=512)
                s0 = (jnp.zeros((bsz, H_A, d // H_A, d // H_A), F32) if si == 0
                      else state_hgrn_s[j].astype(F32))
                o, s_fin = _hgrn(xw.reshape(bsz, s, -1), s0, lower_bounds[i], hgrn_gn_g[j],
                                 rows=min(s, 256))
                hs[si].append(s_fin)
                w_out = hgrn_w_out[j]
            elif kind == 1:
                qkv = _matmul(x, attn_w_qkv[j].astype(BF16), tm=512)
                hd = H_B * 2 * DH_B
                lam_init = _lambda_init(i)
                lam = (jnp.exp(jnp.sum(attn_lam_q1[j].astype(F32) * attn_lam_k1[j].astype(F32)))
                       - jnp.exp(jnp.sum(attn_lam_q2[j].astype(F32) * attn_lam_k2[j].astype(F32)))
                       + lam_init).reshape(1)
                qkv3 = qkv.reshape(bsz, s, 3 * hd)
                if si == 0:
                    o = _attn_prompt(qkv3, rel_bias, lam, attn_subln_g[j], lam_init)
                else:
                    past = cache_k.shape[2]
                    o = _attn_sample(qkv3, cache_k.reshape(-1, past * H_B, 2 * DH_B),
                                     cache_v.reshape(-1, past * H_B, 2 * DH_B),
                                     j, rel_bias, lam, attn_subln_g[j], lam_init)
                kvs[si].append((qkv3[:, :, hd:2 * hd].reshape(bsz, s, H_B, 2 * DH_B),
                                qkv3[:, :, 2 * hd:].reshape(bsz, s, H_B, 2 * DH_B)))
                w_out = attn_w_out[j]
            else:
                xw = _matmul(x, rglru_w_in[j].astype(BF16), tm=512).reshape(bsz, s, 2 * d)
                if si == 0:
                    cbuf = jnp.zeros((bsz, CONV_W - 1, d), F32)
                    h0 = jnp.zeros((bsz, d), F32)
                else:
                    cbuf, h0 = state_rglru_conv[j].astype(F32), state_rglru_h[j].astype(F32)
                cbuf8 = jnp.pad(cbuf, ((0, 0), (8 - (CONV_W - 1), 0), (0, 0)))
                o, hlast = _rglru(xw, cbuf8, h0.reshape(bsz, 1, d), rglru_conv_w[j], rglru_conv_b[j],
                                  rglru_w_a[j], rglru_b_a[j], rglru_w_x[j], rglru_b_x[j], rglru_lam[j],
                                  rows=min(s, 256))
                cvs[si].append(xw[:, s - (CONV_W - 1):, d:])
                hls[si].append(hlast.reshape(bsz, d))
                w_out = rglru_w_out[j]
            x = _mix_out_moe(o.reshape(t, -1), w_out.astype(BF16), x, ln1_g[i], ln1_b[i],
                             moe_w_router[i], moe_router_bias[i], w8, inv_in, wd8, inv_down,
                             moe_ws_in[i], moe_ws_down[i], ln2_g[i], ln2_b[i], tm=tm, layer=i)
            new_streams.append((x, bsz, s))
        streams = new_streams

    yp = streams[0][0].reshape(bp, sp, d)
    ys = streams[1][0].reshape(bs, ss, d)
    st = lambda xs: jnp.stack(xs)
    return (yp, ys, st(hs[0]), st(hs[1]),
            st([kv[0] for kv in kvs[0]]), st([kv[1] for kv in kvs[0]]),
            st([kv[0] for kv in kvs[1]]), st([kv[1] for kv in kvs[1]]),
            st(cvs[0]), st(hls[0]), st(cvs[1]), st(hls[1]))
```

```python
import functools
import math

import jax
import jax.numpy as jnp
from jax import lax
from jax.experimental import pallas as pl
from jax.experimental.pallas import tpu as pltpu

F32 = jnp.float32
BF16 = jnp.bfloat16
F8 = jnp.float8_e4m3fn
F8_MAX = 448.0
ACT_SCALE = 32.0

DEPTH = 4
CHUNK = 64
N_MIXERS = 3
H_A = 8
H_B = 8
DH_B = 64
REL_BUCKETS = 32
REL_MAX_DIST = 1024
N_BLK_C = 4
CONV_W = 4
C_RG = 8.0
N_EXPERTS = 64
D_EXPERT = 128
TOP_K = 8
N_GROUPS = 8
TOPK_GROUPS = 4
ROUTED_SCALE = 2.5
ALPHA = (2 * DEPTH) ** 0.25
LN_EPS = 1e-5
RMS_EPS = 1e-6

VMEM_LIMIT = 56 * 1024 * 1024
LOG2E = math.log2(math.e)
REC_C = 64
ATT_T = 256
ATT_G = 4
ATT_GS = 8
SCAN_G = 8
ONES_ROWS = 16
MOE_EB = 16
FP8_PREP_BYTES = 8 * 1024 * 1024

_NT = (((1,), (1,)), ((), ()))
_TN = (((0,), (0,)), ((), ()))


def _cparams(sem):
    return pltpu.CompilerParams(dimension_semantics=sem, vmem_limit_bytes=VMEM_LIMIT)


def _sigmoid(x):
    return 1.0 / (1.0 + jnp.exp(-x))


def _layer_norm(z, g, b):
    mu = jnp.mean(z, -1, keepdims=True)
    zc = z - mu
    var = jnp.mean(zc * zc, -1, keepdims=True)
    return zc * lax.rsqrt(var + LN_EPS) * g + b


def _mm_kernel(x_ref, w_ref, o_ref):
    o_ref[...] = jnp.dot(x_ref[...].astype(BF16), w_ref[...],
                         preferred_element_type=F32).astype(o_ref.dtype)


def _matmul(x, w, *, tm, out_dtype=F32):
    t, k = x.shape
    n = w.shape[1]
    return pl.pallas_call(
        _mm_kernel,
        grid=(t // tm,),
        in_specs=[pl.BlockSpec((tm, k), lambda i: (i, 0)),
                  pl.BlockSpec((k, n), lambda i: (0, 0))],
        out_specs=pl.BlockSpec((tm, n), lambda i: (i, 0)),
        out_shape=jax.ShapeDtypeStruct((t, n), out_dtype),
        compiler_params=_cparams(("arbitrary",)),
        name="proj_in",
    )(x, w)


def _hgrn_kernel(q_ref, fz_ref, v_ref, g_ref, lb_ref, gn_ref, s0_ref, o_ref, sfin_ref, st_scr,
                 *, chunk):
    c = pl.program_id(1)
    nh, dv, dk = st_scr.shape

    @pl.when(c == 0)
    def _():
        for h in range(nh):
            st_scr[h] = s0_ref[0, h].T

    rows = fz_ref.shape[1]
    lb = lb_ref[...]
    causal = (lax.broadcasted_iota(jnp.int32, (chunk, chunk), 0)
              >= lax.broadcasted_iota(jnp.int32, (chunk, chunk), 1))
    tri = jnp.where(causal, 1.0, 0.0).astype(BF16)
    sts = [st_scr[h] for h in range(nh)]
    for i in range(rows // chunk):
        sl = slice(i * chunk, (i + 1) * chunk)
        fz = fz_ref[0, sl, :]
        e = jnp.exp(-jnp.abs(fz))
        r = 1.0 / (1.0 + e)
        pos = fz >= 0
        sig = jnp.where(pos, r, e * r)
        nsig = jnp.where(pos, e * r, r)
        logf = jnp.log(lb + (1.0 - lb) * sig)
        kk = (1.0 - lb) * nsig
        hi = logf.astype(BF16)
        r1 = logf - hi.astype(F32)
        mid = r1.astype(BF16)
        lo = (r1 - mid.astype(F32)).astype(BF16)
        b = (jnp.dot(tri, hi, preferred_element_type=F32)
             + jnp.dot(tri, mid, preferred_element_type=F32)
             + jnp.dot(tri, lo, preferred_element_type=F32))
        bm = b[chunk // 2 - 1:chunk // 2]
        be = b[chunk - 1:chunk]
        qv = q_ref[0, sl, :]
        a_fac = qv * _sigmoid(qv) * jnp.exp(b - bm)
        k_fac = kk * jnp.exp(bm - b)
        qe = (a_fac * jnp.exp(bm)).astype(BF16)
        kd = (k_fac * jnp.exp(be - bm)).astype(BF16)
        a16 = a_fac.astype(BF16)
        k16 = k_fac.astype(BF16)
        dec = jnp.exp(be)
        v = v_ref[0, sl, :].astype(BF16)
        gv = g_ref[0, sl, :]
        gate = gv * _sigmoid(gv)
        for h in range(nh):
            hs = slice(h * dk, (h + 1) * dk)
            st = sts[h]
            o = lax.dot_general(qe[:, hs], st.astype(BF16), _NT, preferred_element_type=F32)
            sc = lax.dot_general(a16[:, hs], k16[:, hs], _NT, preferred_element_type=F32)
            sc = jnp.where(causal, sc, 0.0)
            o = o + jnp.dot(sc.astype(BF16), v[:, hs], preferred_element_type=F32)
            sts[h] = st * dec[:, hs] + lax.dot_general(v[:, hs], kd[:, hs], _TN,
                                                       preferred_element_type=F32)
            o = o * lax.rsqrt(jnp.mean(o * o, -1, keepdims=True) + RMS_EPS) * gn_ref[...] * gate[:, hs]
            o_ref[0, sl, hs] = o.astype(o_ref.dtype)
    for h in range(nh):
        st_scr[h] = sts[h]

    @pl.when(c == pl.num_programs(1) - 1)
    def _():
        for h in range(nh):
            sfin_ref[0, h] = st_scr[h].T


def _hgrn(xw, s0, lb, gn_g, *, rows):
    bsz, s, _ = xw.shape
    h, dk, dv = s0.shape[1:]
    sec = lambda k: pl.BlockSpec((1, rows, h * dk), lambda b, c, k=k: (b, c, k))
    return pl.pallas_call(
        functools.partial(_hgrn_kernel, chunk=REC_C),
        grid=(bsz, s // rows),
        in_specs=[sec(0), sec(1), sec(2), sec(3),
                  pl.BlockSpec((1, h * dk), lambda b, c: (0, 0)),
                  pl.BlockSpec((1, dv), lambda b, c: (0, 0)),
                  pl.BlockSpec((1, h, dk, dv), lambda b, c: (b, 0, 0, 0))],
        out_specs=[pl.BlockSpec((1, rows, h * dv), lambda b, c: (b, c, 0)),
                   pl.BlockSpec((1, h, dk, dv), lambda b, c: (b, 0, 0, 0))],
        out_shape=[jax.ShapeDtypeStruct((bsz, s, h * dv), BF16),
                   jax.ShapeDtypeStruct((bsz, h, dk, dv), F32)],
        scratch_shapes=[pltpu.VMEM((h, dv, dk), F32)],
        compiler_params=_cparams(("arbitrary", "arbitrary")),
        name="hgrn2",
    )(xw, xw, xw, xw, lb.reshape(1, h * dk), gn_g.reshape(1, dv), s0)


def _rel_bucket(rel):
    nb = REL_BUCKETS // 2
    max_exact = nb // 2
    ret = jnp.where(rel > 0, nb, 0)
    n = jnp.abs(rel)
    nf = jnp.maximum(n, 1).astype(F32)
    large = max_exact + (jnp.log(nf / max_exact) / math.log(REL_MAX_DIST / max_exact)
                         * (nb - max_exact)).astype(jnp.int32)
    large = jnp.minimum(large, nb - 1)
    return ret + jnp.where(n < max_exact, n, large)


def _saturation_distance():
    nb = REL_BUCKETS // 2
    max_exact = nb // 2
    return math.ceil(max_exact * (REL_MAX_DIST / max_exact) ** ((nb - 1 - max_exact) / (nb - max_exact))) + 1


def _bias_rows(rel_bias, rel0s, width):
    rel = jnp.asarray(rel0s, jnp.int32)[:, None] - jnp.arange(width, dtype=jnp.int32)[None, :]
    seg = rel_bias.astype(F32)[_rel_bucket(rel)] * LOG2E
    return jnp.transpose(seg, (2, 0, 1))[:, :, None, :]


def _far_bias(rel_bias):
    return rel_bias.astype(F32)[_rel_bucket(jnp.int32(-_saturation_distance()))] * LOG2E


def _bias_tile_t(seg, tk, shift):
    w = seg.shape[1]
    return pltpu.roll(jnp.broadcast_to(seg, (tk, w)), w - tk + shift, 1, stride=1, stride_axis=0)


def _stack_q(q):
    q = q * (DH_B ** -0.5 * LOG2E)
    lane = lax.broadcasted_iota(jnp.int32, q.shape, 1)
    qs = jnp.concatenate([jnp.where(lane < DH_B, q, 0.0), jnp.where(lane >= DH_B, q, 0.0)], axis=0)
    return qs.T.astype(BF16)


def _online(st, m, shift=None):
    if shift is None:
        m_new = jnp.maximum(m, jnp.max(st, axis=0, keepdims=True))
        p = jnp.exp2(st - m_new)
    else:
        m_raw = jnp.maximum(m - shift, jnp.max(st, axis=0, keepdims=True))
        p = jnp.exp2(st - m_raw)
        m_new = m_raw + shift
    return m_new, jnp.exp2(m - m_new), p.astype(BF16)


def _with_ones(vt):
    return jnp.concatenate([vt, jnp.ones((ONES_ROWS, vt.shape[1]), vt.dtype)], axis=0)


def _attn_out(ot, g_ref, out_scale):
    return ot * lax.rsqrt(jnp.mean(ot * ot, -1, keepdims=True) + RMS_EPS) * g_ref[...] * out_scale


def _attn_prompt_kernel(lam_ref, fb_ref, q_ref, k_ref, v_ref, seg_ref, g_ref, o_ref,
                        kb_scr, vt_scr, qst_scr, acc_scr, *, d_far, out_scale):
    hg = pl.program_id(1)
    qi = pl.program_id(2)
    ng, n, t, dv = kb_scr.shape
    w2 = 2 * t

    @pl.when(qi == 0)
    def _():
        for u in range(ng):
            for c in range(n):
                kb_scr[u, c] = k_ref[0, c * t:(c + 1) * t, u * dv:(u + 1) * dv].astype(BF16)
                vt_scr[u, c] = _with_ones(v_ref[0, c * t:(c + 1) * t, u * dv:(u + 1) * dv].T.astype(BF16))

    for u in range(ng):
        qst_scr[u] = _stack_q(q_ref[0, :, u * dv:(u + 1) * dv])
    acc_scr[...] = jnp.zeros(acc_scr.shape, F32)
    far_shift = jnp.concatenate([jnp.full((1, w2), fb_ref[hg * ng + u], F32) for u in range(ng)], axis=1)

    def scores(ki, u):
        return jnp.dot(kb_scr[u, ki], qst_scr[u], preferred_element_type=F32)

    def biased(st, seg, vis=None):
        bt = _bias_tile_t(seg, t, 0)[:, :t]
        s1, s2 = st[:, :t] + bt, st[:, t:] + bt
        if vis is not None:
            s1, s2 = jnp.where(vis, s1, -1e30), jnp.where(vis, s2, -1e30)
        return jnp.concatenate([s1, s2], axis=1)

    def update(ki, m, sts, shift=None):
        m, alpha, p = _online(jnp.concatenate(sts, axis=1), m, shift)
        for u in range(ng):
            sl = slice(u * w2, (u + 1) * w2)
            acc_scr[u] = alpha[:, sl] * acc_scr[u] + jnp.dot(vt_scr[u, ki], p[:, sl],
                                                             preferred_element_type=F32)
        return m

    def far_scores(ki):
        return [scores(ki, u) for u in range(ng)]

    def near_scores(ki):
        return [biased(scores(ki, u), seg_ref[u, qi - ki]) for u in range(ng)]

    def tiles(lo, hi, tile_scores, shift, m):
        def two(i, m):
            ka = lo + 2 * i
            sa, sb = tile_scores(ka), tile_scores(ka + 1)
            return update(ka + 1, update(ka, m, sa, shift), sb, shift)
        n2 = (hi - lo) // 2
        m = lax.fori_loop(0, n2, two, m)
        return lax.fori_loop(lo + 2 * n2, hi, lambda ki, m: update(ki, m, tile_scores(ki), shift), m)

    n_far = jnp.maximum(qi - (d_far - 1), 0)
    m = tiles(0, n_far, far_scores, far_shift, jnp.full((1, ng * w2), -1e30, F32))
    m = tiles(n_far, qi, near_scores, None, m)
    vis = (lax.broadcasted_iota(jnp.int32, (t, t), 0) // CHUNK
           <= lax.broadcasted_iota(jnp.int32, (t, t), 1) // CHUNK)
    update(qi, m, [biased(scores(qi, u), seg_ref[u, 0], vis) for u in range(ng)])
    for u in range(ng):
        acc = acc_scr[u]
        on = acc[:dv] * (1.0 / acc[dv:dv + 1])
        ot = (on[:, :t] - lam_ref[0] * on[:, t:]).T
        o_ref[0, :, u * dv:(u + 1) * dv] = _attn_out(ot, g_ref, out_scale).astype(o_ref.dtype)


def _attn_prompt(qkv, rel_bias, lam, subln_g, lam_init):
    bsz, s, _ = qkv.shape
    h, dv, t, g = H_B, 2 * DH_B, ATT_T, ATT_G
    n = s // t
    w = 2 * t
    d_far = -(-(_saturation_distance() - 1) // t) + 1
    segs = _bias_rows(rel_bias, [t - d * t for d in range(n)], w)
    kv = lambda sec: pl.BlockSpec((1, s, g * dv), lambda b, hg, qi, sec=sec: (b, 0, sec * (h // g) + hg))
    return pl.pallas_call(
        functools.partial(_attn_prompt_kernel, d_far=d_far, out_scale=1.0 - lam_init),
        grid=(bsz, h // g, n),
        in_specs=[pl.BlockSpec(memory_space=pltpu.SMEM), pl.BlockSpec(memory_space=pltpu.SMEM),
                  pl.BlockSpec((1, t, g * dv), lambda b, hg, qi: (b, qi, hg)),
                  kv(1), kv(2),
                  pl.BlockSpec((g, n, 1, w), lambda b, hg, qi: (hg, 0, 0, 0)),
                  pl.BlockSpec((1, dv), lambda b, hg, qi: (0, 0))],
        out_specs=pl.BlockSpec((1, t, g * dv), lambda b, hg, qi: (b, qi, hg)),
        out_shape=jax.ShapeDtypeStruct((bsz, s, h * dv), BF16),
        scratch_shapes=[pltpu.VMEM((g, n, t, dv), BF16), pltpu.VMEM((g, n, dv + ONES_ROWS, t), BF16),
                        pltpu.VMEM((g, dv, 2 * t), BF16), pltpu.VMEM((g, dv + ONES_ROWS, 2 * t), F32)],
        compiler_params=_cparams(("arbitrary",) * 3),
        name="diff_attn_prompt",
    )(lam, _far_bias(rel_bias), qkv, qkv, qkv, segs, subln_g.reshape(1, dv))


def _attn_sample_kernel(lam_ref, fb_ref, q_ref, kp_ref, vp_ref, segp_ref, kn_ref, vn_ref, segn_ref,
                        g_ref, o_ref, *, tkp, n_past, far_tiles, out_scale):
    hg = pl.program_id(1)
    tq = q_ref.shape[1]
    dv = g_ref.shape[1]
    ng = q_ref.shape[2] // dv
    nh = kp_ref.shape[1] // (n_past * tkp)
    w2 = 2 * tq
    qst = [_stack_q(q_ref[0, :, u * dv:(u + 1) * dv]) for u in range(ng)]
    lane = lax.broadcasted_iota(jnp.int32, (1, w2), 1)

    def bias(seg, tk):
        return jnp.where(lane < tq, _bias_tile_t(seg, tk, 0)[:, :w2], _bias_tile_t(seg, tk, tq)[:, :w2])

    m = jnp.full((1, ng * w2), -1e30, F32)
    far_shift = jnp.concatenate([jnp.full((1, w2), fb_ref[hg * ng + u], F32) for u in range(ng)], axis=1)
    acc = [jnp.zeros((dv + ONES_ROWS, w2), F32) for _ in range(ng)]
    for c in range(n_past + 1):
        sts, vts = [], []
        for u in range(ng):
            if c < n_past:
                rows = pl.ds(c * tkp * nh + hg * ng + u, tkp, stride=nh)
                k, v = kp_ref[0, rows, :], vp_ref[0, rows, :]
            else:
                k, v = kn_ref[0, :, u * dv:(u + 1) * dv], vn_ref[0, :, u * dv:(u + 1) * dv]
            st = jnp.dot(k.astype(BF16), qst[u], preferred_element_type=F32)
            if far_tiles <= c < n_past:
                st = st + bias(segp_ref[u, c - far_tiles], tkp)
            elif c == n_past:
                st = st + bias(segn_ref[u, 0], tq)
            sts.append(st)
            vts.append(_with_ones(v.T.astype(BF16)))
        m, alpha, p = _online(jnp.concatenate(sts, axis=1), m, far_shift if c < far_tiles else None)
        for u in range(ng):
            sl = slice(u * w2, (u + 1) * w2)
            acc[u] = alpha[:, sl] * acc[u] + jnp.dot(vts[u], p[:, sl], preferred_element_type=F32)
    for u in range(ng):
        on = acc[u][:dv] * (1.0 / acc[u][dv:dv + 1])
        ot = (on - lam_ref[0] * pltpu.roll(on, tq, 1)).T[:tq]
        o_ref[0, :, u * dv:(u + 1) * dv] = _attn_out(ot, g_ref, out_scale).astype(o_ref.dtype)


def _attn_sample(qkv, k_past, v_past, layer, rel_bias, lam, subln_g, lam_init):
    bsz, s, _ = qkv.shape
    h, dv, tkp, g = H_B, 2 * DH_B, ATT_T, ATT_GS
    past = k_past.shape[1] // h
    n_past = past // tkp
    far_tiles = sum(1 for c in range(n_past) if past - (c * tkp + tkp - 1) >= _saturation_distance())
    near = list(range(far_tiles, n_past)) or [n_past - 1]
    segp = _bias_rows(rel_bias, [c * tkp - past + tkp for c in near], 2 * s + tkp)
    segn = _bias_rows(rel_bias, [s], 4 * s)
    pk = lambda: pl.BlockSpec((1, past * h, dv), lambda b, hg: (layer * bsz + b, 0, 0))
    nk = lambda sec: pl.BlockSpec((1, s, g * dv), lambda b, hg, sec=sec: (b, 0, sec * (h // g) + hg))
    return pl.pallas_call(
        functools.partial(_attn_sample_kernel, tkp=tkp, n_past=n_past, far_tiles=far_tiles,
                          out_scale=1.0 - lam_init),
        grid=(bsz, h // g),
        in_specs=[pl.BlockSpec(memory_space=pltpu.SMEM), pl.BlockSpec(memory_space=pltpu.SMEM),
                  nk(0), pk(), pk(),
                  pl.BlockSpec((g, len(near), 1, 2 * s + tkp), lambda b, hg: (hg, 0, 0, 0)),
                  nk(1), nk(2),
                  pl.BlockSpec((g, 1, 1, 4 * s), lambda b, hg: (hg, 0, 0, 0)),
                  pl.BlockSpec((1, dv), lambda b, hg: (0, 0))],
        out_specs=pl.BlockSpec((1, s, g * dv), lambda b, hg: (b, 0, hg)),
        out_shape=jax.ShapeDtypeStruct((bsz, s, h * dv), BF16),
        compiler_params=_cparams(("arbitrary",) * 2),
        name="diff_attn_sample",
    )(lam, _far_bias(rel_bias), qkv, k_past, v_past, segp, qkv, qkv, segn, subln_g.reshape(1, dv))


def _rglru_kernel(xw_ref, cw_ref, cb_ref, wa_ref, ba_ref, wx_ref, bx_ref, lam_ref, cbuf_ref, h0_ref,
                  y_ref, hlast_ref, ext_scr, h_scr):
    t = pl.program_id(1)
    tt = y_ref.shape[1]
    d = y_ref.shape[2]
    pad = ext_scr.shape[0]

    @pl.when(t == 0)
    def _():
        ext_scr[...] = cbuf_ref[0]
        h_scr[...] = h0_ref[0]

    gate = xw_ref[0, :, 0:d]
    u = xw_ref[0, :, d:2 * d]
    hist = ext_scr[...]
    ext_scr[...] = u[tt - pad:tt]
    row = lax.broadcasted_iota(jnp.int32, (tt, d), 0)
    conv = cb_ref[...] + cw_ref[CONV_W - 1:CONV_W] * u
    for j in range(1, CONV_W):
        ru = pltpu.roll(u, j, 0)
        rh = jnp.concatenate([pltpu.roll(hist, j, 0), ru[pad:]], axis=0)
        conv = conv + cw_ref[CONV_W - 1 - j:CONV_W - j] * jnp.where(row < j, rh, ru)

    cb16 = conv.astype(BF16)
    blk = d // N_BLK_C
    ra, rx = [], []
    for n in range(N_BLK_C):
        cs = cb16[:, n * blk:(n + 1) * blk]
        ra.append(jnp.dot(cs, wa_ref[n], preferred_element_type=F32))
        rx.append(jnp.dot(cs, wx_ref[n], preferred_element_type=F32))
    r = _sigmoid(jnp.concatenate(ra, axis=1) + ba_ref[...])
    ig = _sigmoid(jnp.concatenate(rx, axis=1) + bx_ref[...])
    nl = -lam_ref[...]
    sp = jnp.maximum(nl, 0.0) + jnp.log(1.0 + jnp.exp(-jnp.abs(nl)))
    log_a = (-C_RG) * r * sp
    a = jnp.exp(log_a)
    xin = jnp.sqrt(1.0 - a * a) * ig * conv

    ca, cx = a, xin
    sft = 1
    while sft < SCAN_G:
        keep = (row % SCAN_G) >= sft
        pa = jnp.where(keep, pltpu.roll(ca, sft, 0), 1.0)
        px = jnp.where(keep, pltpu.roll(cx, sft, 0), 0.0)
        cx = ca * px + cx
        ca = ca * pa
        sft *= 2
    carry = h_scr[...]
    groups = []
    for gi in range(tt // SCAN_G):
        gs = slice(gi * SCAN_G, (gi + 1) * SCAN_G)
        hg = ca[gs] * carry + cx[gs]
        carry = hg[SCAN_G - 1:SCAN_G]
        groups.append(hg)
    hh = jnp.concatenate(groups, axis=0)
    h_scr[...] = carry
    k1 = -2.0 * math.sqrt(2.0 / math.pi) * LOG2E
    gelu = gate / (1.0 + jnp.exp2(gate * (k1 + (k1 * 0.044715) * (gate * gate))))
    y_ref[0] = (hh * gelu).astype(y_ref.dtype)

    @pl.when(t == pl.num_programs(1) - 1)
    def _():
        hlast_ref[0] = hh[tt - 1:tt]


def _rglru(xw, cbuf8, h0, conv_w, conv_b, w_a, b_a, w_x, b_x, lam, *, rows):
    bsz, s, d2 = xw.shape
    d = d2 // 2
    pad = cbuf8.shape[1]
    blk = d // N_BLK_C
    row = lambda a: a.reshape(1, d).astype(F32)
    full2 = lambda shp: pl.BlockSpec(shp, lambda b, t: (0,) * len(shp))
    return pl.pallas_call(
        _rglru_kernel,
        grid=(bsz, s // rows),
        in_specs=[pl.BlockSpec((1, rows, d2), lambda b, t: (b, t, 0)),
                  full2((CONV_W, d)), full2((1, d)),
                  full2((N_BLK_C, blk, blk)), full2((1, d)),
                  full2((N_BLK_C, blk, blk)), full2((1, d)),
                  full2((1, d)),
                  pl.BlockSpec((1, pad, d), lambda b, t: (b, 0, 0)),
                  pl.BlockSpec((1, 1, d), lambda b, t: (b, 0, 0))],
        out_specs=[pl.BlockSpec((1, rows, d), lambda b, t: (b, t, 0)),
                   pl.BlockSpec((1, 1, d), lambda b, t: (b, 0, 0))],
        out_shape=[jax.ShapeDtypeStruct((bsz, s, d), BF16),
                   jax.ShapeDtypeStruct((bsz, 1, d), F32)],
        scratch_shapes=[pltpu.VMEM((pad, d), F32), pltpu.VMEM((1, d), F32)],
        compiler_params=_cparams(("arbitrary", "arbitrary")),
        name="rglru",
    )(xw, conv_w.astype(F32), row(conv_b), w_a.astype(BF16), row(b_a), w_x.astype(BF16), row(b_x),
      row(lam), cbuf8, h0)


def _route(x, wr_t, rbias):
    ne = wr_t.shape[0]
    tm = x.shape[0]
    gsz = ne // N_GROUPS
    logits = lax.dot_general(wr_t, x, _NT, preferred_element_type=F32)
    score = _sigmoid(logits)
    biased = score + rbias
    neg = -jnp.inf
    sub = lax.broadcasted_iota(jnp.int32, (gsz, tm), 0)
    gscore = []
    for g in range(N_GROUPS):
        vg = biased[g * gsz:(g + 1) * gsz]
        m1 = jnp.max(vg, axis=0, keepdims=True)
        first = jnp.min(jnp.where(vg == m1, sub, gsz), axis=0, keepdims=True)
        m2 = jnp.max(jnp.where(sub == first, neg, vg), axis=0, keepdims=True)
        gscore.append(m1 + m2)
    masked = []
    for g in range(N_GROUPS):
        rank = jnp.zeros((1, tm), jnp.int32)
        for o in range(N_GROUPS):
            if o == g:
                continue
            ahead = (gscore[o] >= gscore[g]) if o < g else (gscore[o] > gscore[g])
            rank = rank + ahead.astype(jnp.int32)
        keep = rank < TOPK_GROUPS
        masked.append(jnp.where(keep, biased[g * gsz:(g + 1) * gsz], neg))
    cur = jnp.concatenate(masked, axis=0)
    eidx = lax.broadcasted_iota(jnp.int32, (ne, tm), 0)
    chosen = jnp.zeros((ne, tm), F32)
    for _ in range(TOP_K):
        m = jnp.max(cur, axis=0, keepdims=True)
        first = jnp.min(jnp.where(cur == m, eidx, ne), axis=0, keepdims=True)
        hit = eidx == first
        chosen = jnp.where(hit, 1.0, chosen)
        cur = jnp.where(hit, neg, cur)
    wsel = chosen * score
    return wsel / jnp.sum(wsel, axis=0, keepdims=True) * ROUTED_SCALE


def _swiglu(h, f):
    hg = h[:, :f]
    return hg * _sigmoid(hg) * h[:, f:]


def _moe_kernel(inv_ref, invd_ref, o_ref, wo_ref, x0_ref, g1_ref, b1_ref, wr_ref, rb_ref, wsi_ref, wsd_ref,
                wi_ref, wd_ref, g_ref, b_ref, out_ref, x_scr, xb_scr, x8_scr, acc_scr, gate_scr, *, eb, layer):
    j = pl.program_id(1)
    f = wsd_ref.shape[0]

    @pl.when(j == 0)
    def _():
        x = _layer_norm(ALPHA * x0_ref[...] + jnp.dot(o_ref[...], wo_ref[...], preferred_element_type=F32),
                        g1_ref[...], b1_ref[...])
        x_scr[...] = x
        xb = x.astype(BF16)
        xb_scr[...] = xb
        x8_scr[...] = x.astype(F8)
        w_et = _route(xb, wr_ref[...], rb_ref[...])
        ne, tm = w_et.shape
        gate_scr[...] = jnp.concatenate([w_et, jnp.zeros((gate_scr.shape[1] - ne, tm), F32)], axis=0).T
        sh = _swiglu(jnp.dot(xb, wsi_ref[...], preferred_element_type=F32), f)
        acc_scr[...] = jnp.dot(sh.astype(BF16), wsd_ref[...], preferred_element_type=F32)

    x8 = x8_scr[...]
    gates = gate_scr[...]
    lane = lax.broadcasted_iota(jnp.int32, gates.shape, 1)
    acts = []
    for i in range(eb):
        e = j * eb + i
        inv = inv_ref[layer * (eb * pl.num_programs(1)) + e]
        hs = jnp.dot(x8, wi_ref[i], preferred_element_type=F32)
        hg, hu = hs[:, :f], hs[:, f:]
        gcol = jnp.sum(jnp.where(lane == e, gates, 0.0), axis=1, keepdims=True)
        sig = 1.0 / (1.0 + jnp.exp2(hg * (-LOG2E * inv)))
        act = hg * sig * hu * (gcol * (ACT_SCALE * inv * inv))
        acts.append(jnp.clip(act, -F8_MAX, F8_MAX).astype(F8))
    acc_scr[...] += (jnp.dot(jnp.concatenate(acts, axis=1), wd_ref[0], preferred_element_type=F32)
                     * (invd_ref[layer * pl.num_programs(1) + j] * (1.0 / ACT_SCALE)))

    @pl.when(j == pl.num_programs(1) - 1)
    def _():
        out_ref[...] = _layer_norm(ALPHA * x_scr[...] + acc_scr[...], g_ref[...], b_ref[...])


def _fp8_prep_kernel(w_ref, o_ref, inv_ref):
    for i in range(w_ref.shape[0]):
        w = w_ref[i]
        amax = jnp.max(jnp.max(jnp.abs(w), axis=1, keepdims=True), axis=0, keepdims=True)
        k = jnp.floor(jnp.log2(F8_MAX / jnp.maximum(amax, 1e-30))) - 1.0
        scale = jnp.exp2(jnp.clip(k, -60.0, 60.0))
        o_ref[i] = (w * scale).astype(F8)
        inv_ref[i] = jnp.broadcast_to(1.0 / scale, inv_ref.shape[1:])


def _fp8_weights(w):
    n, r, c = w.shape
    per = max(1, min(n, FP8_PREP_BYTES // (r * c * 4)))
    w8, inv = pl.pallas_call(
        _fp8_prep_kernel,
        grid=(n // per,),
        in_specs=[pl.BlockSpec((per, r, c), lambda i: (i, 0, 0))],
        out_specs=[pl.BlockSpec((per, r, c), lambda i: (i, 0, 0)),
                   pl.BlockSpec((per, 8, 128), lambda i: (i, 0, 0))],
        out_shape=[jax.ShapeDtypeStruct((n, r, c), F8), jax.ShapeDtypeStruct((n, 8, 128), F32)],
        compiler_params=_cparams(("arbitrary",)),
        name="fp8_weights",
    )(w.astype(F32))
    return w8, inv[:, 0, 0]


def _mix_out_moe(o, w_out, x0, g1, b1, w_router, router_bias, w8, inv_scale, wd8, inv_scale_d, ws_in, ws_down,
                 g, b, *, tm, layer):
    t, d = x0.shape
    ne = w_router.shape[1]
    f2 = w8.shape[2]
    f = f2 // 2
    eb = MOE_EB
    nb = ne // eb
    const = lambda shp: pl.BlockSpec(shp, lambda i, j: (0,) * len(shp))
    return pl.pallas_call(
        functools.partial(_moe_kernel, eb=eb, layer=layer),
        grid=(t // tm, nb),
        in_specs=[pl.BlockSpec(memory_space=pltpu.SMEM), pl.BlockSpec(memory_space=pltpu.SMEM),
                  pl.BlockSpec((tm, o.shape[1]), lambda i, j: (i, 0)), const(w_out.shape),
                  pl.BlockSpec((tm, d), lambda i, j: (i, 0)), const((1, d)), const((1, d)),
                  const((ne, d)), const((ne, 1)),
                  const((d, f2)), const((f, d)),
                  pl.BlockSpec((eb, d, f2), lambda i, j: (layer * nb + j, 0, 0)),
                  pl.BlockSpec((1, eb * f, d), lambda i, j: (layer * nb + j, 0, 0)),
                  const((1, d)), const((1, d))],
        out_specs=pl.BlockSpec((tm, d), lambda i, j: (i, 0)),
        out_shape=jax.ShapeDtypeStruct((t, d), F32),
        scratch_shapes=[pltpu.VMEM((tm, d), F32), pltpu.VMEM((tm, d), BF16), pltpu.VMEM((tm, d), F8),
                        pltpu.VMEM((tm, d), F32), pltpu.VMEM((tm, 128), F32)],
        compiler_params=_cparams(("arbitrary", "arbitrary")),
        name="moe_ln",
    )(inv_scale, inv_scale_d, o, w_out, x0, g1.reshape(1, d), b1.reshape(1, d),
      w_router.T.astype(BF16), router_bias.reshape(ne, 1).astype(F32),
      ws_in.astype(BF16), ws_down.astype(BF16), w8, wd8,
      g.reshape(1, d), b.reshape(1, d))


def _lambda_init(layer_idx):
    return 0.8 - 0.6 * math.exp(-0.3 * layer_idx)


def kernel(x_prompt, x_sample, state_hgrn_s, cache_k, cache_v, state_rglru_conv, state_rglru_h, hgrn_lb_logits, hgrn_w_in, hgrn_gn_g, hgrn_w_out, rel_bias, attn_w_qkv, attn_lam_q1, attn_lam_k1, attn_lam_q2, attn_lam_k2, attn_subln_g, attn_w_out, rglru_w_in, rglru_conv_w, rglru_conv_b, rglru_w_a, rglru_b_a, rglru_w_x, rglru_b_x, rglru_lam, rglru_w_out, ln1_g, ln1_b, ln2_g, ln2_b, moe_w_router, moe_router_bias, moe_w_in, moe_w_down, moe_ws_in, moe_ws_down):
    bp, sp, d = x_prompt.shape
    bs, ss, _ = x_sample.shape
    lb_p = jax.nn.softmax(hgrn_lb_logits.astype(F32), axis=0)
    lower_bounds = jnp.clip(jnp.cumsum(lb_p, axis=0) - lb_p[0], 0.0, 1.0)

    nl, ne, _, f2 = moe_w_in.shape
    w8, inv_in = _fp8_weights(moe_w_in.reshape(nl * ne, d, f2))
    wd8, inv_down = _fp8_weights(moe_w_down.reshape(nl * ne // MOE_EB, MOE_EB * (f2 // 2), d))

    streams = [(x_prompt.reshape(bp * sp, d), bp, sp), (x_sample.reshape(bs * ss, d), bs, ss)]
    tile = lambda t: 1024 if t % 1024 == 0 else 512
    hs, kvs, cvs, hls = [[], []], [[], []], [[], []], [[], []]
    for i in range(DEPTH):
        j, kind = i // N_MIXERS, i % N_MIXERS
        new_streams = []
        for si, (x, bsz, s) in enumerate(streams):
            t = bsz * s
            tm = tile(t)
            if kind == 0:
                xw = _matmul(x, hgrn_w_in[j].astype(BF16), tm=512)
                s0 = (jnp.zeros((bsz, H_A, d // H_A, d // H_A), F32) if si == 0
                      else state_hgrn_s[j].astype(F32))
                o, s_fin = _hgrn(xw.reshape(bsz, s, -1), s0, lower_bounds[i], hgrn_gn_g[j],
                                 rows=min(s, 256))
                hs[si].append(s_fin)
                w_out = hgrn_w_out[j]
            elif kind == 1:
                qkv = _matmul(x, attn_w_qkv[j].astype(BF16), tm=512)
                hd = H_B * 2 * DH_B
                lam_init = _lambda_init(i)
                lam = (jnp.exp(jnp.sum(attn_lam_q1[j].astype(F32) * attn_lam_k1[j].astype(F32)))
                       - jnp.exp(jnp.sum(attn_lam_q2[j].astype(F32) * attn_lam_k2[j].astype(F32)))
                       + lam_init).reshape(1)
                qkv3 = qkv.reshape(bsz, s, 3 * hd)
                if si == 0:
                    o = _attn_prompt(qkv3, rel_bias, lam, attn_subln_g[j], lam_init)
                else:
                    past = cache_k.shape[2]
                    o = _attn_sample(qkv3, cache_k.reshape(-1, past * H_B, 2 * DH_B),
                                     cache_v.reshape(-1, past * H_B, 2 * DH_B),
                                     j, rel_bias, lam, attn_subln_g[j], lam_init)
                kvs[si].append((qkv3[:, :, hd:2 * hd].reshape(bsz, s, H_B, 2 * DH_B),
                                qkv3[:, :, 2 * hd:].reshape(bsz, s, H_B, 2 * DH_B)))
                w_out = attn_w_out[j]
            else:
                xw = _matmul(x, rglru_w_in[j].astype(BF16), tm=512).reshape(bsz, s, 2 * d)
                if si == 0:
                    cbuf = jnp.zeros((bsz, CONV_W - 1, d), F32)
                    h0 = jnp.zeros((bsz, d), F32)
                else:
                    cbuf, h0 = state_rglru_conv[j].astype(F32), state_rglru_h[j].astype(F32)
                cbuf8 = jnp.pad(cbuf, ((0, 0), (8 - (CONV_W - 1), 0), (0, 0)))
                o, hlast = _rglru(xw, cbuf8, h0.reshape(bsz, 1, d), rglru_conv_w[j], rglru_conv_b[j],
                                  rglru_w_a[j], rglru_b_a[j], rglru_w_x[j], rglru_b_x[j], rglru_lam[j],
                                  rows=min(s, 256))
                cvs[si].append(xw[:, s - (CONV_W - 1):, d:])
                hls[si].append(hlast.reshape(bsz, d))
                w_out = rglru_w_out[j]
            x = _mix_out_moe(o.reshape(t, -1), w_out.astype(BF16), x, ln1_g[i], ln1_b[i],
                             moe_w_router[i], moe_router_bias[i], w8, inv_in, wd8, inv_down,
                             moe_ws_in[i], moe_ws_down[i], ln2_g[i], ln2_b[i], tm=tm, layer=i)
            new_streams.append((x, bsz, s))
        streams = new_streams

    yp = streams[0][0].reshape(bp, sp, d)
    ys = streams[1][0].reshape(bs, ss, d)
    st = lambda xs: jnp.stack(xs)
    return (yp, ys, st(hs[0]), st(hs[1]),
            st([kv[0] for kv in kvs[0]]), st([kv[1] for kv in kvs[0]]),
            st([kv[0] for kv in kvs[1]]), st([kv[1] for kv in kvs[1]]),
            st(cvs[0]), st(hls[0]), st(cvs[1]), st(hls[1]))
```

```python
import functools
import math

import jax
import jax.numpy as jnp
from jax import lax
from jax.experimental import pallas as pl
from jax.experimental.pallas import tpu as pltpu

F32 = jnp.float32
BF16 = jnp.bfloat16
F8 = jnp.float8_e4m3fn
F8_MAX = 448.0
ACT_SCALE = 32.0

DEPTH = 4
CHUNK = 64
N_MIXERS = 3
H_A = 8
H_B = 8
DH_B = 64
REL_BUCKETS = 32
REL_MAX_DIST = 1024
N_BLK_C = 4
CONV_W = 4
C_RG = 8.0
N_EXPERTS = 64
D_EXPERT = 128
TOP_K = 8
N_GROUPS = 8
TOPK_GROUPS = 4
ROUTED_SCALE = 2.5
ALPHA = (2 * DEPTH) ** 0.25
LN_EPS = 1e-5
RMS_EPS = 1e-6

VMEM_LIMIT = 56 * 1024 * 1024
LOG2E = math.log2(math.e)
REC_C = 64
ATT_T = 256
ATT_G = 4
ATT_GS = 8
SCAN_G = 8
ONES_ROWS = 16
MOE_EB = 16
FP8_PREP_BYTES = 8 * 1024 * 1024

_NT = (((1,), (1,)), ((), ()))
_TN = (((0,), (0,)), ((), ()))


def _cparams(sem):
    return pltpu.CompilerParams(dimension_semantics=sem, vmem_limit_bytes=VMEM_LIMIT)


def _sigmoid(x):
    return 1.0 / (1.0 + jnp.exp(-x))


def _layer_norm(z, g, b):
    mu = jnp.mean(z, -1, keepdims=True)
    zc = z - mu
    var = jnp.mean(zc * zc, -1, keepdims=True)
    return zc * lax.rsqrt(var + LN_EPS) * g + b


def _mm_kernel(x_ref, w_ref, o_ref):
    o_ref[...] = jnp.dot(x_ref[...].astype(BF16), w_ref[...],
                         preferred_element_type=F32).astype(o_ref.dtype)


def _matmul(x, w, *, tm, out_dtype=F32):
    t, k = x.shape
    n = w.shape[1]
    return pl.pallas_call(
        _mm_kernel,
        grid=(t // tm,),
        in_specs=[pl.BlockSpec((tm, k), lambda i: (i, 0)),
                  pl.BlockSpec((k, n), lambda i: (0, 0))],
        out_specs=pl.BlockSpec((tm, n), lambda i: (i, 0)),
        out_shape=jax.ShapeDtypeStruct((t, n), out_dtype),
        compiler_params=_cparams(("arbitrary",)),
        name="proj_in",
    )(x, w)


def _cumsum_rows(x):
    n = x.shape[0]
    row = lax.broadcasted_iota(jnp.int32, x.shape, 0)
    sft = 1
    while sft < SCAN_G:
        x = x + jnp.where((row % SCAN_G) >= sft, pltpu.roll(x, sft, 0), 0.0)
        sft *= 2
    carry = jnp.zeros((1, x.shape[1]), x.dtype)
    groups = []
    for gi in range(n // SCAN_G):
        xg = x[gi * SCAN_G:(gi + 1) * SCAN_G] + carry
        carry = xg[SCAN_G - 1:SCAN_G]
        groups.append(xg)
    return jnp.concatenate(groups, axis=0)


def _hgrn_kernel(q_ref, fz_ref, v_ref, g_ref, lb_ref, gn_ref, s0_ref, o_ref, sfin_ref, st_scr,
                 *, chunk):
    c = pl.program_id(1)
    nh, dv, dk = st_scr.shape

    @pl.when(c == 0)
    def _():
        for h in range(nh):
            st_scr[h] = s0_ref[0, h].T

    rows = fz_ref.shape[1]
    lb = lb_ref[...]
    causal = (lax.broadcasted_iota(jnp.int32, (chunk, chunk), 0)
              >= lax.broadcasted_iota(jnp.int32, (chunk, chunk), 1))
    sts = [st_scr[h] for h in range(nh)]
    for i in range(rows // chunk):
        sl = slice(i * chunk, (i + 1) * chunk)
        fz = fz_ref[0, sl, :]
        e = jnp.exp(-jnp.abs(fz))
        r = 1.0 / (1.0 + e)
        pos = fz >= 0
        sig = jnp.where(pos, r, e * r)
        nsig = jnp.where(pos, e * r, r)
        logf = jnp.log(lb + (1.0 - lb) * sig)
        kk = (1.0 - lb) * nsig
        b = _cumsum_rows(logf)
        bm = b[chunk // 2 - 1:chunk // 2]
        be = b[chunk - 1:chunk]
        qv = q_ref[0, sl, :]
        a_fac = qv * _sigmoid(qv) * jnp.exp(b - bm)
        k_fac = kk * jnp.exp(bm - b)
        qe = (a_fac * jnp.exp(bm)).astype(BF16)
        kd = (k_fac * jnp.exp(be - bm)).astype(BF16)
        a16 = a_fac.astype(BF16)
        k16 = k_fac.astype(BF16)
        dec = jnp.exp(be)
        v = v_ref[0, sl, :].astype(BF16)
        gv = g_ref[0, sl, :]
        gate = gv * _sigmoid(gv)
        for h in range(nh):
            hs = slice(h * dk, (h + 1) * dk)
            st = sts[h]
            o = lax.dot_general(qe[:, hs], st.astype(BF16), _NT, preferred_element_type=F32)
            sc = lax.dot_general(a16[:, hs], k16[:, hs], _NT, preferred_element_type=F32)
            sc = jnp.where(causal, sc, 0.0)
            o = o + jnp.dot(sc.astype(BF16), v[:, hs], preferred_element_type=F32)
            sts[h] = st * dec[:, hs] + lax.dot_general(v[:, hs], kd[:, hs], _TN,
                                                       preferred_element_type=F32)
            o = o * lax.rsqrt(jnp.mean(o * o, -1, keepdims=True) + RMS_EPS) * gn_ref[...] * gate[:, hs]
            o_ref[0, sl, hs] = o.astype(o_ref.dtype)
    for h in range(nh):
        st_scr[h] = sts[h]

    @pl.when(c == pl.num_programs(1) - 1)
    def _():
        for h in range(nh):
            sfin_ref[0, h] = st_scr[h].T


def _hgrn(xw, s0, lb, gn_g, *, rows):
    bsz, s, _ = xw.shape
    h, dk, dv = s0.shape[1:]
    sec = lambda k: pl.BlockSpec((1, rows, h * dk), lambda b, c, k=k: (b, c, k))
    return pl.pallas_call(
        functools.partial(_hgrn_kernel, chunk=REC_C),
        grid=(bsz, s // rows),
        in_specs=[sec(0), sec(1), sec(2), sec(3),
                  pl.BlockSpec((1, h * dk), lambda b, c: (0, 0)),
                  pl.BlockSpec((1, dv), lambda b, c: (0, 0)),
                  pl.BlockSpec((1, h, dk, dv), lambda b, c: (b, 0, 0, 0))],
        out_specs=[pl.BlockSpec((1, rows, h * dv), lambda b, c: (b, c, 0)),
                   pl.BlockSpec((1, h, dk, dv), lambda b, c: (b, 0, 0, 0))],
        out_shape=[jax.ShapeDtypeStruct((bsz, s, h * dv), BF16),
                   jax.ShapeDtypeStruct((bsz, h, dk, dv), F32)],
        scratch_shapes=[pltpu.VMEM((h, dv, dk), F32)],
        compiler_params=_cparams(("arbitrary", "arbitrary")),
        name="hgrn2",
    )(xw, xw, xw, xw, lb.reshape(1, h * dk), gn_g.reshape(1, dv), s0)


def _rel_bucket(rel):
    nb = REL_BUCKETS // 2
    max_exact = nb // 2
    ret = jnp.where(rel > 0, nb, 0)
    n = jnp.abs(rel)
    nf = jnp.maximum(n, 1).astype(F32)
    large = max_exact + (jnp.log(nf / max_exact) / math.log(REL_MAX_DIST / max_exact)
                         * (nb - max_exact)).astype(jnp.int32)
    large = jnp.minimum(large, nb - 1)
    return ret + jnp.where(n < max_exact, n, large)


def _saturation_distance():
    nb = REL_BUCKETS // 2
    max_exact = nb // 2
    return math.ceil(max_exact * (REL_MAX_DIST / max_exact) ** ((nb - 1 - max_exact) / (nb - max_exact))) + 1


def _bias_rows(rel_bias, rel0s, width):
    rel = jnp.asarray(rel0s, jnp.int32)[:, None] - jnp.arange(width, dtype=jnp.int32)[None, :]
    seg = rel_bias.astype(F32)[_rel_bucket(rel)] * LOG2E
    return jnp.transpose(seg, (2, 0, 1))[:, :, None, :]


def _far_bias(rel_bias):
    return rel_bias.astype(F32)[_rel_bucket(jnp.int32(-_saturation_distance()))] * LOG2E


def _bias_tile_t(seg, tk, shift):
    w = seg.shape[1]
    return pltpu.roll(jnp.broadcast_to(seg, (tk, w)), w - tk + shift, 1, stride=1, stride_axis=0)


def _stack_q(q):
    q = q * (DH_B ** -0.5 * LOG2E)
    lane = lax.broadcasted_iota(jnp.int32, q.shape, 1)
    qs = jnp.concatenate([jnp.where(lane < DH_B, q, 0.0), jnp.where(lane >= DH_B, q, 0.0)], axis=0)
    return qs.T.astype(BF16)


def _online(st, m, shift=None):
    if shift is None:
        m_new = jnp.maximum(m, jnp.max(st, axis=0, keepdims=True))
        p = jnp.exp2(st - m_new)
    else:
        m_raw = jnp.maximum(m - shift, jnp.max(st, axis=0, keepdims=True))
        p = jnp.exp2(st - m_raw)
        m_new = m_raw + shift
    return m_new, jnp.exp2(m - m_new), p.astype(BF16)


def _with_ones(vt):
    return jnp.concatenate([vt, jnp.ones((ONES_ROWS, vt.shape[1]), vt.dtype)], axis=0)


def _attn_out(ot, g_ref, out_scale):
    return ot * lax.rsqrt(jnp.mean(ot * ot, -1, keepdims=True) + RMS_EPS) * g_ref[...] * out_scale


def _attn_prompt_kernel(lam_ref, fb_ref, q_ref, k_ref, v_ref, seg_ref, g_ref, o_ref,
                        kb_scr, vt_scr, qst_scr, acc_scr, *, d_far, out_scale):
    hg = pl.program_id(1)
    qi = pl.program_id(2)
    ng, n, t, dv = kb_scr.shape
    w2 = 2 * t

    @pl.when(qi == 0)
    def _():
        for u in range(ng):
            for c in range(n):
                kb_scr[u, c] = k_ref[0, c * t:(c + 1) * t, u * dv:(u + 1) * dv].astype(BF16)
                vt_scr[u, c] = _with_ones(v_ref[0, c * t:(c + 1) * t, u * dv:(u + 1) * dv].T.astype(BF16))

    for u in range(ng):
        qst_scr[u] = _stack_q(q_ref[0, :, u * dv:(u + 1) * dv])
    acc_scr[...] = jnp.zeros(acc_scr.shape, F32)
    far_shift = jnp.concatenate([jnp.full((1, w2), fb_ref[hg * ng + u], F32) for u in range(ng)], axis=1)

    def scores(ki, u):
        return jnp.dot(kb_scr[u, ki], qst_scr[u], preferred_element_type=F32)

    def biased(st, seg, vis=None):
        bt = _bias_tile_t(seg, t, 0)[:, :t]
        s1, s2 = st[:, :t] + bt, st[:, t:] + bt
        if vis is not None:
            s1, s2 = jnp.where(vis, s1, -1e30), jnp.where(vis, s2, -1e30)
        return jnp.concatenate([s1, s2], axis=1)

    def update(ki, m, sts, shift=None):
        m, alpha, p = _online(jnp.concatenate(sts, axis=1), m, shift)
        for u in range(ng):
            sl = slice(u * w2, (u + 1) * w2)
            acc_scr[u] = alpha[:, sl] * acc_scr[u] + jnp.dot(vt_scr[u, ki], p[:, sl],
                                                             preferred_element_type=F32)
        return m

    def far_scores(ki):
        return [scores(ki, u) for u in range(ng)]

    def near_scores(ki):
        return [biased(scores(ki, u), seg_ref[u, qi - ki]) for u in range(ng)]

    def tiles(lo, hi, tile_scores, shift, m):
        def two(i, m):
            ka = lo + 2 * i
            sa, sb = tile_scores(ka), tile_scores(ka + 1)
            return update(ka + 1, update(ka, m, sa, shift), sb, shift)
        n2 = (hi - lo) // 2
        m = lax.fori_loop(0, n2, two, m)
        return lax.fori_loop(lo + 2 * n2, hi, lambda ki, m: update(ki, m, tile_scores(ki), shift), m)

    n_far = jnp.maximum(qi - (d_far - 1), 0)
    m = tiles(0, n_far, far_scores, far_shift, jnp.full((1, ng * w2), -1e30, F32))
    m = tiles(n_far, qi, near_scores, None, m)
    vis = (lax.broadcasted_iota(jnp.int32, (t, t), 0) // CHUNK
           <= lax.broadcasted_iota(jnp.int32, (t, t), 1) // CHUNK)
    update(qi, m, [biased(scores(qi, u), seg_ref[u, 0], vis) for u in range(ng)])
    for u in range(ng):
        acc = acc_scr[u]
        on = acc[:dv] * (1.0 / acc[dv:dv + 1])
        ot = (on[:, :t] - lam_ref[0] * on[:, t:]).T
        o_ref[0, :, u * dv:(u + 1) * dv] = _attn_out(ot, g_ref, out_scale).astype(o_ref.dtype)


def _attn_prompt(qkv, rel_bias, lam, subln_g, lam_init):
    bsz, s, _ = qkv.shape
    h, dv, t, g = H_B, 2 * DH_B, ATT_T, ATT_G
    n = s // t
    w = 2 * t
    d_far = -(-(_saturation_distance() - 1) // t) + 1
    segs = _bias_rows(rel_bias, [t - d * t for d in range(n)], w)
    kv = lambda sec: pl.BlockSpec((1, s, g * dv), lambda b, hg, qi, sec=sec: (b, 0, sec * (h // g) + hg))
    return pl.pallas_call(
        functools.partial(_attn_prompt_kernel, d_far=d_far, out_scale=1.0 - lam_init),
        grid=(bsz, h // g, n),
        in_specs=[pl.BlockSpec(memory_space=pltpu.SMEM), pl.BlockSpec(memory_space=pltpu.SMEM),
                  pl.BlockSpec((1, t, g * dv), lambda b, hg, qi: (b, qi, hg)),
                  kv(1), kv(2),
                  pl.BlockSpec((g, n, 1, w), lambda b, hg, qi: (hg, 0, 0, 0)),
                  pl.BlockSpec((1, dv), lambda b, hg, qi: (0, 0))],
        out_specs=pl.BlockSpec((1, t, g * dv), lambda b, hg, qi: (b, qi, hg)),
        out_shape=jax.ShapeDtypeStruct((bsz, s, h * dv), BF16),
        scratch_shapes=[pltpu.VMEM((g, n, t, dv), BF16), pltpu.VMEM((g, n, dv + ONES_ROWS, t), BF16),
                        pltpu.VMEM((g, dv, 2 * t), BF16), pltpu.VMEM((g, dv + ONES_ROWS, 2 * t), F32)],
        compiler_params=_cparams(("arbitrary",) * 3),
        name="diff_attn_prompt",
    )(lam, _far_bias(rel_bias), qkv, qkv, qkv, segs, subln_g.reshape(1, dv))


def _attn_sample_kernel(lam_ref, fb_ref, q_ref, kp_ref, vp_ref, segp_ref, kn_ref, vn_ref, segn_ref,
                        g_ref, o_ref, *, tkp, n_past, far_tiles, out_scale):
    hg = pl.program_id(1)
    tq = q_ref.shape[1]
    dv = g_ref.shape[1]
    ng = q_ref.shape[2] // dv
    nh = kp_ref.shape[1] // (n_past * tkp)
    w2 = 2 * tq
    qst = [_stack_q(q_ref[0, :, u * dv:(u + 1) * dv]) for u in range(ng)]
    lane = lax.broadcasted_iota(jnp.int32, (1, w2), 1)

    def bias(seg, tk):
        return jnp.where(lane < tq, _bias_tile_t(seg, tk, 0)[:, :w2], _bias_tile_t(seg, tk, tq)[:, :w2])

    m = jnp.full((1, ng * w2), -1e30, F32)
    far_shift = jnp.concatenate([jnp.full((1, w2), fb_ref[hg * ng + u], F32) for u in range(ng)], axis=1)
    acc = [jnp.zeros((dv + ONES_ROWS, w2), F32) for _ in range(ng)]
    for c in range(n_past + 1):
        sts, vts = [], []
        for u in range(ng):
            if c < n_past:
                rows = pl.ds(c * tkp * nh + hg * ng + u, tkp, stride=nh)
                k, v = kp_ref[0, rows, :], vp_ref[0, rows, :]
            else:
                k, v = kn_ref[0, :, u * dv:(u + 1) * dv], vn_ref[0, :, u * dv:(u + 1) * dv]
            st = jnp.dot(k.astype(BF16), qst[u], preferred_element_type=F32)
            if far_tiles <= c < n_past:
                st = st + bias(segp_ref[u, c - far_tiles], tkp)
            elif c == n_past:
                st = st + bias(segn_ref[u, 0], tq)
            sts.append(st)
            vts.append(_with_ones(v.T.astype(BF16)))
        m, alpha, p = _online(jnp.concatenate(sts, axis=1), m, far_shift if c < far_tiles else None)
        for u in range(ng):
            sl = slice(u * w2, (u + 1) * w2)
            acc[u] = alpha[:, sl] * acc[u] + jnp.dot(vts[u], p[:, sl], preferred_element_type=F32)
    for u in range(ng):
        on = acc[u][:dv] * (1.0 / acc[u][dv:dv + 1])
        ot = (on - lam_ref[0] * pltpu.roll(on, tq, 1)).T[:tq]
        o_ref[0, :, u * dv:(u + 1) * dv] = _attn_out(ot, g_ref, out_scale).astype(o_ref.dtype)


def _attn_sample(qkv, k_past, v_past, layer, rel_bias, lam, subln_g, lam_init):
    bsz, s, _ = qkv.shape
    h, dv, tkp, g = H_B, 2 * DH_B, ATT_T, ATT_GS
    past = k_past.shape[1] // h
    n_past = past // tkp
    far_tiles = sum(1 for c in range(n_past) if past - (c * tkp + tkp - 1) >= _saturation_distance())
    near = list(range(far_tiles, n_past)) or [n_past - 1]
    segp = _bias_rows(rel_bias, [c * tkp - past + tkp for c in near], 2 * s + tkp)
    segn = _bias_rows(rel_bias, [s], 4 * s)
    pk = lambda: pl.BlockSpec((1, past * h, dv), lambda b, hg: (layer * bsz + b, 0, 0))
    nk = lambda sec: pl.BlockSpec((1, s, g * dv), lambda b, hg, sec=sec: (b, 0, sec * (h // g) + hg))
    return pl.pallas_call(
        functools.partial(_attn_sample_kernel, tkp=tkp, n_past=n_past, far_tiles=far_tiles,
                          out_scale=1.0 - lam_init),
        grid=(bsz, h // g),
        in_specs=[pl.BlockSpec(memory_space=pltpu.SMEM), pl.BlockSpec(memory_space=pltpu.SMEM),
                  nk(0), pk(), pk(),
                  pl.BlockSpec((g, len(near), 1, 2 * s + tkp), lambda b, hg: (hg, 0, 0, 0)),
                  nk(1), nk(2),
                  pl.BlockSpec((g, 1, 1, 4 * s), lambda b, hg: (hg, 0, 0, 0)),
                  pl.BlockSpec((1, dv), lambda b, hg: (0, 0))],
        out_specs=pl.BlockSpec((1, s, g * dv), lambda b, hg: (b, 0, hg)),
        out_shape=jax.ShapeDtypeStruct((bsz, s, h * dv), BF16),
        compiler_params=_cparams(("arbitrary",) * 2),
        name="diff_attn_sample",
    )(lam, _far_bias(rel_bias), qkv, k_past, v_past, segp, qkv, qkv, segn, subln_g.reshape(1, dv))


def _rglru_kernel(xw_ref, cw_ref, cb_ref, wa_ref, ba_ref, wx_ref, bx_ref, lam_ref, cbuf_ref, h0_ref,
                  y_ref, hlast_ref, ext_scr, h_scr):
    t = pl.program_id(1)
    tt = y_ref.shape[1]
    d = y_ref.shape[2]
    pad = ext_scr.shape[0]

    @pl.when(t == 0)
    def _():
        ext_scr[...] = cbuf_ref[0]
        h_scr[...] = h0_ref[0]

    gate = xw_ref[0, :, 0:d]
    u = xw_ref[0, :, d:2 * d]
    hist = ext_scr[...]
    ext_scr[...] = u[tt - pad:tt]
    row = lax.broadcasted_iota(jnp.int32, (tt, d), 0)
    conv = cb_ref[...] + cw_ref[CONV_W - 1:CONV_W] * u
    for j in range(1, CONV_W):
        ru = pltpu.roll(u, j, 0)
        rh = jnp.concatenate([pltpu.roll(hist, j, 0), ru[pad:]], axis=0)
        conv = conv + cw_ref[CONV_W - 1 - j:CONV_W - j] * jnp.where(row < j, rh, ru)

    cb16 = conv.astype(BF16)
    blk = d // N_BLK_C
    ra, rx = [], []
    for n in range(N_BLK_C):
        cs = cb16[:, n * blk:(n + 1) * blk]
        ra.append(jnp.dot(cs, wa_ref[n], preferred_element_type=F32))
        rx.append(jnp.dot(cs, wx_ref[n], preferred_element_type=F32))
    r = _sigmoid(jnp.concatenate(ra, axis=1) + ba_ref[...])
    ig = _sigmoid(jnp.concatenate(rx, axis=1) + bx_ref[...])
    nl = -lam_ref[...]
    sp = jnp.maximum(nl, 0.0) + jnp.log(1.0 + jnp.exp(-jnp.abs(nl)))
    log_a = (-C_RG) * r * sp
    a = jnp.exp(log_a)
    xin = jnp.sqrt(1.0 - a * a) * ig * conv

    ca, cx = a, xin
    sft = 1
    while sft < SCAN_G:
        keep = (row % SCAN_G) >= sft
        pa = jnp.where(keep, pltpu.roll(ca, sft, 0), 1.0)
        px = jnp.where(keep, pltpu.roll(cx, sft, 0), 0.0)
        cx = ca * px + cx
        ca = ca * pa
        sft *= 2
    carry = h_scr[...]
    groups = []
    for gi in range(tt // SCAN_G):
        gs = slice(gi * SCAN_G, (gi + 1) * SCAN_G)
        hg = ca[gs] * carry + cx[gs]
        carry = hg[SCAN_G - 1:SCAN_G]
        groups.append(hg)
    hh = jnp.concatenate(groups, axis=0)
    h_scr[...] = carry
    k1 = -2.0 * math.sqrt(2.0 / math.pi) * LOG2E
    gelu = gate / (1.0 + jnp.exp2(gate * (k1 + (k1 * 0.044715) * (gate * gate))))
    y_ref[0] = (hh * gelu).astype(y_ref.dtype)

    @pl.when(t == pl.num_programs(1) - 1)
    def _():
        hlast_ref[0] = hh[tt - 1:tt]


def _rglru(xw, cbuf8, h0, conv_w, conv_b, w_a, b_a, w_x, b_x, lam, *, rows):
    bsz, s, d2 = xw.shape
    d = d2 // 2
    pad = cbuf8.shape[1]
    blk = d // N_BLK_C
    row = lambda a: a.reshape(1, d).astype(F32)
    full2 = lambda shp: pl.BlockSpec(shp, lambda b, t: (0,) * len(shp))
    return pl.pallas_call(
        _rglru_kernel,
        grid=(bsz, s // rows),
        in_specs=[pl.BlockSpec((1, rows, d2), lambda b, t: (b, t, 0)),
                  full2((CONV_W, d)), full2((1, d)),
                  full2((N_BLK_C, blk, blk)), full2((1, d)),
                  full2((N_BLK_C, blk, blk)), full2((1, d)),
                  full2((1, d)),
                  pl.BlockSpec((1, pad, d), lambda b, t: (b, 0, 0)),
                  pl.BlockSpec((1, 1, d), lambda b, t: (b, 0, 0))],
        out_specs=[pl.BlockSpec((1, rows, d), lambda b, t: (b, t, 0)),
                   pl.BlockSpec((1, 1, d), lambda b, t: (b, 0, 0))],
        out_shape=[jax.ShapeDtypeStruct((bsz, s, d), BF16),
                   jax.ShapeDtypeStruct((bsz, 1, d), F32)],
        scratch_shapes=[pltpu.VMEM((pad, d), F32), pltpu.VMEM((1, d), F32)],
        compiler_params=_cparams(("arbitrary", "arbitrary")),
        name="rglru",
    )(xw, conv_w.astype(F32), row(conv_b), w_a.astype(BF16), row(b_a), w_x.astype(BF16), row(b_x),
      row(lam), cbuf8, h0)


def _route(x, wr_t, rbias):
    ne = wr_t.shape[0]
    tm = x.shape[0]
    gsz = ne // N_GROUPS
    logits = lax.dot_general(wr_t, x, _NT, preferred_element_type=F32)
    score = _sigmoid(logits)
    biased = score + rbias
    neg = -jnp.inf
    sub = lax.broadcasted_iota(jnp.int32, (gsz, tm), 0)
    gscore = []
    for g in range(N_GROUPS):
        vg = biased[g * gsz:(g + 1) * gsz]
        m1 = jnp.max(vg, axis=0, keepdims=True)
        first = jnp.min(jnp.where(vg == m1, sub, gsz), axis=0, keepdims=True)
        m2 = jnp.max(jnp.where(sub == first, neg, vg), axis=0, keepdims=True)
        gscore.append(m1 + m2)
    masked = []
    for g in range(N_GROUPS):
        rank = jnp.zeros((1, tm), jnp.int32)
        for o in range(N_GROUPS):
            if o == g:
                continue
            ahead = (gscore[o] >= gscore[g]) if o < g else (gscore[o] > gscore[g])
            rank = rank + ahead.astype(jnp.int32)
        keep = rank < TOPK_GROUPS
        masked.append(jnp.where(keep, biased[g * gsz:(g + 1) * gsz], neg))
    cur = jnp.concatenate(masked, axis=0)
    eidx = lax.broadcasted_iota(jnp.int32, (ne, tm), 0)
    chosen = jnp.zeros((ne, tm), F32)
    for _ in range(TOP_K):
        m = jnp.max(cur, axis=0, keepdims=True)
        first = jnp.min(jnp.where(cur == m, eidx, ne), axis=0, keepdims=True)
        hit = eidx == first
        chosen = jnp.where(hit, 1.0, chosen)
        cur = jnp.where(hit, neg, cur)
    wsel = chosen * score
    return wsel / jnp.sum(wsel, axis=0, keepdims=True) * ROUTED_SCALE


def _swiglu(h, f):
    hg = h[:, :f]
    return hg * _sigmoid(hg) * h[:, f:]


def _moe_kernel(inv_ref, invd_ref, o_ref, wo_ref, x0_ref, g1_ref, b1_ref, wr_ref, rb_ref, wsi_ref, wsd_ref,
                wi_ref, wd_ref, g_ref, b_ref, out_ref, x_scr, x8_scr, acc_scr, gate_scr, *, eb, layer):
    j = pl.program_id(1)
    f = wsd_ref.shape[0]

    @pl.when(j == 0)
    def _():
        x = _layer_norm(ALPHA * x0_ref[...] + jnp.dot(o_ref[...], wo_ref[...], preferred_element_type=F32),
                        g1_ref[...], b1_ref[...])
        x_scr[...] = x
        xb = x.astype(BF16)
        x8_scr[...] = x.astype(F8)
        w_et = _route(xb, wr_ref[...], rb_ref[...])
        ne, tm = w_et.shape
        gate_scr[...] = jnp.concatenate([w_et, jnp.zeros((gate_scr.shape[1] - ne, tm), F32)], axis=0).T
        sh = _swiglu(jnp.dot(xb, wsi_ref[...], preferred_element_type=F32), f)
        acc_scr[...] = jnp.dot(sh.astype(BF16), wsd_ref[...], preferred_element_type=F32)

    x8 = x8_scr[...]
    gates = gate_scr[...]
    lane = lax.broadcasted_iota(jnp.int32, gates.shape, 1)
    acts = []
    for i in range(eb):
        e = j * eb + i
        inv = inv_ref[layer * (eb * pl.num_programs(1)) + e]
        hs = jnp.dot(x8, wi_ref[i], preferred_element_type=F32)
        hg, hu = hs[:, :f], hs[:, f:]
        gcol = jnp.sum(jnp.where(lane == e, gates, 0.0), axis=1, keepdims=True)
        sig = 1.0 / (1.0 + jnp.exp2(hg * (-LOG2E * inv)))
        act = hg * sig * hu * (gcol * (ACT_SCALE * inv * inv))
        acts.append(jnp.clip(act, -F8_MAX, F8_MAX).astype(F8))
    acc_scr[...] += (jnp.dot(jnp.concatenate(acts, axis=1), wd_ref[0], preferred_element_type=F32)
                     * (invd_ref[layer * pl.num_programs(1) + j] * (1.0 / ACT_SCALE)))

    @pl.when(j == pl.num_programs(1) - 1)
    def _():
        out_ref[...] = _layer_norm(ALPHA * x_scr[...] + acc_scr[...], g_ref[...], b_ref[...])


def _fp8_prep_kernel(w_ref, o_ref, inv_ref):
    for i in range(w_ref.shape[0]):
        w = w_ref[i]
        amax = jnp.max(jnp.max(jnp.abs(w), axis=1, keepdims=True), axis=0, keepdims=True)
        k = jnp.floor(jnp.log2(F8_MAX / jnp.maximum(amax, 1e-30))) - 1.0
        scale = jnp.exp2(jnp.clip(k, -60.0, 60.0))
        o_ref[i] = (w * scale).astype(F8)
        inv_ref[i] = jnp.broadcast_to(1.0 / scale, inv_ref.shape[1:])


def _fp8_weights(w):
    n, r, c = w.shape
    per = max(1, min(n, FP8_PREP_BYTES // (r * c * 4)))
    w8, inv = pl.pallas_call(
        _fp8_prep_kernel,
        grid=(n // per,),
        in_specs=[pl.BlockSpec((per, r, c), lambda i: (i, 0, 0))],
        out_specs=[pl.BlockSpec((per, r, c), lambda i: (i, 0, 0)),
                   pl.BlockSpec((per, 8, 128), lambda i: (i, 0, 0))],
        out_shape=[jax.ShapeDtypeStruct((n, r, c), F8), jax.ShapeDtypeStruct((n, 8, 128), F32)],
        compiler_params=_cparams(("arbitrary",)),
        name="fp8_weights",
    )(w.astype(F32))
    return w8, inv[:, 0, 0]


def _mix_out_moe(o, w_out, x0, g1, b1, w_router, router_bias, w8, inv_scale, wd8, inv_scale_d, ws_in, ws_down,
                 g, b, *, tm, layer):
    t, d = x0.shape
    ne = w_router.shape[1]
    f2 = w8.shape[2]
    f = f2 // 2
    eb = MOE_EB
    nb = ne // eb
    const = lambda shp: pl.BlockSpec(shp, lambda i, j: (0,) * len(shp))
    return pl.pallas_call(
        functools.partial(_moe_kernel, eb=eb, layer=layer),
        grid=(t // tm, nb),
        in_specs=[pl.BlockSpec(memory_space=pltpu.SMEM), pl.BlockSpec(memory_space=pltpu.SMEM),
                  pl.BlockSpec((tm, o.shape[1]), lambda i, j: (i, 0)), const(w_out.shape),
                  pl.BlockSpec((tm, d), lambda i, j: (i, 0)), const((1, d)), const((1, d)),
                  const((ne, d)), const((ne, 1)),
                  const((d, f2)), const((f, d)),
                  pl.BlockSpec((eb, d, f2), lambda i, j: (layer * nb + j, 0, 0)),
                  pl.BlockSpec((1, eb * f, d), lambda i, j: (layer * nb + j, 0, 0)),
                  const((1, d)), const((1, d))],
        out_specs=pl.BlockSpec((tm, d), lambda i, j: (i, 0)),
        out_shape=jax.ShapeDtypeStruct((t, d), F32),
        scratch_shapes=[pltpu.VMEM((tm, d), F32), pltpu.VMEM((tm, d), F8),
                        pltpu.VMEM((tm, d), F32), pltpu.VMEM((tm, 128), F32)],
        compiler_params=_cparams(("arbitrary", "arbitrary")),
        name="moe_ln",
    )(inv_scale, inv_scale_d, o, w_out, x0, g1.reshape(1, d), b1.reshape(1, d),
      w_router.T.astype(BF16), router_bias.reshape(ne, 1).astype(F32),
      ws_in.astype(BF16), ws_down.astype(BF16), w8, wd8,
      g.reshape(1, d), b.reshape(1, d))


def _lambda_init(layer_idx):
    return 0.8 - 0.6 * math.exp(-0.3 * layer_idx)


def kernel(x_prompt, x_sample, state_hgrn_s, cache_k, cache_v, state_rglru_conv, state_rglru_h, hgrn_lb_logits, hgrn_w_in, hgrn_gn_g, hgrn_w_out, rel_bias, attn_w_qkv, attn_lam_q1, attn_lam_k1, attn_lam_q2, attn_lam_k2, attn_subln_g, attn_w_out, rglru_w_in, rglru_conv_w, rglru_conv_b, rglru_w_a, rglru_b_a, rglru_w_x, rglru_b_x, rglru_lam, rglru_w_out, ln1_g, ln1_b, ln2_g, ln2_b, moe_w_router, moe_router_bias, moe_w_in, moe_w_down, moe_ws_in, moe_ws_down):
    bp, sp, d = x_prompt.shape
    bs, ss, _ = x_sample.shape
    lb_p = jax.nn.softmax(hgrn_lb_logits.astype(F32), axis=0)
    lower_bounds = jnp.clip(jnp.cumsum(lb_p, axis=0) - lb_p[0], 0.0, 1.0)

    nl, ne, _, f2 = moe_w_in.shape
    w8, inv_in = _fp8_weights(moe_w_in.reshape(nl * ne, d, f2))
    wd8, inv_down = _fp8_weights(moe_w_down.reshape(nl * ne // MOE_EB, MOE_EB * (f2 // 2), d))

    streams = [(x_prompt.reshape(bp * sp, d), bp, sp), (x_sample.reshape(bs * ss, d), bs, ss)]
    tile = lambda t: 1024 if t % 1024 == 0 else 512
    hs, kvs, cvs, hls = [[], []], [[], []], [[], []], [[], []]
    for i in range(DEPTH):
        j, kind = i // N_MIXERS, i % N_MIXERS
        new_streams = []
        for si, (x, bsz, s) in enumerate(streams):
            t = bsz * s
            tm = tile(t)
            if kind == 0:
                xw = _matmul(x, hgrn_w_in[j].astype(BF16), tm=512)
                s0 = (jnp.zeros((bsz, H_A, d // H_A, d // H_A), F32) if si == 0
                      else state_hgrn_s[j].astype(F32))
                o, s_fin = _hgrn(xw.reshape(bsz, s, -1), s0, lower_bounds[i], hgrn_gn_g[j],
                                 rows=min(s, 256))
                hs[si].append(s_fin)
                w_out = hgrn_w_out[j]
            elif kind == 1:
                qkv = _matmul(x, attn_w_qkv[j].astype(BF16), tm=512)
                hd = H_B * 2 * DH_B
                lam_init = _lambda_init(i)
                lam = (jnp.exp(jnp.sum(attn_lam_q1[j].astype(F32) * attn_lam_k1[j].astype(F32)))
                       - jnp.exp(jnp.sum(attn_lam_q2[j].astype(F32) * attn_lam_k2[j].astype(F32)))
                       + lam_init).reshape(1)
                qkv3 = qkv.reshape(bsz, s, 3 * hd)
                if si == 0:
                    o = _attn_prompt(qkv3, rel_bias, lam, attn_subln_g[j], lam_init)
                else:
                    past = cache_k.shape[2]
                    o = _attn_sample(qkv3, cache_k.reshape(-1, past * H_B, 2 * DH_B),
                                     cache_v.reshape(-1, past * H_B, 2 * DH_B),
                                     j, rel_bias, lam, attn_subln_g[j], lam_init)
                kvs[si].append((qkv3[:, :, hd:2 * hd].reshape(bsz, s, H_B, 2 * DH_B),
                                qkv3[:, :, 2 * hd:].reshape(bsz, s, H_B, 2 * DH_B)))
                w_out = attn_w_out[j]
            else:
                xw = _matmul(x, rglru_w_in[j].astype(BF16), tm=512).reshape(bsz, s, 2 * d)
                if si == 0:
                    cbuf = jnp.zeros((bsz, CONV_W - 1, d), F32)
                    h0 = jnp.zeros((bsz, d), F32)
                else:
                    cbuf, h0 = state_rglru_conv[j].astype(F32), state_rglru_h[j].astype(F32)
                cbuf8 = jnp.pad(cbuf, ((0, 0), (8 - (CONV_W - 1), 0), (0, 0)))
                o, hlast = _rglru(xw, cbuf8, h0.reshape(bsz, 1, d), rglru_conv_w[j], rglru_conv_b[j],
                                  rglru_w_a[j], rglru_b_a[j], rglru_w_x[j], rglru_b_x[j], rglru_lam[j],
                                  rows=min(s, 256))
                cvs[si].append(xw[:, s - (CONV_W - 1):, d:])
                hls[si].append(hlast.reshape(bsz, d))
                w_out = rglru_w_out[j]
            x = _mix_out_moe(o.reshape(t, -1), w_out.astype(BF16), x, ln1_g[i], ln1_b[i],
                             moe_w_router[i], moe_router_bias[i], w8, inv_in, wd8, inv_down,
                             moe_ws_in[i], moe_ws_down[i], ln2_g[i], ln2_b[i], tm=tm, layer=i)
            new_streams.append((x, bsz, s))
        streams = new_streams

    yp = streams[0][0].reshape(bp, sp, d)
    ys = streams[1][0].reshape(bs, ss, d)
    st = lambda xs: jnp.stack(xs)
    return (yp, ys, st(hs[0]), st(hs[1]),
            st([kv[0] for kv in kvs[0]]), st([kv[1] for kv in kvs[0]]),
            st([kv[0] for kv in kvs[1]]), st([kv[1] for kv in kvs[1]]),
            st(cvs[0]), st(hls[0]), st(cvs[1]), st(hls[1]))
```

```python
import functools
import math

import jax
import jax.numpy as jnp
from jax import lax
from jax.experimental import pallas as pl
from jax.experimental.pallas import tpu as pltpu

F32 = jnp.float32
BF16 = jnp.bfloat16
F8 = jnp.float8_e4m3fn
F8_MAX = 448.0
ACT_SCALE = 32.0

DEPTH = 4
CHUNK = 64
N_MIXERS = 3
H_A = 8
H_B = 8
DH_B = 64
REL_BUCKETS = 32
REL_MAX_DIST = 1024
N_BLK_C = 4
CONV_W = 4
C_RG = 8.0
N_EXPERTS = 64
D_EXPERT = 128
TOP_K = 8
N_GROUPS = 8
TOPK_GROUPS = 4
ROUTED_SCALE = 2.5
ALPHA = (2 * DEPTH) ** 0.25
LN_EPS = 1e-5
RMS_EPS = 1e-6

VMEM_LIMIT = 56 * 1024 * 1024
LOG2E = math.log2(math.e)
REC_C = 64
ATT_T = 256
ATT_G = 4
ATT_GS = 8
SCAN_G = 8
ONES_ROWS = 16
MOE_EB = 16
FP8_PREP_BYTES = 8 * 1024 * 1024

_NT = (((1,), (1,)), ((), ()))
_TN = (((0,), (0,)), ((), ()))


def _cparams(sem):
    return pltpu.CompilerParams(dimension_semantics=sem, vmem_limit_bytes=VMEM_LIMIT)


def _sigmoid(x):
    return 1.0 / (1.0 + jnp.exp(-x))


def _layer_norm(z, g, b):
    mu = jnp.mean(z, -1, keepdims=True)
    zc = z - mu
    var = jnp.mean(zc * zc, -1, keepdims=True)
    return zc * lax.rsqrt(var + LN_EPS) * g + b


def _mm_kernel(x_ref, w_ref, o_ref):
    o_ref[...] = jnp.dot(x_ref[...].astype(BF16), w_ref[...],
                         preferred_element_type=F32).astype(o_ref.dtype)


def _matmul(x, w, *, tm, out_dtype=F32):
    t, k = x.shape
    n = w.shape[1]
    return pl.pallas_call(
        _mm_kernel,
        grid=(t // tm,),
        in_specs=[pl.BlockSpec((tm, k), lambda i: (i, 0)),
                  pl.BlockSpec((k, n), lambda i: (0, 0))],
        out_specs=pl.BlockSpec((tm, n), lambda i: (i, 0)),
        out_shape=jax.ShapeDtypeStruct((t, n), out_dtype),
        compiler_params=_cparams(("arbitrary",)),
        name="proj_in",
    )(x, w)


def _qkv_kernel(x_ref, w_ref, o_ref, k_ref, v_ref):
    res = jnp.dot(x_ref[...].astype(BF16), w_ref[...], preferred_element_type=F32)
    o_ref[...] = res
    nh, dh = k_ref.shape[1:]
    for h in range(nh):
        k_ref[:, h, :] = res[:, (nh + h) * dh:(nh + h + 1) * dh]
        v_ref[:, h, :] = res[:, (2 * nh + h) * dh:(2 * nh + h + 1) * dh]


def _qkv_proj(x, w, nh, *, tm):
    t, k = x.shape
    n = w.shape[1]
    dh = n // (3 * nh)
    return pl.pallas_call(
        _qkv_kernel,
        grid=(t // tm,),
        in_specs=[pl.BlockSpec((tm, k), lambda i: (i, 0)),
                  pl.BlockSpec((k, n), lambda i: (0, 0))],
        out_specs=[pl.BlockSpec((tm, n), lambda i: (i, 0)),
                   pl.BlockSpec((tm, nh, dh), lambda i: (i, 0, 0)),
                   pl.BlockSpec((tm, nh, dh), lambda i: (i, 0, 0))],
        out_shape=[jax.ShapeDtypeStruct((t, n), F32), jax.ShapeDtypeStruct((t, nh, dh), F32),
                   jax.ShapeDtypeStruct((t, nh, dh), F32)],
        compiler_params=_cparams(("arbitrary",)),
        name="proj_qkv",
    )(x, w)


def _cumsum_rows(x):
    n = x.shape[0]
    row = lax.broadcasted_iota(jnp.int32, x.shape, 0)
    sft = 1
    while sft < SCAN_G:
        x = x + jnp.where((row % SCAN_G) >= sft, pltpu.roll(x, sft, 0), 0.0)
        sft *= 2
    carry = jnp.zeros((1, x.shape[1]), x.dtype)
    groups = []
    for gi in range(n // SCAN_G):
        xg = x[gi * SCAN_G:(gi + 1) * SCAN_G] + carry
        carry = xg[SCAN_G - 1:SCAN_G]
        groups.append(xg)
    return jnp.concatenate(groups, axis=0)


def _hgrn_kernel(q_ref, fz_ref, v_ref, g_ref, lb_ref, gn_ref, s0_ref, o_ref, sfin_ref, st_scr,
                 *, chunk):
    c = pl.program_id(1)
    nh, dv, dk = st_scr.shape

    @pl.when(c == 0)
    def _():
        for h in range(nh):
            st_scr[h] = s0_ref[0, h].T

    rows = fz_ref.shape[1]
    lb = lb_ref[...]
    causal = (lax.broadcasted_iota(jnp.int32, (chunk, chunk), 0)
              >= lax.broadcasted_iota(jnp.int32, (chunk, chunk), 1))
    sts = [st_scr[h] for h in range(nh)]
    for i in range(rows // chunk):
        sl = slice(i * chunk, (i + 1) * chunk)
        fz = fz_ref[0, sl, :]
        e = jnp.exp(-jnp.abs(fz))
        r = 1.0 / (1.0 + e)
        pos = fz >= 0
        sig = jnp.where(pos, r, e * r)
        nsig = jnp.where(pos, e * r, r)
        logf = jnp.log(lb + (1.0 - lb) * sig)
        kk = (1.0 - lb) * nsig
        b = _cumsum_rows(logf)
        bm = b[chunk // 2 - 1:chunk // 2]
        be = b[chunk - 1:chunk]
        qv = q_ref[0, sl, :]
        a_fac = qv * _sigmoid(qv) * jnp.exp(b - bm)
        k_fac = kk * jnp.exp(bm - b)
        qe = (a_fac * jnp.exp(bm)).astype(BF16)
        kd = (k_fac * jnp.exp(be - bm)).astype(BF16)
        a16 = a_fac.astype(BF16)
        k16 = k_fac.astype(BF16)
        dec = jnp.exp(be)
        v = v_ref[0, sl, :].astype(BF16)
        gv = g_ref[0, sl, :]
        gate = gv * _sigmoid(gv)
        for h in range(nh):
            hs = slice(h * dk, (h + 1) * dk)
            st = sts[h]
            o = lax.dot_general(qe[:, hs], st.astype(BF16), _NT, preferred_element_type=F32)
            sc = lax.dot_general(a16[:, hs], k16[:, hs], _NT, preferred_element_type=F32)
            sc = jnp.where(causal, sc, 0.0)
            o = o + jnp.dot(sc.astype(BF16), v[:, hs], preferred_element_type=F32)
            sts[h] = st * dec[:, hs] + lax.dot_general(v[:, hs], kd[:, hs], _TN,
                                                       preferred_element_type=F32)
            o = o * lax.rsqrt(jnp.mean(o * o, -1, keepdims=True) + RMS_EPS) * gn_ref[...] * gate[:, hs]
            o_ref[0, sl, hs] = o.astype(o_ref.dtype)
    for h in range(nh):
        st_scr[h] = sts[h]

    @pl.when(c == pl.num_programs(1) - 1)
    def _():
        for h in range(nh):
            sfin_ref[0, h] = st_scr[h].T


def _hgrn(xw, s0, lb, gn_g, *, rows):
    bsz, s, _ = xw.shape
    h, dk, dv = s0.shape[1:]
    sec = lambda k: pl.BlockSpec((1, rows, h * dk), lambda b, c, k=k: (b, c, k))
    return pl.pallas_call(
        functools.partial(_hgrn_kernel, chunk=REC_C),
        grid=(bsz, s // rows),
        in_specs=[sec(0), sec(1), sec(2), sec(3),
                  pl.BlockSpec((1, h * dk), lambda b, c: (0, 0)),
                  pl.BlockSpec((1, dv), lambda b, c: (0, 0)),
                  pl.BlockSpec((1, h, dk, dv), lambda b, c: (b, 0, 0, 0))],
        out_specs=[pl.BlockSpec((1, rows, h * dv), lambda b, c: (b, c, 0)),
                   pl.BlockSpec((1, h, dk, dv), lambda b, c: (b, 0, 0, 0))],
        out_shape=[jax.ShapeDtypeStruct((bsz, s, h * dv), BF16),
                   jax.ShapeDtypeStruct((bsz, h, dk, dv), F32)],
        scratch_shapes=[pltpu.VMEM((h, dv, dk), F32)],
        compiler_params=_cparams(("arbitrary", "arbitrary")),
        name="hgrn2",
    )(xw, xw, xw, xw, lb.reshape(1, h * dk), gn_g.reshape(1, dv), s0)


def _rel_bucket(rel):
    nb = REL_BUCKETS // 2
    max_exact = nb // 2
    ret = jnp.where(rel > 0, nb, 0)
    n = jnp.abs(rel)
    nf = jnp.maximum(n, 1).astype(F32)
    large = max_exact + (jnp.log(nf / max_exact) / math.log(REL_MAX_DIST / max_exact)
                         * (nb - max_exact)).astype(jnp.int32)
    large = jnp.minimum(large, nb - 1)
    return ret + jnp.where(n < max_exact, n, large)


def _saturation_distance():
    nb = REL_BUCKETS // 2
    max_exact = nb // 2
    return math.ceil(max_exact * (REL_MAX_DIST / max_exact) ** ((nb - 1 - max_exact) / (nb - max_exact))) + 1


def _bias_rows(rel_bias, rel0s, width):
    rel = jnp.asarray(rel0s, jnp.int32)[:, None] - jnp.arange(width, dtype=jnp.int32)[None, :]
    seg = rel_bias.astype(F32)[_rel_bucket(rel)] * LOG2E
    return jnp.transpose(seg, (2, 0, 1))[:, :, None, :]


def _far_bias(rel_bias):
    return rel_bias.astype(F32)[_rel_bucket(jnp.int32(-_saturation_distance()))] * LOG2E


def _bias_tile_t(seg, tk, shift):
    w = seg.shape[1]
    return pltpu.roll(jnp.broadcast_to(seg, (tk, w)), w - tk + shift, 1, stride=1, stride_axis=0)


def _stack_q(q):
    q = q * (DH_B ** -0.5 * LOG2E)
    lane = lax.broadcasted_iota(jnp.int32, q.shape, 1)
    qs = jnp.concatenate([jnp.where(lane < DH_B, q, 0.0), jnp.where(lane >= DH_B, q, 0.0)], axis=0)
    return qs.T.astype(BF16)


def _online(st, m, shift=None):
    if shift is None:
        m_new = jnp.maximum(m, jnp.max(st, axis=0, keepdims=True))
        p = jnp.exp2(st - m_new)
    else:
        m_raw = jnp.maximum(m - shift, jnp.max(st, axis=0, keepdims=True))
        p = jnp.exp2(st - m_raw)
        m_new = m_raw + shift
    return m_new, jnp.exp2(m - m_new), p.astype(BF16)


def _with_ones(vt):
    return jnp.concatenate([vt, jnp.ones((ONES_ROWS, vt.shape[1]), vt.dtype)], axis=0)


def _attn_out(ot, g_ref, out_scale):
    return ot * lax.rsqrt(jnp.mean(ot * ot, -1, keepdims=True) + RMS_EPS) * g_ref[...] * out_scale


def _attn_prompt_kernel(lam_ref, fb_ref, q_ref, k_ref, v_ref, seg_ref, g_ref, o_ref,
                        kb_scr, vt_scr, qst_scr, acc_scr, *, d_far, out_scale):
    hg = pl.program_id(1)
    qi = pl.program_id(2)
    ng, n, t, dv = kb_scr.shape
    w2 = 2 * t

    @pl.when(qi == 0)
    def _():
        for u in range(ng):
            for c in range(n):
                kb_scr[u, c] = k_ref[0, c * t:(c + 1) * t, u * dv:(u + 1) * dv].astype(BF16)
                vt_scr[u, c] = _with_ones(v_ref[0, c * t:(c + 1) * t, u * dv:(u + 1) * dv].T.astype(BF16))

    for u in range(ng):
        qst_scr[u] = _stack_q(q_ref[0, :, u * dv:(u + 1) * dv])
    acc_scr[...] = jnp.zeros(acc_scr.shape, F32)
    far_shift = jnp.concatenate([jnp.full((1, w2), fb_ref[hg * ng + u], F32) for u in range(ng)], axis=1)

    def scores(ki, u):
        return jnp.dot(kb_scr[u, ki], qst_scr[u], preferred_element_type=F32)

    def biased(st, seg, vis=None):
        bt = _bias_tile_t(seg, t, 0)[:, :t]
        s1, s2 = st[:, :t] + bt, st[:, t:] + bt
        if vis is not None:
            s1, s2 = jnp.where(vis, s1, -1e30), jnp.where(vis, s2, -1e30)
        return jnp.concatenate([s1, s2], axis=1)

    def update(ki, m, sts, shift=None):
        m, alpha, p = _online(jnp.concatenate(sts, axis=1), m, shift)
        for u in range(ng):
            sl = slice(u * w2, (u + 1) * w2)
            acc_scr[u] = alpha[:, sl] * acc_scr[u] + jnp.dot(vt_scr[u, ki], p[:, sl],
                                                             preferred_element_type=F32)
        return m

    def far_scores(ki):
        return [scores(ki, u) for u in range(ng)]

    def near_scores(ki):
        return [biased(scores(ki, u), seg_ref[u, qi - ki]) for u in range(ng)]

    def tiles(lo, hi, tile_scores, shift, m):
        def two(i, m):
            ka = lo + 2 * i
            sa, sb = tile_scores(ka), tile_scores(ka + 1)
            return update(ka + 1, update(ka, m, sa, shift), sb, shift)
        n2 = (hi - lo) // 2
        m = lax.fori_loop(0, n2, two, m)
        return lax.fori_loop(lo + 2 * n2, hi, lambda ki, m: update(ki, m, tile_scores(ki), shift), m)

    n_far = jnp.maximum(qi - (d_far - 1), 0)
    m = tiles(0, n_far, far_scores, far_shift, jnp.full((1, ng * w2), -1e30, F32))
    m = tiles(n_far, qi, near_scores, None, m)
    vis = (lax.broadcasted_iota(jnp.int32, (t, t), 0) // CHUNK
           <= lax.broadcasted_iota(jnp.int32, (t, t), 1) // CHUNK)
    update(qi, m, [biased(scores(qi, u), seg_ref[u, 0], vis) for u in range(ng)])
    for u in range(ng):
        acc = acc_scr[u]
        on = acc[:dv] * (1.0 / acc[dv:dv + 1])
        ot = (on[:, :t] - lam_ref[0] * on[:, t:]).T
        o_ref[0, :, u * dv:(u + 1) * dv] = _attn_out(ot, g_ref, out_scale).astype(o_ref.dtype)


def _attn_prompt(qkv, rel_bias, lam, subln_g, lam_init):
    bsz, s, _ = qkv.shape
    h, dv, t, g = H_B, 2 * DH_B, ATT_T, ATT_G
    n = s // t
    w = 2 * t
    d_far = -(-(_saturation_distance() - 1) // t) + 1
    segs = _bias_rows(rel_bias, [t - d * t for d in range(n)], w)
    kv = lambda sec: pl.BlockSpec((1, s, g * dv), lambda b, hg, qi, sec=sec: (b, 0, sec * (h // g) + hg))
    return pl.pallas_call(
        functools.partial(_attn_prompt_kernel, d_far=d_far, out_scale=1.0 - lam_init),
        grid=(bsz, h // g, n),
        in_specs=[pl.BlockSpec(memory_space=pltpu.SMEM), pl.BlockSpec(memory_space=pltpu.SMEM),
                  pl.BlockSpec((1, t, g * dv), lambda b, hg, qi: (b, qi, hg)),
                  kv(1), kv(2),
                  pl.BlockSpec((g, n, 1, w), lambda b, hg, qi: (hg, 0, 0, 0)),
                  pl.BlockSpec((1, dv), lambda b, hg, qi: (0, 0))],
        out_specs=pl.BlockSpec((1, t, g * dv), lambda b, hg, qi: (b, qi, hg)),
        out_shape=jax.ShapeDtypeStruct((bsz, s, h * dv), BF16),
        scratch_shapes=[pltpu.VMEM((g, n, t, dv), BF16), pltpu.VMEM((g, n, dv + ONES_ROWS, t), BF16),
                        pltpu.VMEM((g, dv, 2 * t), BF16), pltpu.VMEM((g, dv + ONES_ROWS, 2 * t), F32)],
        compiler_params=_cparams(("arbitrary",) * 3),
        name="diff_attn_prompt",
    )(lam, _far_bias(rel_bias), qkv, qkv, qkv, segs, subln_g.reshape(1, dv))


def _attn_sample_kernel(lam_ref, fb_ref, q_ref, kp_ref, vp_ref, segp_ref, kn_ref, vn_ref, segn_ref,
                        g_ref, o_ref, *, tkp, n_past, far_tiles, out_scale):
    hg = pl.program_id(1)
    tq = q_ref.shape[1]
    dv = g_ref.shape[1]
    ng = q_ref.shape[2] // dv
    nh = kp_ref.shape[1] // (n_past * tkp)
    w2 = 2 * tq
    qst = [_stack_q(q_ref[0, :, u * dv:(u + 1) * dv]) for u in range(ng)]
    lane = lax.broadcasted_iota(jnp.int32, (1, w2), 1)

    def bias(seg, tk):
        return jnp.where(lane < tq, _bias_tile_t(seg, tk, 0)[:, :w2], _bias_tile_t(seg, tk, tq)[:, :w2])

    m = jnp.full((1, ng * w2), -1e30, F32)
    far_shift = jnp.concatenate([jnp.full((1, w2), fb_ref[hg * ng + u], F32) for u in range(ng)], axis=1)
    acc = [jnp.zeros((dv + ONES_ROWS, w2), F32) for _ in range(ng)]
    for c in range(n_past + 1):
        sts, vts = [], []
        for u in range(ng):
            if c < n_past:
                rows = pl.ds(c * tkp * nh + hg * ng + u, tkp, stride=nh)
                k, v = kp_ref[0, rows, :], vp_ref[0, rows, :]
            else:
                k, v = kn_ref[0, :, u * dv:(u + 1) * dv], vn_ref[0, :, u * dv:(u + 1) * dv]
            st = jnp.dot(k.astype(BF16), qst[u], preferred_element_type=F32)
            if far_tiles <= c < n_past:
                st = st + bias(segp_ref[u, c - far_tiles], tkp)
            elif c == n_past:
                st = st + bias(segn_ref[u, 0], tq)
            sts.append(st)
            vts.append(_with_ones(v.T.astype(BF16)))
        m, alpha, p = _online(jnp.concatenate(sts, axis=1), m, far_shift if c < far_tiles else None)
        for u in range(ng):
            sl = slice(u * w2, (u + 1) * w2)
            acc[u] = alpha[:, sl] * acc[u] + jnp.dot(vts[u], p[:, sl], preferred_element_type=F32)
    for u in range(ng):
        on = acc[u][:dv] * (1.0 / acc[u][dv:dv + 1])
        ot = (on - lam_ref[0] * pltpu.roll(on, tq, 1)).T[:tq]
        o_ref[0, :, u * dv:(u + 1) * dv] = _attn_out(ot, g_ref, out_scale).astype(o_ref.dtype)


def _attn_sample(qkv, k_past, v_past, layer, rel_bias, lam, subln_g, lam_init):
    bsz, s, _ = qkv.shape
    h, dv, tkp, g = H_B, 2 * DH_B, ATT_T, ATT_GS
    past = k_past.shape[1] // h
    n_past = past // tkp
    far_tiles = sum(1 for c in range(n_past) if past - (c * tkp + tkp - 1) >= _saturation_distance())
    near = list(range(far_tiles, n_past)) or [n_past - 1]
    segp = _bias_rows(rel_bias, [c * tkp - past + tkp for c in near], 2 * s + tkp)
    segn = _bias_rows(rel_bias, [s], 4 * s)
    pk = lambda: pl.BlockSpec((1, past * h, dv), lambda b, hg: (layer * bsz + b, 0, 0))
    nk = lambda sec: pl.BlockSpec((1, s, g * dv), lambda b, hg, sec=sec: (b, 0, sec * (h // g) + hg))
    return pl.pallas_call(
        functools.partial(_attn_sample_kernel, tkp=tkp, n_past=n_past, far_tiles=far_tiles,
                          out_scale=1.0 - lam_init),
        grid=(bsz, h // g),
        in_specs=[pl.BlockSpec(memory_space=pltpu.SMEM), pl.BlockSpec(memory_space=pltpu.SMEM),
                  nk(0), pk(), pk(),
                  pl.BlockSpec((g, len(near), 1, 2 * s + tkp), lambda b, hg: (hg, 0, 0, 0)),
                  nk(1), nk(2),
                  pl.BlockSpec((g, 1, 1, 4 * s), lambda b, hg: (hg, 0, 0, 0)),
                  pl.BlockSpec((1, dv), lambda b, hg: (0, 0))],
        out_specs=pl.BlockSpec((1, s, g * dv), lambda b, hg: (b, 0, hg)),
        out_shape=jax.ShapeDtypeStruct((bsz, s, h * dv), BF16),
        compiler_params=_cparams(("arbitrary",) * 2),
        name="diff_attn_sample",
    )(lam, _far_bias(rel_bias), qkv, k_past, v_past, segp, qkv, qkv, segn, subln_g.reshape(1, dv))


def _rglru_kernel(xw_ref, cw_ref, cb_ref, wa_ref, ba_ref, wx_ref, bx_ref, lam_ref, cbuf_ref, h0_ref,
                  y_ref, hlast_ref, ext_scr, h_scr):
    t = pl.program_id(1)
    tt = y_ref.shape[1]
    d = y_ref.shape[2]
    pad = ext_scr.shape[0]

    @pl.when(t == 0)
    def _():
        ext_scr[...] = cbuf_ref[0]
        h_scr[...] = h0_ref[0]

    gate = xw_ref[0, :, 0:d]
    u = xw_ref[0, :, d:2 * d]
    hist = ext_scr[...]
    ext_scr[...] = u[tt - pad:tt]
    row = lax.broadcasted_iota(jnp.int32, (tt, d), 0)
    conv = cb_ref[...] + cw_ref[CONV_W - 1:CONV_W] * u
    for j in range(1, CONV_W):
        ru = pltpu.roll(u, j, 0)
        rh = jnp.concatenate([pltpu.roll(hist, j, 0), ru[pad:]], axis=0)
        conv = conv + cw_ref[CONV_W - 1 - j:CONV_W - j] * jnp.where(row < j, rh, ru)

    cb16 = conv.astype(BF16)
    blk = d // N_BLK_C
    ra, rx = [], []
    for n in range(N_BLK_C):
        cs = cb16[:, n * blk:(n + 1) * blk]
        ra.append(jnp.dot(cs, wa_ref[n], preferred_element_type=F32))
        rx.append(jnp.dot(cs, wx_ref[n], preferred_element_type=F32))
    r = _sigmoid(jnp.concatenate(ra, axis=1) + ba_ref[...])
    ig = _sigmoid(jnp.concatenate(rx, axis=1) + bx_ref[...])
    nl = -lam_ref[...]
    sp = jnp.maximum(nl, 0.0) + jnp.log(1.0 + jnp.exp(-jnp.abs(nl)))
    log_a = (-C_RG) * r * sp
    a = jnp.exp(log_a)
    xin = jnp.sqrt(1.0 - a * a) * ig * conv

    ca, cx = a, xin
    sft = 1
    while sft < SCAN_G:
        keep = (row % SCAN_G) >= sft
        pa = jnp.where(keep, pltpu.roll(ca, sft, 0), 1.0)
        px = jnp.where(keep, pltpu.roll(cx, sft, 0), 0.0)
        cx = ca * px + cx
        ca = ca * pa
        sft *= 2
    carry = h_scr[...]
    groups = []
    for gi in range(tt // SCAN_G):
        gs = slice(gi * SCAN_G, (gi + 1) * SCAN_G)
        hg = ca[gs] * carry + cx[gs]
        carry = hg[SCAN_G - 1:SCAN_G]
        groups.append(hg)
    hh = jnp.concatenate(groups, axis=0)
    h_scr[...] = carry
    k1 = -2.0 * math.sqrt(2.0 / math.pi) * LOG2E
    gelu = gate / (1.0 + jnp.exp2(gate * (k1 + (k1 * 0.044715) * (gate * gate))))
    y_ref[0] = (hh * gelu).astype(y_ref.dtype)

    @pl.when(t == pl.num_programs(1) - 1)
    def _():
        hlast_ref[0] = hh[tt - 1:tt]


def _rglru(xw, cbuf8, h0, conv_w, conv_b, w_a, b_a, w_x, b_x, lam, *, rows):
    bsz, s, d2 = xw.shape
    d = d2 // 2
    pad = cbuf8.shape[1]
    blk = d // N_BLK_C
    row = lambda a: a.reshape(1, d).astype(F32)
    full2 = lambda shp: pl.BlockSpec(shp, lambda b, t: (0,) * len(shp))
    return pl.pallas_call(
        _rglru_kernel,
        grid=(bsz, s // rows),
        in_specs=[pl.BlockSpec((1, rows, d2), lambda b, t: (b, t, 0)),
                  full2((CONV_W, d)), full2((1, d)),
                  full2((N_BLK_C, blk, blk)), full2((1, d)),
                  full2((N_BLK_C, blk, blk)), full2((1, d)),
                  full2((1, d)),
                  pl.BlockSpec((1, pad, d), lambda b, t: (b, 0, 0)),
                  pl.BlockSpec((1, 1, d), lambda b, t: (b, 0, 0))],
        out_specs=[pl.BlockSpec((1, rows, d), lambda b, t: (b, t, 0)),
                   pl.BlockSpec((1, 1, d), lambda b, t: (b, 0, 0))],
        out_shape=[jax.ShapeDtypeStruct((bsz, s, d), BF16),
                   jax.ShapeDtypeStruct((bsz, 1, d), F32)],
        scratch_shapes=[pltpu.VMEM((pad, d), F32), pltpu.VMEM((1, d), F32)],
        compiler_params=_cparams(("arbitrary", "arbitrary")),
        name="rglru",
    )(xw, conv_w.astype(F32), row(conv_b), w_a.astype(BF16), row(b_a), w_x.astype(BF16), row(b_x),
      row(lam), cbuf8, h0)


def _route(x, wr_t, rbias):
    ne = wr_t.shape[0]
    tm = x.shape[0]
    gsz = ne // N_GROUPS
    logits = lax.dot_general(wr_t, x, _NT, preferred_element_type=F32)
    score = _sigmoid(logits)
    biased = score + rbias
    neg = -jnp.inf
    sub = lax.broadcasted_iota(jnp.int32, (gsz, tm), 0)
    gscore = []
    for g in range(N_GROUPS):
        vg = biased[g * gsz:(g + 1) * gsz]
        m1 = jnp.max(vg, axis=0, keepdims=True)
        first = jnp.min(jnp.where(vg == m1, sub, gsz), axis=0, keepdims=True)
        m2 = jnp.max(jnp.where(sub == first, neg, vg), axis=0, keepdims=True)
        gscore.append(m1 + m2)
    masked = []
    for g in range(N_GROUPS):
        rank = jnp.zeros((1, tm), jnp.int32)
        for o in range(N_GROUPS):
            if o == g:
                continue
            ahead = (gscore[o] >= gscore[g]) if o < g else (gscore[o] > gscore[g])
            rank = rank + ahead.astype(jnp.int32)
        keep = rank < TOPK_GROUPS
        masked.append(jnp.where(keep, biased[g * gsz:(g + 1) * gsz], neg))
    cur = jnp.concatenate(masked, axis=0)
    eidx = lax.broadcasted_iota(jnp.int32, (ne, tm), 0)
    chosen = jnp.zeros((ne, tm), F32)
    for _ in range(TOP_K):
        m = jnp.max(cur, axis=0, keepdims=True)
        first = jnp.min(jnp.where(cur == m, eidx, ne), axis=0, keepdims=True)
        hit = eidx == first
        chosen = jnp.where(hit, 1.0, chosen)
        cur = jnp.where(hit, neg, cur)
    wsel = chosen * score
    return wsel / jnp.sum(wsel, axis=0, keepdims=True) * ROUTED_SCALE


def _swiglu(h, f):
    hg = h[:, :f]
    return hg * _sigmoid(hg) * h[:, f:]


def _moe_kernel(inv_ref, invd_ref, o_ref, wo_ref, x0_ref, g1_ref, b1_ref, wr_ref, rb_ref, wsi_ref, wsd_ref,
                wi_ref, wd_ref, g_ref, b_ref, out_ref, x_scr, x8_scr, acc_scr, gate_scr, *, eb, layer):
    j = pl.program_id(1)
    f = wsd_ref.shape[0]

    @pl.when(j == 0)
    def _():
        x = _layer_norm(ALPHA * x0_ref[...] + jnp.dot(o_ref[...], wo_ref[...], preferred_element_type=F32),
                        g1_ref[...], b1_ref[...])
        x_scr[...] = x
        xb = x.astype(BF16)
        x8_scr[...] = x.astype(F8)
        w_et = _route(xb, wr_ref[...], rb_ref[...])
        ne, tm = w_et.shape
        gate_scr[...] = jnp.concatenate([w_et, jnp.zeros((gate_scr.shape[1] - ne, tm), F32)], axis=0).T
        sh = _swiglu(jnp.dot(xb, wsi_ref[...], preferred_element_type=F32), f)
        acc_scr[...] = jnp.dot(sh.astype(BF16), wsd_ref[...], preferred_element_type=F32)

    x8 = x8_scr[...]
    gates = gate_scr[...]
    lane = lax.broadcasted_iota(jnp.int32, gates.shape, 1)
    acts = []
    for i in range(eb):
        e = j * eb + i
        inv = inv_ref[layer * (eb * pl.num_programs(1)) + e]
        hs = jnp.dot(x8, wi_ref[i], preferred_element_type=F32)
        hg, hu = hs[:, :f], hs[:, f:]
        gcol = jnp.sum(jnp.where(lane == e, gates, 0.0), axis=1, keepdims=True)
        sig = 1.0 / (1.0 + jnp.exp2(hg * (-LOG2E * inv)))
        act = hg * sig * hu * (gcol * (ACT_SCALE * inv * inv))
        acts.append(jnp.clip(act, -F8_MAX, F8_MAX).astype(F8))
    acc_scr[...] += (jnp.dot(jnp.concatenate(acts, axis=1), wd_ref[0], preferred_element_type=F32)
                     * (invd_ref[layer * pl.num_programs(1) + j] * (1.0 / ACT_SCALE)))

    @pl.when(j == pl.num_programs(1) - 1)
    def _():
        out_ref[...] = _layer_norm(ALPHA * x_scr[...] + acc_scr[...], g_ref[...], b_ref[...])


def _fp8_prep_kernel(w_ref, o_ref, inv_ref):
    for i in range(w_ref.shape[0]):
        w = w_ref[i]
        amax = jnp.max(jnp.max(jnp.abs(w), axis=1, keepdims=True), axis=0, keepdims=True)
        k = jnp.floor(jnp.log2(F8_MAX / jnp.maximum(amax, 1e-30))) - 1.0
        scale = jnp.exp2(jnp.clip(k, -60.0, 60.0))
        o_ref[i] = (w * scale).astype(F8)
        inv_ref[i] = jnp.broadcast_to(1.0 / scale, inv_ref.shape[1:])


def _fp8_weights(w):
    n, r, c = w.shape
    per = max(1, min(n, FP8_PREP_BYTES // (r * c * 4)))
    w8, inv = pl.pallas_call(
        _fp8_prep_kernel,
        grid=(n // per,),
        in_specs=[pl.BlockSpec((per, r, c), lambda i: (i, 0, 0))],
        out_specs=[pl.BlockSpec((per, r, c), lambda i: (i, 0, 0)),
                   pl.BlockSpec((per, 8, 128), lambda i: (i, 0, 0))],
        out_shape=[jax.ShapeDtypeStruct((n, r, c), F8), jax.ShapeDtypeStruct((n, 8, 128), F32)],
        compiler_params=_cparams(("arbitrary",)),
        name="fp8_weights",
    )(w.astype(F32))
    return w8, inv[:, 0, 0]


def _mix_out_moe(o, w_out, x0, g1, b1, w_router, router_bias, w8, inv_scale, wd8, inv_scale_d, ws_in, ws_down,
                 g, b, *, tm, layer):
    t, d = x0.shape
    ne = w_router.shape[1]
    f2 = w8.shape[2]
    f = f2 // 2
    eb = MOE_EB
    nb = ne // eb
    const = lambda shp: pl.BlockSpec(shp, lambda i, j: (0,) * len(shp))
    return pl.pallas_call(
        functools.partial(_moe_kernel, eb=eb, layer=layer),
        grid=(t // tm, nb),
        in_specs=[pl.BlockSpec(memory_space=pltpu.SMEM), pl.BlockSpec(memory_space=pltpu.SMEM),
                  pl.BlockSpec((tm, o.shape[1]), lambda i, j: (i, 0)), const(w_out.shape),
                  pl.BlockSpec((tm, d), lambda i, j: (i, 0)), const((1, d)), const((1, d)),
                  const((ne, d)), const((ne, 1)),
                  const((d, f2)), const((f, d)),
                  pl.BlockSpec((eb, d, f2), lambda i, j: (layer * nb + j, 0, 0)),
                  pl.BlockSpec((1, eb * f, d), lambda i, j: (layer * nb + j, 0, 0)),
                  const((1, d)), const((1, d))],
        out_specs=pl.BlockSpec((tm, d), lambda i, j: (i, 0)),
        out_shape=jax.ShapeDtypeStruct((t, d), F32),
        scratch_shapes=[pltpu.VMEM((tm, d), F32), pltpu.VMEM((tm, d), F8),
                        pltpu.VMEM((tm, d), F32), pltpu.VMEM((tm, 128), F32)],
        compiler_params=_cparams(("arbitrary", "arbitrary")),
        name="moe_ln",
    )(inv_scale, inv_scale_d, o, w_out, x0, g1.reshape(1, d), b1.reshape(1, d),
      w_router.T.astype(BF16), router_bias.reshape(ne, 1).astype(F32),
      ws_in.astype(BF16), ws_down.astype(BF16), w8, wd8,
      g.reshape(1, d), b.reshape(1, d))


def _lambda_init(layer_idx):
    return 0.8 - 0.6 * math.exp(-0.3 * layer_idx)


def kernel(x_prompt, x_sample, state_hgrn_s, cache_k, cache_v, state_rglru_conv, state_rglru_h, hgrn_lb_logits, hgrn_w_in, hgrn_gn_g, hgrn_w_out, rel_bias, attn_w_qkv, attn_lam_q1, attn_lam_k1, attn_lam_q2, attn_lam_k2, attn_subln_g, attn_w_out, rglru_w_in, rglru_conv_w, rglru_conv_b, rglru_w_a, rglru_b_a, rglru_w_x, rglru_b_x, rglru_lam, rglru_w_out, ln1_g, ln1_b, ln2_g, ln2_b, moe_w_router, moe_router_bias, moe_w_in, moe_w_down, moe_ws_in, moe_ws_down):
    bp, sp, d = x_prompt.shape
    bs, ss, _ = x_sample.shape
    lb_p = jax.nn.softmax(hgrn_lb_logits.astype(F32), axis=0)
    lower_bounds = jnp.clip(jnp.cumsum(lb_p, axis=0) - lb_p[0], 0.0, 1.0)

    nl, ne, _, f2 = moe_w_in.shape
    w8, inv_in = _fp8_weights(moe_w_in.reshape(nl * ne, d, f2))
    wd8, inv_down = _fp8_weights(moe_w_down.reshape(nl * ne // MOE_EB, MOE_EB * (f2 // 2), d))

    streams = [(x_prompt.reshape(bp * sp, d), bp, sp), (x_sample.reshape(bs * ss, d), bs, ss)]
    tile = lambda t: 1024 if t % 1024 == 0 else 512
    hs, kvs, cvs, hls = [[], []], [[], []], [[], []], [[], []]
    for i in range(DEPTH):
        j, kind = i // N_MIXERS, i % N_MIXERS
        new_streams = []
        for si, (x, bsz, s) in enumerate(streams):
            t = bsz * s
            tm = tile(t)
            if kind == 0:
                xw = _matmul(x, hgrn_w_in[j].astype(BF16), tm=512)
                s0 = (jnp.zeros((bsz, H_A, d // H_A, d // H_A), F32) if si == 0
                      else state_hgrn_s[j].astype(F32))
                o, s_fin = _hgrn(xw.reshape(bsz, s, -1), s0, lower_bounds[i], hgrn_gn_g[j],
                                 rows=min(s, 256))
                hs[si].append(s_fin)
                w_out = hgrn_w_out[j]
            elif kind == 1:
                qkv, k_new, v_new = _qkv_proj(x, attn_w_qkv[j].astype(BF16), H_B, tm=512)
                hd = H_B * 2 * DH_B
                lam_init = _lambda_init(i)
                lam = (jnp.exp(jnp.sum(attn_lam_q1[j].astype(F32) * attn_lam_k1[j].astype(F32)))
                       - jnp.exp(jnp.sum(attn_lam_q2[j].astype(F32) * attn_lam_k2[j].astype(F32)))
                       + lam_init).reshape(1)
                qkv3 = qkv.reshape(bsz, s, 3 * hd)
                if si == 0:
                    o = _attn_prompt(qkv3, rel_bias, lam, attn_subln_g[j], lam_init)
                else:
                    past = cache_k.shape[2]
                    o = _attn_sample(qkv3, cache_k.reshape(-1, past * H_B, 2 * DH_B),
                                     cache_v.reshape(-1, past * H_B, 2 * DH_B),
                                     j, rel_bias, lam, attn_subln_g[j], lam_init)
                kvs[si].append((k_new.reshape(bsz, s, H_B, 2 * DH_B), v_new.reshape(bsz, s, H_B, 2 * DH_B)))
                w_out = attn_w_out[j]
            else:
                xw = _matmul(x, rglru_w_in[j].astype(BF16), tm=512).reshape(bsz, s, 2 * d)
                if si == 0:
                    cbuf = jnp.zeros((bsz, CONV_W - 1, d), F32)
                    h0 = jnp.zeros((bsz, d), F32)
                else:
                    cbuf, h0 = state_rglru_conv[j].astype(F32), state_rglru_h[j].astype(F32)
                cbuf8 = jnp.pad(cbuf, ((0, 0), (8 - (CONV_W - 1), 0), (0, 0)))
                o, hlast = _rglru(xw, cbuf8, h0.reshape(bsz, 1, d), rglru_conv_w[j], rglru_conv_b[j],
                                  rglru_w_a[j], rglru_b_a[j], rglru_w_x[j], rglru_b_x[j], rglru_lam[j],
                                  rows=min(s, 256))
                cvs[si].append(xw[:, s - (CONV_W - 1):, d:])
                hls[si].append(hlast.reshape(bsz, d))
                w_out = rglru_w_out[j]
            x = _mix_out_moe(o.reshape(t, -1), w_out.astype(BF16), x, ln1_g[i], ln1_b[i],
                             moe_w_router[i], moe_router_bias[i], w8, inv_in, wd8, inv_down,
                             moe_ws_in[i], moe_ws_down[i], ln2_g[i], ln2_b[i], tm=tm, layer=i)
            new_streams.append((x, bsz, s))
        streams = new_streams

    yp = streams[0][0].reshape(bp, sp, d)
    ys = streams[1][0].reshape(bs, ss, d)
    st = lambda xs: jnp.stack(xs)
    return (yp, ys, st(hs[0]), st(hs[1]),
            st([kv[0] for kv in kvs[0]]), st([kv[1] for kv in kvs[0]]),
            st([kv[0] for kv in kvs[1]]), st([kv[1] for kv in kvs[1]]),
            st(cvs[0]), st(hls[0]), st(cvs[1]), st(hls[1]))
```
